```python
import math
import jax, jax.numpy as jnp
from jax import lax
import numpy as np

D_MODEL = 1024
BATCH = 1
SEQ = 16384
DEPTH = 2
DEC_BATCH = 32
DEC_SEQ = 4
PAST_LEN = 16384
PAGE_SIZE = 128

N_MIXERS = 2
N_ATTN_LAYERS = (DEPTH + 1) // 2
N_SSM_LAYERS = DEPTH // 2

N_HEADS = 16
HEAD_DIM = D_MODEL // N_HEADS
MOBA_BLOCK = 256
MOBA_TOPK = 3
Q_BLOCK = 128
REL_BUCKETS = 32
REL_MAX_DIST = 128

SSM_EXPAND = 2
D_INNER = SSM_EXPAND * D_MODEL
SSM_HEAD_DIM = 64
SSM_HEADS = D_INNER // SSM_HEAD_DIM
SSM_GROUPS = 4
SSM_STATE = 128
CONV_WIDTH = 4
CONV_DIM = D_INNER + 2 * SSM_GROUPS * SSM_STATE
SSM_IN_DIM = 2 * D_INNER + 2 * SSM_GROUPS * SSM_STATE + SSM_HEADS
SSD_CHUNK = 256

D_FF = 2816
RMS_EPS = 1e-6

kernel_name = 'moba_mamba2_macaron_step'

F32 = jnp.float32


def rms_norm(x, g):
    xf = x.astype(F32)
    y = xf * lax.rsqrt(jnp.mean(xf * xf, axis=-1, keepdims=True) + RMS_EPS)
    return (y * g.astype(F32)).astype(x.dtype)


def swiglu(x, w_in, w_out):
    a, b = jnp.split(x @ w_in, 2, axis=-1)
    return (jax.nn.silu(a) * b) @ w_out


def rel_bucket(dist):
    n = jnp.maximum(dist, 0)
    max_exact = REL_BUCKETS // 2
    nf = jnp.maximum(n, 1).astype(F32)
    large = max_exact + (jnp.log(nf / max_exact) / math.log(REL_MAX_DIST / max_exact)
                         * (REL_BUCKETS - max_exact)).astype(jnp.int32)
    large = jnp.minimum(large, REL_BUCKETS - 1)
    return jnp.where(n < max_exact, n, large)


def rel_pos_bias(rel_bias, dist):
    hidx = jnp.arange(N_HEADS).reshape((1, N_HEADS) + (1,) * (dist.ndim - 2))
    return rel_bias.T[hidx, rel_bucket(dist)].astype(F32)


def moba_gate(q, kmean, q_pos, nb):
    scores = jnp.einsum('bthd,bnhd->bhtn', q.astype(F32), kmean)
    own = q_pos // MOBA_BLOCK
    fully_past = jnp.arange(nb)[None, :] < own[:, None]
    scores = jnp.where(fully_past, scores, -jnp.inf)
    if nb < MOBA_TOPK:
        scores = jnp.pad(scores, ((0, 0), (0, 0), (0, 0), (0, MOBA_TOPK - nb)),
                         constant_values=-jnp.inf)
    _, idx = lax.top_k(scores, MOBA_TOPK)
    valid = idx < own[:, None]
    own_b = jnp.broadcast_to(own[:, None], idx.shape[:-1] + (1,)).astype(jnp.int32)
    sel = jnp.concatenate([idx.astype(jnp.int32), own_b], axis=-1)
    sel_valid = jnp.concatenate([valid, jnp.ones(own_b.shape, bool)], axis=-1)
    return sel, sel_valid


def moba_prompt(h, w_qkv, w_o, rel_bias):
    b, s, _ = h.shape
    qkv = (h @ w_qkv).reshape(b, s, 3, N_HEADS, HEAD_DIM)
    q, k, v = qkv[:, :, 0], qkv[:, :, 1], qkv[:, :, 2]
    nb = -(-s // MOBA_BLOCK)
    pad = nb * MOBA_BLOCK - s

    def to_blocks(t):
        t = jnp.pad(t, ((0, 0), (0, pad), (0, 0), (0, 0)))
        return t.reshape(b, nb, MOBA_BLOCK, N_HEADS, HEAD_DIM).transpose(0, 3, 1, 2, 4)

    kb, vb = to_blocks(k), to_blocks(v)
    kmean = (jnp.sum(kb, axis=3, dtype=F32) / MOBA_BLOCK).transpose(0, 2, 1, 3)
    q_pos = jnp.arange(s, dtype=jnp.int32)
    sel, sel_valid = moba_gate(q, kmean, q_pos, nb)
    k1 = MOBA_TOPK + 1
    nq = s // Q_BLOCK
    qs = q.reshape(b, nq, Q_BLOCK, N_HEADS, HEAD_DIM).transpose(1, 0, 3, 2, 4)
    sels = sel.reshape(b, N_HEADS, nq, Q_BLOCK, k1).transpose(2, 0, 1, 3, 4)
    vals = sel_valid.reshape(b, N_HEADS, nq, Q_BLOCK, k1).transpose(2, 0, 1, 3, 4)
    poss = q_pos.reshape(nq, Q_BLOCK)
    bidx = jnp.arange(b).reshape(b, 1, 1, 1)
    hidx = jnp.arange(N_HEADS).reshape(1, N_HEADS, 1, 1)
    offs = jnp.arange(MOBA_BLOCK, dtype=jnp.int32)
    scale = HEAD_DIM ** -0.5

    def attend(args):
        qb, sb, okb, pb = args
        sbc = jnp.minimum(sb, nb - 1)
        kg = kb[bidx, hidx, sbc]
        vg = vb[bidx, hidx, sbc]
        logits = jnp.einsum('bhqd,bhqjkd->bhqjk', qb, kg, preferred_element_type=F32) * scale
        k_pos = sb[..., None] * MOBA_BLOCK + offs
        dist = pb[None, None, :, None, None] - k_pos
        mask = okb[..., None] & (dist >= 0)
        logits = jnp.where(mask, logits + rel_pos_bias(rel_bias, dist), -jnp.inf)
        shp = logits.shape
        p = jax.nn.softmax(logits.reshape(shp[:3] + (-1,)), axis=-1).reshape(shp)
        return jnp.einsum('bhqjk,bhqjkd->bhqd', p.astype(vg.dtype), vg)

    o = lax.map(attend, (qs, sels, vals, poss))
    o = o.transpose(1, 0, 3, 2, 4).reshape(b, s, D_MODEL)
    y = o @ w_o
    k_pages = k.reshape(b, s // PAGE_SIZE, PAGE_SIZE, N_HEADS, HEAD_DIM)
    v_pages = v.reshape(b, s // PAGE_SIZE, PAGE_SIZE, N_HEADS, HEAD_DIM)
    return y, k_pages, v_pages


def moba_sample(h, cache_k, cache_v, page_ksum, layer, page_table, w_qkv, w_o, rel_bias):
    b, t, _ = h.shape
    n_pages = page_table.shape[1]
    past = n_pages * PAGE_SIZE
    qkv = (h @ w_qkv).reshape(b, t, 3, N_HEADS, HEAD_DIM)
    q, k, v = qkv[:, :, 0], qkv[:, :, 1], qkv[:, :, 2]
    nb = -(-(past + t) // MOBA_BLOCK)
    ppb = MOBA_BLOCK // PAGE_SIZE
    seq_sums = page_ksum[page_table]
    seq_sums = jnp.pad(seq_sums, ((0, 0), (0, nb * ppb - n_pages), (0, 0), (0, 0)))
    block_sums = seq_sums.reshape(b, nb, ppb, N_HEADS, HEAD_DIM).sum(axis=2)
    new_pos = past + jnp.arange(t, dtype=jnp.int32)
    new_blk = new_pos // MOBA_BLOCK
    block_sums = block_sums + jnp.einsum('tn,bthd->bnhd', jax.nn.one_hot(new_blk, nb, dtype=F32), k.astype(F32))
    kmean = block_sums / MOBA_BLOCK
    sel, sel_valid = moba_gate(q, kmean, new_pos, nb)
    k1 = MOBA_TOPK + 1
    scale = HEAD_DIM ** -0.5
    page_log = sel[..., None] * ppb + jnp.arange(ppb, dtype=jnp.int32)
    page_ok = sel_valid[..., None] & (page_log < n_pages)
    phys = page_table[jnp.arange(b).reshape(b, 1, 1, 1, 1), jnp.minimum(page_log, n_pages - 1)]
    rows = jnp.arange(PAGE_SIZE, dtype=jnp.int32)
    hidx6 = jnp.arange(N_HEADS).reshape(1, N_HEADS, 1, 1, 1, 1)
    nk = k1 * MOBA_BLOCK
    kg = cache_k[layer, phys[..., None], rows, hidx6].reshape(b, N_HEADS, t, nk, HEAD_DIM)
    vg = cache_v[layer, phys[..., None], rows, hidx6].reshape(b, N_HEADS, t, nk, HEAD_DIM)
    k_pos1 = (page_log[..., None] * PAGE_SIZE + rows).reshape(b, N_HEADS, t, nk)
    ok1 = jnp.broadcast_to(page_ok[..., None], page_log.shape + (PAGE_SIZE,)).reshape(b, N_HEADS, t, nk)
    qh = q.transpose(0, 2, 1, 3)
    l1 = jnp.einsum('bhtd,bhtkd->bhtk', qh, kg, preferred_element_type=F32) * scale
    d1 = new_pos[None, None, :, None] - k_pos1
    l1 = jnp.where(ok1, l1 + rel_pos_bias(rel_bias, d1), -jnp.inf)
    l2 = jnp.einsum('bthd,buhd->bhtu', q, k, preferred_element_type=F32) * scale
    d2 = jnp.broadcast_to(new_pos[:, None] - new_pos[None, :], (b, N_HEADS, t, t))
    in_sel = jnp.any((sel[..., None] == new_blk) & sel_valid[..., None], axis=3)
    ok2 = in_sel & (d2 >= 0)
    l2 = jnp.where(ok2, l2 + rel_pos_bias(rel_bias, d2), -jnp.inf)
    p = jax.nn.softmax(jnp.concatenate([l1, l2], axis=-1), axis=-1)
    o = (jnp.einsum('bhtk,bhtkd->bhtd', p[..., :nk].astype(vg.dtype), vg)
         + jnp.einsum('bhtu,buhd->bhtd', p[..., nk:].astype(v.dtype), v))
    y = o.transpose(0, 2, 1, 3).reshape(b, t, D_MODEL) @ w_o
    return y, k, v


def causal_dwconv(u, w, bias):
    c = u.shape[-1]
    out = lax.conv_general_dilated(u, w[:, None, :], window_strides=(1,), padding='VALID',
                                   dimension_numbers=('NWC', 'WIO', 'NWC'), feature_group_count=c)
    return out + bias


def ssd(x, dt, A, Bm, Cm, init_state):
    b, l, h, p = x.shape
    g, n = Bm.shape[2], Bm.shape[3]
    r = h // g
    T = min(SSD_CHUNK, l)
    nc = -(-l // T)
    pad = nc * T - l

    def padl(a):
        return jnp.pad(a, ((0, 0), (0, pad)) + ((0, 0),) * (a.ndim - 2))

    xc = padl(x.astype(F32)).reshape(b, nc, T, g, r, p)
    dtc = padl(dt).reshape(b, nc, T, g, r)
    Bc = padl(Bm.astype(F32)).reshape(b, nc, T, g, n)
    Cc = padl(Cm.astype(F32)).reshape(b, nc, T, g, n)
    xdt = xc * dtc[..., None]
    a = jnp.moveaxis(dtc * A.reshape(g, r), 2, -1)
    a_cs = jnp.cumsum(a, axis=-1)
    causal = jnp.tril(jnp.ones((T, T), bool))
    Lmat = jnp.exp(jnp.where(causal, a_cs[..., :, None] - a_cs[..., None, :], -jnp.inf))
    cb = jnp.einsum('bclgn,bcsgn->bcgls', Cc, Bc)
    y_diag = jnp.einsum('bcgls,bcgrls,bcsgrp->bclgrp', cb, Lmat, xdt)
    decay_states = jnp.exp(a_cs[..., -1:] - a_cs)
    states = jnp.einsum('bcsgn,bcgrs,bcsgrp->bcgrpn', Bc, decay_states, xdt)
    chunk_decay = jnp.exp(a_cs[..., -1])

    def carry_step(prev, inp):
        st, dec = inp
        return prev * dec[..., None, None] + st, prev

    final, states_in = lax.scan(carry_step, init_state.astype(F32).reshape(b, g, r, p, n),
                                (jnp.moveaxis(states, 1, 0), jnp.moveaxis(chunk_decay, 1, 0)))
    states_in = jnp.moveaxis(states_in, 0, 1)
    y_off = jnp.einsum('bclgn,bcgrpn,bcgrl->bclgrp', Cc, states_in, jnp.exp(a_cs))
    y = (y_diag + y_off).reshape(b, nc * T, h, p)[:, :l]
    return y, final.reshape(b, h, p, n)


def mamba_mixer(h, conv_prefix, init_state, w_in, conv_w, conv_b, dt_bias, A_log, D_skip, norm_w, w_out):
    b, l, _ = h.shape
    proj = h @ w_in
    z = proj[..., :D_INNER]
    xbc = proj[..., D_INNER:D_INNER + CONV_DIM]
    dt_raw = proj[..., D_INNER + CONV_DIM:]
    u = jnp.concatenate([conv_prefix.astype(xbc.dtype), xbc], axis=1)
    new_conv = u[:, u.shape[1] - (CONV_WIDTH - 1):]
    xbc = jax.nn.silu(causal_dwconv(u, conv_w, conv_b))
    gn = SSM_GROUPS * SSM_STATE
    xs = xbc[..., :D_INNER].reshape(b, l, SSM_HEADS, SSM_HEAD_DIM)
    Bm = xbc[..., D_INNER:D_INNER + gn].reshape(b, l, SSM_GROUPS, SSM_STATE)
    Cm = xbc[..., D_INNER + gn:].reshape(b, l, SSM_GROUPS, SSM_STATE)
    dt = jax.nn.softplus(dt_raw.astype(F32) + dt_bias.astype(F32))
    A = -jnp.exp(A_log.astype(F32))
    y, final = ssd(xs, dt, A, Bm, Cm, init_state)
    y = y + D_skip.astype(F32)[:, None] * xs.astype(F32)
    gy = (y.reshape(b, l, D_INNER) * jax.nn.silu(z.astype(F32))).reshape(b, l, SSM_GROUPS, D_INNER // SSM_GROUPS)
    gy = gy * lax.rsqrt(jnp.mean(gy * gy, axis=-1, keepdims=True) + RMS_EPS)
    gy = gy.reshape(b, l, D_INNER) * norm_w.astype(F32)
    out = gy.astype(h.dtype) @ w_out
    return out, final.astype(init_state.dtype), new_conv


def setup_inputs(seed: int = 0) -> dict:
    key = jax.random.key(seed)
    ks = jax.random.split(key, 32)
    n_pages = PAST_LEN // PAGE_SIZE
    n_pool = (5 * DEC_BATCH * n_pages + 3) // 4

    def nrm(k, shape, scale=1.0):
        return jax.random.normal(k, shape, F32) * scale

    def gain(k, shape):
        return 1.0 + 0.02 * jax.random.normal(k, shape, F32)

    dt0 = jnp.exp(jax.random.uniform(ks[20], (N_SSM_LAYERS, SSM_HEADS), F32)
                  * (math.log(0.1) - math.log(0.001)) + math.log(0.001))
    return {
        'x_prompt': nrm(ks[0], (BATCH, SEQ, D_MODEL)),
        'x_sample': nrm(ks[1], (DEC_BATCH, DEC_SEQ, D_MODEL)),
        'cache_k': nrm(ks[2], (N_ATTN_LAYERS, n_pool, PAGE_SIZE, N_HEADS, HEAD_DIM)),
        'cache_v': nrm(ks[3], (N_ATTN_LAYERS, n_pool, PAGE_SIZE, N_HEADS, HEAD_DIM)),
        'page_table': jax.random.permutation(ks[4], n_pool)[:DEC_BATCH * n_pages].reshape(DEC_BATCH, n_pages).astype(jnp.int32),
        'state_ssm': nrm(ks[5], (N_SSM_LAYERS, DEC_BATCH, SSM_HEADS, SSM_HEAD_DIM, SSM_STATE), 0.1),
        'state_conv': nrm(ks[6], (N_SSM_LAYERS, DEC_BATCH, CONV_WIDTH - 1, CONV_DIM)),
        'ffn1_norm': gain(ks[7], (DEPTH, D_MODEL)),
        'ffn1_w_in': nrm(ks[8], (DEPTH, D_MODEL, 2 * D_FF), D_MODEL ** -0.5),
        'ffn1_w_out': nrm(ks[9], (DEPTH, D_FF, D_MODEL), D_FF ** -0.5),
        'mix_norm': gain(ks[10], (DEPTH, D_MODEL)),
        'ffn2_norm': gain(ks[11], (DEPTH, D_MODEL)),
        'ffn2_w_in': nrm(ks[12], (DEPTH, D_MODEL, 2 * D_FF), D_MODEL ** -0.5),
        'ffn2_w_out': nrm(ks[13], (DEPTH, D_FF, D_MODEL), D_FF ** -0.5),
        'attn_w_qkv': nrm(ks[14], (N_ATTN_LAYERS, D_MODEL, 3 * D_MODEL), D_MODEL ** -0.5),
        'attn_w_o': nrm(ks[15], (N_ATTN_LAYERS, D_MODEL, D_MODEL), D_MODEL ** -0.5),
        'rel_bias': nrm(ks[16], (REL_BUCKETS, N_HEADS), 0.2),
        'ssm_w_in': nrm(ks[17], (N_SSM_LAYERS, D_MODEL, SSM_IN_DIM), D_MODEL ** -0.5),
        'ssm_conv_w': nrm(ks[18], (N_SSM_LAYERS, CONV_WIDTH, CONV_DIM), CONV_WIDTH ** -0.5),
        'ssm_conv_b': nrm(ks[19], (N_SSM_LAYERS, CONV_DIM), 0.02),
        'ssm_dt_bias': dt0 + jnp.log(-jnp.expm1(-dt0)),
        'ssm_A_log': jnp.log(jax.random.uniform(ks[21], (N_SSM_LAYERS, SSM_HEADS), F32, 1.0, 16.0)),
        'ssm_D': gain(ks[22], (N_SSM_LAYERS, SSM_HEADS)),
        'ssm_norm': gain(ks[23], (N_SSM_LAYERS, D_INNER)),
        'ssm_w_out': nrm(ks[24], (N_SSM_LAYERS, D_INNER, D_MODEL), D_INNER ** -0.5),
        'final_norm': gain(ks[25], (D_MODEL,)),
    }


def reference(x_prompt, x_sample, cache_k, cache_v, page_table, state_ssm, state_conv,
              ffn1_norm, ffn1_w_in, ffn1_w_out, mix_norm, ffn2_norm, ffn2_w_in, ffn2_w_out,
              attn_w_qkv, attn_w_o, rel_bias,
              ssm_w_in, ssm_conv_w, ssm_conv_b, ssm_dt_bias, ssm_A_log, ssm_D, ssm_norm, ssm_w_out,
              final_norm):
    xp, xs = x_prompt, x_sample
    page_ksum = jnp.sum(cache_k, axis=2, dtype=F32)
    kp_l, vp_l, ks_l, vs_l = [], [], [], []
    sp_l, cp_l, ss_l, cs_l = [], [], [], []
    for i in range(DEPTH):
        xp = xp + 0.5 * swiglu(rms_norm(xp, ffn1_norm[i]), ffn1_w_in[i], ffn1_w_out[i])
        xs = xs + 0.5 * swiglu(rms_norm(xs, ffn1_norm[i]), ffn1_w_in[i], ffn1_w_out[i])
        hp = rms_norm(xp, mix_norm[i])
        hs = rms_norm(xs, mix_norm[i])
        j = i // N_MIXERS
        if i % N_MIXERS == 0:
            yp, kp, vp = moba_prompt(hp, attn_w_qkv[j], attn_w_o[j], rel_bias)
            ys, kn, vn = moba_sample(hs, cache_k, cache_v, page_ksum[j], j, page_table,
                                     attn_w_qkv[j], attn_w_o[j], rel_bias)
            kp_l.append(kp); vp_l.append(vp); ks_l.append(kn); vs_l.append(vn)
        else:
            bp = hp.shape[0]
            yp, sp, cp = mamba_mixer(hp, jnp.zeros((bp, CONV_WIDTH - 1, CONV_DIM), hp.dtype),
                                     jnp.zeros((bp, SSM_HEADS, SSM_HEAD_DIM, SSM_STATE), hp.dtype),
                                     ssm_w_in[j], ssm_conv_w[j], ssm_conv_b[j], ssm_dt_bias[j],
                                     ssm_A_log[j], ssm_D[j], ssm_norm[j], ssm_w_out[j])
            ys, sn, cn = mamba_mixer(hs, state_conv[j], state_ssm[j],
                                     ssm_w_in[j], ssm_conv_w[j], ssm_conv_b[j], ssm_dt_bias[j],
                                     ssm_A_log[j], ssm_D[j], ssm_norm[j], ssm_w_out[j])
            sp_l.append(sp); cp_l.append(cp); ss_l.append(sn); cs_l.append(cn)
        xp = xp + yp
        xs = xs + ys
        xp = xp + 0.5 * swiglu(rms_norm(xp, ffn2_norm[i]), ffn2_w_in[i], ffn2_w_out[i])
        xs = xs + 0.5 * swiglu(rms_norm(xs, ffn2_norm[i]), ffn2_w_in[i], ffn2_w_out[i])
    y_prompt = rms_norm(xp, final_norm)
    y_sample = rms_norm(xs, final_norm)
    return (y_prompt, y_sample, jnp.stack(kp_l), jnp.stack(vp_l), jnp.stack(ks_l), jnp.stack(vs_l),
            jnp.stack(sp_l), jnp.stack(cp_l), jnp.stack(ss_l), jnp.stack(cs_l))
```

```python
import functools
import math

import jax
import jax.numpy as jnp
from jax import lax
from jax.experimental import pallas as pl
from jax.experimental.pallas import tpu as pltpu

F32 = jnp.float32
BF16 = jnp.bfloat16
I32 = jnp.int32
NEG_INF = float("-inf")

D_MODEL = 1024
N_HEADS = 16
HEAD_DIM = 64
MOBA_BLOCK = 256
MOBA_TOPK = 3
PAGE_SIZE = 128
REL_BUCKETS = 32
REL_MAX_DIST = 128
D_INNER = 2048
SSM_HEADS = 32
SSM_HEAD_DIM = 64
SSM_GROUPS = 4
SSM_STATE = 128
CONV_WIDTH = 4
CONV_DIM = D_INNER + 2 * SSM_GROUPS * SSM_STATE
SSM_IN_DIM = 2 * D_INNER + 2 * SSM_GROUPS * SSM_STATE + SSM_HEADS
SSD_CHUNK = 256
D_FF = 2816
RMS_EPS = 1e-6

LANES = 128
SUBLANES = 8
VMEM_LIMIT_BYTES = 56 * 1024 * 1024

FF_CHUNK = 256
N_FF_CHUNKS = D_FF // FF_CHUNK
HEADS_PER_SLAB = LANES // HEAD_DIM
N_SLABS = N_HEADS // HEADS_PER_SLAB
GROUP_DIM = D_INNER // SSM_GROUPS
HEADS_PER_GROUP = SSM_HEADS // SSM_GROUPS
SSM_PROJ_PAD = 5248
DT_OFF = D_INNER + CONV_DIM
SAMPLE_T = 8
PAGES_PER_BLOCK = MOBA_BLOCK // PAGE_SIZE


def _params(n_axes, vmem=VMEM_LIMIT_BYTES):
    return pltpu.CompilerParams(dimension_semantics=("arbitrary",) * n_axes,
                                vmem_limit_bytes=vmem)


def _whole(shape):
    nd = len(shape)
    return pl.BlockSpec(shape, lambda *_: (0,) * nd)


def _smem():
    return pl.BlockSpec(memory_space=pltpu.SMEM)


def _rms(x, g):
    ms = jnp.mean(x * x, axis=-1, keepdims=True)
    return x * lax.rsqrt(ms + RMS_EPS) * g


def _silu(x):
    return x * jax.nn.sigmoid(x)


def _softplus(x):
    return jnp.maximum(x, 0.0) + jnp.log1p(jnp.exp(-jnp.abs(x)))


def _split3(x):
    hi = x.astype(BF16)
    r1 = x - hi.astype(F32)
    mid = r1.astype(BF16)
    lo = (r1 - mid.astype(F32)).astype(BF16)
    return hi, mid, lo


def _select_matmul(x, sel):
    hi, mid, lo = _split3(x)
    dot = functools.partial(jnp.dot, preferred_element_type=F32)
    return dot(hi, sel) + dot(mid, sel) + dot(lo, sel)


def _top3_rows(s, n_iota, sentinel):
    picks = []
    for _ in range(MOBA_TOPK):
        m = jnp.max(s, axis=0, keepdims=True)
        idx = jnp.min(jnp.where(s == m, n_iota, sentinel), axis=0, keepdims=True)
        picks.append((idx, m))
        s = jnp.where(n_iota == idx, NEG_INF, s)
    return picks


def _bias_from_buckets(bk, rb_ref, h):
    out = jnp.full(bk.shape, NEG_INF, F32)
    for b in range(REL_BUCKETS):
        out = jnp.where(bk == b, rb_ref[b, h], out)
    return out


def _rel_bucket(dist):
    n = jnp.maximum(dist, 0)
    max_exact = REL_BUCKETS // 2
    nf = jnp.maximum(n, 1).astype(F32)
    large = max_exact + (jnp.log(nf / max_exact) / math.log(REL_MAX_DIST / max_exact)
                         * (REL_BUCKETS - max_exact)).astype(I32)
    large = jnp.minimum(large, REL_BUCKETS - 1)
    return jnp.where(n < max_exact, n, large).astype(I32)


def _ffn_kernel(*refs, final):
    if final:
        x_ref, g_ref, wa_ref, wb_ref, wo_ref, fg_ref, o_ref, xn_ref, acc_ref = refs
    else:
        x_ref, g_ref, wa_ref, wb_ref, wo_ref, o_ref, xn_ref, acc_ref = refs
    xn_ref[...] = _rms(x_ref[...], g_ref[...]).astype(BF16)
    acc_ref[...] = jnp.zeros_like(acc_ref)

    def chunk(c, carry):
        xn = xn_ref[...]
        a = jnp.dot(xn, wa_ref[c], preferred_element_type=F32)
        b = jnp.dot(xn, wb_ref[c], preferred_element_type=F32)
        h = (_silu(a) * b).astype(BF16)
        acc_ref[...] += jnp.dot(h, wo_ref[c], preferred_element_type=F32)
        return carry

    lax.fori_loop(0, N_FF_CHUNKS, chunk, 0)
    y = x_ref[...] + 0.5 * acc_ref[...]
    if final:
        y = _rms(y, fg_ref[...])
    o_ref[...] = y


def _ffn(x, g, w, tm, final_g=None):
    rows = x.shape[0]
    wa, wb, wo = w
    row_spec = pl.BlockSpec((tm, D_MODEL), lambda i: (i, 0))
    in_specs = [row_spec, _whole((1, D_MODEL)), _whole(wa.shape), _whole(wb.shape), _whole(wo.shape)]
    args = [x, g.reshape(1, D_MODEL), wa, wb, wo]
    if final_g is not None:
        in_specs.append(_whole((1, D_MODEL)))
        args.append(final_g.reshape(1, D_MODEL))
    return pl.pallas_call(
        functools.partial(_ffn_kernel, final=final_g is not None),
        grid=(rows // tm,),
        in_specs=in_specs,
        out_specs=row_spec,
        out_shape=jax.ShapeDtypeStruct((rows, D_MODEL), F32),
        scratch_shapes=[pltpu.VMEM((tm, D_MODEL), BF16), pltpu.VMEM((tm, D_MODEL), F32)],
        compiler_params=_params(1),
        name="ffn",
    )(*args)


def _proj_kernel(x_ref, g_ref, w_ref, o_ref, xn_ref, *, widths):
    xn_ref[...] = _rms(x_ref[...], g_ref[...]).astype(BF16)
    lo = 0
    for wd in widths:
        o_ref[:, lo:lo + wd] = jnp.dot(xn_ref[...], w_ref[:, lo:lo + wd], preferred_element_type=F32)
        lo += wd


def _proj(x, g, w, tm):
    rows, n = x.shape[0], w.shape[1]
    col_chunk = 4 * LANES
    widths = [col_chunk] * (n // col_chunk) + ([n % col_chunk] if n % col_chunk else [])
    return pl.pallas_call(
        functools.partial(_proj_kernel, widths=tuple(widths)),
        grid=(rows // tm,),
        in_specs=[pl.BlockSpec((tm, D_MODEL), lambda i: (i, 0)), _whole((1, D_MODEL)), _whole(w.shape)],
        out_specs=pl.BlockSpec((tm, n), lambda i: (i, 0)),
        out_shape=jax.ShapeDtypeStruct((rows, n), F32),
        scratch_shapes=[pltpu.VMEM((tm, D_MODEL), BF16)],
        compiler_params=_params(1),
        name="norm_proj",
    )(x, g.reshape(1, D_MODEL), w)


def _matres_kernel(a_ref, w_ref, r_ref, o_ref):
    o_ref[...] = r_ref[...] + jnp.dot(a_ref[...].astype(BF16), w_ref[...], preferred_element_type=F32)


def _matres(a, w, res, tm):
    rows, k = a.shape
    n = w.shape[1]
    return pl.pallas_call(
        _matres_kernel,
        grid=(rows // tm,),
        in_specs=[pl.BlockSpec((tm, k), lambda i: (i, 0)), _whole(w.shape),
                  pl.BlockSpec((tm, n), lambda i: (i, 0))],
        out_specs=pl.BlockSpec((tm, n), lambda i: (i, 0)),
        out_shape=jax.ShapeDtypeStruct((rows, n), F32),
        compiler_params=_params(1),
        name="matmul_residual",
    )(a, w, res)


def _qkv_prompt_kernel(x_ref, g_ref, wqt_ref, wk_ref, wv_ref, wvt_ref,
                       k_ref, v_ref, kb_ref, qt_ref, vt_ref, ks_ref, xn_ref, *, tm):
    xn_ref[...] = _rms(x_ref[...], g_ref[...]).astype(BF16)
    xn = xn_ref[...]
    k = jnp.dot(xn, wk_ref[...], preferred_element_type=F32)
    k_ref[...] = k
    kb_ref[...] = k.astype(BF16)
    for j in range(tm // MOBA_BLOCK):
        ks_ref[0, j:j + 1, :] = jnp.sum(k[j * MOBA_BLOCK:(j + 1) * MOBA_BLOCK], axis=0, keepdims=True)
    v_ref[...] = jnp.dot(xn, wv_ref[...], preferred_element_type=F32)
    nt = (((1,), (1,)), ((), ()))
    qt = lax.dot_general(wqt_ref[...], xn, nt, preferred_element_type=F32)
    qt_ref[...] = (qt * (HEAD_DIM ** -0.5)).astype(BF16)
    vt = lax.dot_general(wvt_ref[...], xn, nt, preferred_element_type=F32)
    for j in range(tm // MOBA_BLOCK):
        vt_ref[j] = vt[:, j * MOBA_BLOCK:(j + 1) * MOBA_BLOCK].astype(BF16)


def _qkv_prompt(x, g, wqt, wk, wv, wvt, tm):
    s = x.shape[0]
    nblk = s // MOBA_BLOCK
    bpt = tm // MOBA_BLOCK
    row_spec = pl.BlockSpec((tm, D_MODEL), lambda i: (i, 0))
    sq = _whole((D_MODEL, D_MODEL))
    return pl.pallas_call(
        functools.partial(_qkv_prompt_kernel, tm=tm),
        grid=(s // tm,),
        in_specs=[row_spec, _whole((1, D_MODEL)), sq, sq, sq, sq],
        out_specs=[row_spec, row_spec, row_spec,
                   pl.BlockSpec((D_MODEL, tm), lambda i: (0, i)),
                   pl.BlockSpec((bpt, D_MODEL, MOBA_BLOCK), lambda i: (i, 0, 0)),
                   pl.BlockSpec((1, bpt, D_MODEL), lambda i: (i, 0, 0))],
        out_shape=[jax.ShapeDtypeStruct((s, D_MODEL), F32),
                   jax.ShapeDtypeStruct((s, D_MODEL), F32),
                   jax.ShapeDtypeStruct((s, D_MODEL), BF16),
                   jax.ShapeDtypeStruct((D_MODEL, s), BF16),
                   jax.ShapeDtypeStruct((nblk, D_MODEL, MOBA_BLOCK), BF16),
                   jax.ShapeDtypeStruct((s // tm, bpt, D_MODEL), F32)],
        scratch_shapes=[pltpu.VMEM((tm, D_MODEL), BF16)],
        compiler_params=_params(1),
        name="qkv_prompt",
    )(x, g.reshape(1, D_MODEL), wqt, wk, wv, wvt)


def _masked_softmax_step(s, keep_row, shift, m, l, acc, v_t):
    sm = jnp.where(keep_row > 0.0, s, NEG_INF)
    m_new = jnp.maximum(m, jnp.max(sm, axis=0, keepdims=True) + shift)
    alpha = jnp.exp(m - m_new)
    p = jnp.exp(sm - (m_new - shift))
    l = alpha * l + jnp.sum(p, axis=0, keepdims=True)
    acc = alpha * acc + jnp.dot(v_t, p.astype(BF16), preferred_element_type=F32)
    return m_new, l, acc


def _moba_prompt_kernel(rb_ref, qt_ref, kb_ref, vt_ref, ks_ref, bkt_ref, o_ref,
                        sel_ref, bias_ref, m_ref, l_ref, acc_ref, *, nblk):
    slab = pl.program_id(0)
    qi = pl.program_id(1)
    blk = MOBA_BLOCK

    @pl.when(qi == 0)
    def _build_bias():
        for hh in range(HEADS_PER_SLAB):
            for kind in range(2):
                bias_ref[hh, kind] = _bias_from_buckets(bkt_ref[kind], rb_ref, slab * HEADS_PER_SLAB + hh)

    qt = qt_ref[...].astype(F32)
    row_head = lax.broadcasted_iota(I32, (LANES, blk), 0) // HEAD_DIM
    n_iota = lax.broadcasted_iota(I32, (nblk, blk), 0)
    kmean = (ks_ref[...] * (1.0 / blk)).astype(BF16)
    outs = []
    for hh in range(HEADS_PER_SLAB):
        h = slab * HEADS_PER_SLAB + hh
        far_bias = rb_ref[REL_BUCKETS - 1, h]
        qz = jnp.where(row_head == hh, qt, 0.0).astype(BF16)
        scores = jnp.dot(kmean, qz, preferred_element_type=F32)
        scores = jnp.where(n_iota < qi, scores, NEG_INF)
        sel = jnp.zeros((nblk, blk), F32)
        for idx, m in _top3_rows(scores, n_iota, nblk):
            sel = jnp.where((n_iota == idx) & (m > NEG_INF), 1.0, sel)
        sel_ref[hh] = sel

        k_own = kb_ref[pl.ds(pl.multiple_of(qi * blk, blk), blk), :]
        s = jnp.dot(k_own, qz, preferred_element_type=F32) + bias_ref[hh, 0]
        m0 = jnp.max(s, axis=0, keepdims=True)
        p = jnp.exp(s - m0)
        m_ref[...] = m0
        l_ref[...] = jnp.sum(p, axis=0, keepdims=True)
        acc_ref[...] = jnp.dot(vt_ref[qi, hh * HEAD_DIM:(hh + 1) * HEAD_DIM, :], p.astype(BF16),
                               preferred_element_type=F32)

        @pl.when(qi >= 1)
        def _prev_block():
            n = qi - 1
            k_n = kb_ref[pl.ds(pl.multiple_of(n * blk, blk), blk), :]
            s1 = jnp.dot(k_n, qz, preferred_element_type=F32) + bias_ref[hh, 1]
            m1, l1, a1 = _masked_softmax_step(
                s1, sel_ref[hh, pl.ds(n, 1), :], 0.0, m_ref[...], l_ref[...], acc_ref[...],
                vt_ref[n, hh * HEAD_DIM:(hh + 1) * HEAD_DIM, :])
            m_ref[...] = m1
            l_ref[...] = l1
            acc_ref[...] = a1

        def far_block(n, carry):
            k_n = kb_ref[pl.ds(pl.multiple_of(n * blk, blk), blk), :]
            s2 = jnp.dot(k_n, qz, preferred_element_type=F32)
            return _masked_softmax_step(
                s2, sel_ref[hh, pl.ds(n, 1), :], far_bias, *carry,
                vt_ref[n, hh * HEAD_DIM:(hh + 1) * HEAD_DIM, :])

        m, l, acc = lax.fori_loop(0, jnp.maximum(qi - 1, 0), far_block,
                                  (m_ref[...], l_ref[...], acc_ref[...]))
        outs.append(acc / l)
    o_ref[...] = jnp.concatenate(outs, axis=0).T.astype(BF16)


def _prompt_bucket_maps():
    kk = jnp.arange(MOBA_BLOCK, dtype=I32)[:, None]
    tt = jnp.arange(MOBA_BLOCK, dtype=I32)[None, :]
    own = jnp.where(tt - kk >= 0, _rel_bucket(tt - kk), -1)
    prev = _rel_bucket(MOBA_BLOCK + tt - kk)
    return jnp.stack([own, prev]).astype(I32)


def _moba_prompt(qt, kb, vt, ksum, rel_bias):
    s = kb.shape[0]
    nblk = s // MOBA_BLOCK
    return pl.pallas_call(
        functools.partial(_moba_prompt_kernel, nblk=nblk),
        grid=(N_SLABS, nblk),
        in_specs=[_smem(),
                  pl.BlockSpec((LANES, MOBA_BLOCK), lambda sl, qi: (sl, qi)),
                  pl.BlockSpec((s, LANES), lambda sl, qi: (0, sl)),
                  pl.BlockSpec((nblk, LANES, MOBA_BLOCK), lambda sl, qi: (0, sl, 0)),
                  pl.BlockSpec((nblk, LANES), lambda sl, qi: (0, sl)),
                  _whole((2, MOBA_BLOCK, MOBA_BLOCK))],
        out_specs=pl.BlockSpec((MOBA_BLOCK, LANES), lambda sl, qi: (qi, sl)),
        out_shape=jax.ShapeDtypeStruct((s, D_MODEL), BF16),
        scratch_shapes=[pltpu.VMEM((HEADS_PER_SLAB, nblk, MOBA_BLOCK), F32),
                        pltpu.VMEM((HEADS_PER_SLAB, 2, MOBA_BLOCK, MOBA_BLOCK), F32),
                        pltpu.VMEM((1, MOBA_BLOCK), F32),
                        pltpu.VMEM((1, MOBA_BLOCK), F32),
                        pltpu.VMEM((HEAD_DIM, MOBA_BLOCK), F32)],
        compiler_params=_params(2),
        name="moba_prompt",
    )(rel_bias, qt, kb, vt, ksum, _prompt_bucket_maps())


def _page_sum_kernel(pt_ref, *refs, pps):
    del pt_ref
    pages, o_ref = refs[:pps], refs[pps]
    for i in range(pps // PAGES_PER_BLOCK):
        tot = jnp.sum(pages[PAGES_PER_BLOCK * i][0], axis=0, keepdims=True)
        for j in range(1, PAGES_PER_BLOCK):
            tot = tot + jnp.sum(pages[PAGES_PER_BLOCK * i + j][0], axis=0, keepdims=True)
        o_ref[0, i:i + 1, :] = tot


def _page_sums(cache, page_table):
    b, npg = page_table.shape
    pps = min(16, npg)
    bps = pps // PAGES_PER_BLOCK

    def page_spec(i):
        return pl.BlockSpec((1, PAGE_SIZE, D_MODEL),
                            lambda bi, j, pt: (pt[bi * npg + j * pps + i], 0, 0))

    return pl.pallas_call(
        functools.partial(_page_sum_kernel, pps=pps),
        grid_spec=pltpu.PrefetchScalarGridSpec(
            num_scalar_prefetch=1,
            grid=(b, npg // pps),
            in_specs=[page_spec(i) for i in range(pps)],
            out_specs=pl.BlockSpec((1, bps, D_MODEL), lambda bi, j, pt: (bi * (npg // pps) + j, 0, 0)),
        ),
        out_shape=jax.ShapeDtypeStruct((b * (npg // pps), bps, D_MODEL), F32),
        compiler_params=_params(2),
        name="page_sums",
    )(page_table.reshape(-1), *([cache] * pps)).reshape(b, npg // PAGES_PER_BLOCK, D_MODEL)


def _sample_gate_kernel(q_ref, bs_ref, seg_ref, sel_ref, *, t_new, nblk):
    kmean = bs_ref[0] * (1.0 / MOBA_BLOCK)
    n_iota = lax.broadcasted_iota(I32, (nblk, LANES), 0)
    sel_ref[...] = jnp.zeros_like(sel_ref)
    for t in range(t_new):
        prod = kmean * q_ref[0, t:t + 1, :]
        scores = _select_matmul(prod, seg_ref[...])
        for r, (idx, _) in enumerate(_top3_rows(scores, n_iota, nblk)):
            sel_ref[0, t, r:r + 1, :] = idx


def _sample_gate(q, bsum):
    b, t_new, _ = q.shape
    nblk = bsum.shape[1]
    seg = (jnp.arange(D_MODEL, dtype=I32)[:, None] // HEAD_DIM
           == jnp.arange(LANES, dtype=I32)[None, :]).astype(BF16)
    sel = pl.pallas_call(
        functools.partial(_sample_gate_kernel, t_new=t_new, nblk=nblk),
        grid=(b,),
        in_specs=[pl.BlockSpec((1, t_new, D_MODEL), lambda i: (i, 0, 0)),
                  pl.BlockSpec((1, nblk, D_MODEL), lambda i: (i, 0, 0)),
                  _whole((D_MODEL, LANES))],
        out_specs=pl.BlockSpec((1, t_new, SUBLANES, LANES), lambda i: (i, 0, 0, 0)),
        out_shape=jax.ShapeDtypeStruct((b, t_new, SUBLANES, LANES), I32),
        compiler_params=_params(1),
        name="sample_gate",
    )(q, bsum, seg)
    return sel[:, :, :MOBA_TOPK, :N_HEADS]


def _sample_attn_kernel(sel_ref, pt_ref, rb_ref, q_ref, kn_ref, vn_ref, pb_ref, *refs,
                        t_new, last_blk):
    del pt_ref
    n_slabs = HEADS_PER_SLAB * MOBA_TOPK * PAGES_PER_BLOCK
    k_slabs, v_slabs, o_ref = refs[:n_slabs], refs[n_slabs:2 * n_slabs], refs[2 * n_slabs]
    bi, t, slab = pl.program_id(0), pl.program_id(1), pl.program_id(2)
    lane_head = lax.broadcasted_iota(I32, (1, LANES), 1) // HEAD_DIM
    row = lax.broadcasted_iota(I32, (SUBLANES, 1), 0)
    q = q_ref[0, 0] * (HEAD_DIM ** -0.5)
    kn = kn_ref[0]
    vn = vn_ref[0]
    outs = []
    for hh in range(HEADS_PER_SLAB):
        h = slab * HEADS_PER_SLAB + hh
        far_bias = rb_ref[REL_BUCKETS - 1, h]
        qz = jnp.where(lane_head == hh, q, 0.0)
        q8 = jnp.broadcast_to(qz, (SUBLANES, LANES)).astype(BF16)
        logits = []
        for j in range(MOBA_TOPK):
            blk_j = sel_ref[((bi * t_new + t) * MOBA_TOPK + j) * N_HEADS + h]
            for pg in range(PAGES_PER_BLOCK):
                ks = k_slabs[(hh * MOBA_TOPK + j) * PAGES_PER_BLOCK + pg][0].astype(BF16)
                lg = lax.dot_general(q8, ks, (((1,), (1,)), ((), ())),
                                     preferred_element_type=F32)[0:1]
                near = _bias_from_buckets(pb_ref[pl.ds(t * PAGES_PER_BLOCK + pg, 1), :], rb_ref, h)
                logits.append(lg + jnp.where(blk_j == last_blk, near, far_bias))
        l2 = jnp.sum(kn * qz, axis=-1, keepdims=True)
        bias2 = jnp.zeros((SUBLANES, 1), F32)
        for u in range(t_new):
            bias2 = jnp.where(row == u, rb_ref[jnp.maximum(t - u, 0), h], bias2)
        l2 = jnp.where(row <= t, l2 + bias2, NEG_INF)
        m = jnp.max(l2, axis=0, keepdims=True)
        for lg in logits:
            m = jnp.maximum(m, jnp.max(lg, axis=-1, keepdims=True))
        p2 = jnp.exp(l2 - m)
        denom = jnp.sum(p2, axis=0, keepdims=True)
        o = jnp.sum(p2 * vn, axis=0, keepdims=True)
        for i, lg in enumerate(logits):
            p1 = jnp.exp(lg - m)
            denom = denom + jnp.sum(p1, axis=-1, keepdims=True)
            vs = v_slabs[(hh * MOBA_TOPK + i // PAGES_PER_BLOCK) * PAGES_PER_BLOCK
                         + i % PAGES_PER_BLOCK][0].astype(BF16)
            p8 = jnp.broadcast_to(p1, (SUBLANES, LANES)).astype(BF16)
            o = o + jnp.dot(p8, vs, preferred_element_type=F32)[0:1]
        outs.append(o / denom)
    res = outs[0]
    for hh in range(1, HEADS_PER_SLAB):
        res = jnp.where(lane_head == hh, outs[hh], res)
    o_ref[0, 0] = res


def _sample_attn(q, k_new, v_new, cache_k, cache_v, sel, page_table, rel_bias):
    b, t_new, _ = q.shape
    npg = page_table.shape[1]
    last_blk = npg // PAGES_PER_BLOCK - 1
    pad = ((0, 0), (0, SUBLANES - t_new), (0, 0))
    tt = jnp.arange(t_new, dtype=I32)[:, None, None]
    pg = jnp.arange(PAGES_PER_BLOCK, dtype=I32)[None, :, None]
    rr = jnp.arange(PAGE_SIZE, dtype=I32)[None, None, :]
    near_bkt = _rel_bucket(MOBA_BLOCK + tt - pg * PAGE_SIZE - rr).reshape(t_new * PAGES_PER_BLOCK, PAGE_SIZE)

    def slab_spec(hh, j, pgi):
        def index(bi, t, sl, sel_r, pt_r):
            blk = sel_r[((bi * t_new + t) * MOBA_TOPK + j) * N_HEADS + sl * HEADS_PER_SLAB + hh]
            return (pt_r[bi * npg + blk * PAGES_PER_BLOCK + pgi], 0, sl)
        return pl.BlockSpec((1, PAGE_SIZE, LANES), index)

    slab_specs = [slab_spec(hh, j, pgi) for hh in range(HEADS_PER_SLAB)
                  for j in range(MOBA_TOPK) for pgi in range(PAGES_PER_BLOCK)]
    new_spec = pl.BlockSpec((1, SUBLANES, LANES), lambda bi, t, sl, *_: (bi, 0, sl))
    tok_spec = pl.BlockSpec((1, 1, 1, LANES), lambda bi, t, sl, *_: (bi, t, 0, sl))
    n_slabs = len(slab_specs)
    out = pl.pallas_call(
        functools.partial(_sample_attn_kernel, t_new=t_new, last_blk=last_blk),
        grid_spec=pltpu.PrefetchScalarGridSpec(
            num_scalar_prefetch=2,
            grid=(b, t_new, N_SLABS),
            in_specs=[_smem(), tok_spec, new_spec, new_spec,
                      pl.BlockSpec((t_new * PAGES_PER_BLOCK, PAGE_SIZE), lambda *_: (0, 0))]
                     + slab_specs + slab_specs,
            out_specs=tok_spec,
        ),
        out_shape=jax.ShapeDtypeStruct((b, t_new, 1, D_MODEL), F32),
        compiler_params=_params(3),
        name="sample_attn",
    )(sel.reshape(-1), page_table.reshape(-1), rel_bias,
      q.reshape(b, t_new, 1, D_MODEL), jnp.pad(k_new, pad), jnp.pad(v_new, pad), near_bkt,
      *([cache_k] * n_slabs), *([cache_v] * n_slabs))
    return out.reshape(b, t_new, D_MODEL)


def _ssd_chunk(t, ext_ref, p_ref, st_ref, gy_ref, cw_ref, cb_ref, dtb_ref, alog_ref, dsk_ref,
               nw_ref, e_ref, wout_ref, n_valid):
    def conv_act(lo, width):
        acc = cb_ref[:, lo:lo + width]
        for j in range(CONV_WIDTH):
            acc = acc + cw_ref[j:j + 1, lo:lo + width] * ext_ref[5 + j:5 + j + t, lo:lo + width]
        return _silu(acc)

    dt = _softplus(p_ref[:, DT_OFF:DT_OFF + LANES] + dtb_ref[...])
    if n_valid < t:
        dt = jnp.where(lax.broadcasted_iota(I32, (t, LANES), 0) < n_valid, dt, 0.0)
    a = dt * -jnp.exp(alog_ref[...])
    r_i = lax.broadcasted_iota(I32, (t, t), 0)
    c_i = lax.broadcasted_iota(I32, (t, t), 1)
    causal = r_i >= c_i
    acs = jnp.dot(causal.astype(F32), a, preferred_element_type=F32,
                  precision=lax.Precision.HIGHEST)
    if t % LANES == 0:
        acs_t = acs.T
    else:
        acs_t = lax.dot_general(jnp.eye(LANES, dtype=F32), acs, (((1,), (1,)), ((), ())),
                                preferred_element_type=F32, precision=lax.Precision.HIGHEST)
    lane_hi = lax.broadcasted_iota(I32, (t, LANES), 1) >= SSM_HEAD_DIM

    for g in range(SSM_GROUPS):
        ch = slice(g * GROUP_DIM, (g + 1) * GROUP_DIM)
        e_g = e_ref[:, ch]
        xs = conv_act(g * GROUP_DIM, GROUP_DIM)
        b_g = conv_act(D_INNER + g * SSM_STATE, SSM_STATE).astype(BF16)
        c_g = conv_act(D_INNER + (SSM_GROUPS + g) * SSM_STATE, SSM_STATE).astype(BF16)
        dt_g = _select_matmul(dt, e_g)
        acs_g = _select_matmul(acs, e_g)
        last_g = acs_g[t - 1:t, :]
        xdt = xs * dt_g
        xdt_b = xdt.astype(BF16)
        xdt_dec = (xdt * jnp.exp(last_g - acs_g)).astype(BF16)
        cb = lax.dot_general(c_g, b_g, (((1,), (1,)), ((), ())), preferred_element_type=F32)
        st_in = st_ref[:, ch]
        y_off = jnp.dot(c_g, st_in.astype(BF16), preferred_element_type=F32) * jnp.exp(acs_g)
        st_new = lax.dot_general(b_g, xdt_dec, (((0,), (0,)), ((), ())), preferred_element_type=F32)
        st_ref[:, ch] = st_in * jnp.exp(last_g) + st_new
        for j in range(HEADS_PER_GROUP // HEADS_PER_SLAB):
            cols = slice(j * LANES, (j + 1) * LANES)
            pair = xdt_b[:, cols]
            ys = []
            for hh in range(HEADS_PER_SLAB):
                h = g * HEADS_PER_GROUP + j * HEADS_PER_SLAB + hh
                decay = jnp.exp(jnp.where(causal, acs[:, h:h + 1] - acs_t[h:h + 1, :], NEG_INF))
                ys.append(jnp.dot((cb * decay).astype(BF16), pair, preferred_element_type=F32))
            y_diag = jnp.where(lane_hi, ys[1], ys[0])
            lo = g * GROUP_DIM + j * LANES
            y = y_diag + y_off[:, cols] + dsk_ref[:, lo:lo + LANES] * xs[:, cols]
            gy_ref[:, lo:lo + LANES] = y * _silu(p_ref[:, lo:lo + LANES])
        gy = gy_ref[:, ch]
        ms = jnp.mean(gy * gy, axis=-1, keepdims=True)
        gy_ref[:, ch] = gy * lax.rsqrt(ms + RMS_EPS) * nw_ref[:, ch]
    return jnp.dot(gy_ref[...].astype(BF16), wout_ref[...], preferred_element_type=F32)


def _ssm_prompt_kernel(p_ref, xres_ref, cw_ref, cb_ref, dtb_ref, alog_ref, dsk_ref, nw_ref, e_ref,
                       wout_ref, y_ref, st_out_ref, cv_out_ref, ext_ref, st_ref, gy_ref):
    c = pl.program_id(0)
    t = SSD_CHUNK

    @pl.when(c == 0)
    def _init():
        ext_ref[t:t + SUBLANES, :] = jnp.zeros((SUBLANES, CONV_DIM), F32)
        st_ref[...] = jnp.zeros_like(st_ref)

    ext_ref[0:SUBLANES, :] = ext_ref[t:t + SUBLANES, :]
    ext_ref[SUBLANES:SUBLANES + t, :] = p_ref[:, D_INNER:D_INNER + CONV_DIM]
    out = _ssd_chunk(t, ext_ref, p_ref, st_ref, gy_ref, cw_ref, cb_ref, dtb_ref, alog_ref, dsk_ref,
                     nw_ref, e_ref, wout_ref, t)
    y_ref[...] = xres_ref[...] + out

    @pl.when(c == pl.num_programs(0) - 1)
    def _finish():
        st_out_ref[...] = st_ref[...].T
        cv_out_ref[...] = ext_ref[t:t + SUBLANES, :]


def _ssm_consts(w):
    conv_w, conv_b, dt_bias, a_log, d_skip, norm_w, w_out = w
    lane_pad = (0, LANES - SSM_HEADS)
    expand = (jnp.arange(LANES, dtype=I32)[:, None]
              == jnp.arange(D_INNER, dtype=I32)[None, :] // SSM_HEAD_DIM).astype(BF16)
    return (conv_w, conv_b.reshape(1, CONV_DIM),
            jnp.pad(dt_bias, lane_pad).reshape(1, LANES), jnp.pad(a_log, lane_pad).reshape(1, LANES),
            jnp.repeat(d_skip, SSM_HEAD_DIM).reshape(1, D_INNER), norm_w.reshape(1, D_INNER),
            expand, w_out)


def _const_specs(consts):
    return [_whole(c.shape) for c in consts]


def _ssm_prompt(proj, xres, w):
    s = proj.shape[0]
    t = SSD_CHUNK
    consts = _ssm_consts(w)
    y, st, cv = pl.pallas_call(
        _ssm_prompt_kernel,
        grid=(s // t,),
        in_specs=[pl.BlockSpec((t, SSM_PROJ_PAD), lambda c: (c, 0)),
                  pl.BlockSpec((t, D_MODEL), lambda c: (c, 0))] + _const_specs(consts),
        out_specs=[pl.BlockSpec((t, D_MODEL), lambda c: (c, 0)),
                   _whole((D_INNER, SSM_STATE)), _whole((SUBLANES, CONV_DIM))],
        out_shape=[jax.ShapeDtypeStruct((s, D_MODEL), F32),
                   jax.ShapeDtypeStruct((D_INNER, SSM_STATE), F32),
                   jax.ShapeDtypeStruct((SUBLANES, CONV_DIM), F32)],
        scratch_shapes=[pltpu.VMEM((t + 2 * SUBLANES, CONV_DIM), F32),
                        pltpu.VMEM((SSM_STATE, D_INNER), F32),
                        pltpu.VMEM((t, D_INNER), F32)],
        compiler_params=_params(1),
        name="ssm_prompt",
    )(proj, xres, *consts)
    return y, st, cv[SUBLANES - (CONV_WIDTH - 1):]


def _ssm_sample_kernel(p_ref, xres_ref, cst_ref, st_in_ref, cw_ref, cb_ref, dtb_ref, alog_ref,
                       dsk_ref, nw_ref, e_ref, wout_ref, y_ref, st_out_ref, cv_out_ref,
                       ext_ref, st_ref, gy_ref, *, t_new):
    t = SAMPLE_T
    ext_ref[0:SUBLANES, :] = cst_ref[0]
    ext_ref[SUBLANES:SUBLANES + t, :] = p_ref[0, :, D_INNER:D_INNER + CONV_DIM]
    st_ref[...] = st_in_ref[0].T
    out = _ssd_chunk(t, ext_ref, p_ref.at[0], st_ref, gy_ref, cw_ref, cb_ref, dtb_ref, alog_ref,
                     dsk_ref, nw_ref, e_ref, wout_ref, t_new)
    y_ref[0] = xres_ref[0] + out[0:t_new]
    st_out_ref[0] = st_ref[...].T
    cv_out_ref[0] = ext_ref[SUBLANES:SUBLANES + t, :]


def _ssm_sample(proj, xres, conv_state, ssm_state, w):
    b, t_new, _ = proj.shape
    t = SAMPLE_T
    consts = _ssm_consts(w)
    proj8 = jnp.pad(proj, ((0, 0), (0, t - t_new), (0, 0)))
    cst8 = jnp.pad(conv_state, ((0, 0), (SUBLANES - (CONV_WIDTH - 1), 0), (0, 0)))
    y, st, cv = pl.pallas_call(
        functools.partial(_ssm_sample_kernel, t_new=t_new),
        grid=(b,),
        in_specs=[pl.BlockSpec((1, t, SSM_PROJ_PAD), lambda i: (i, 0, 0)),
                  pl.BlockSpec((1, t_new, D_MODEL), lambda i: (i, 0, 0)),
                  pl.BlockSpec((1, SUBLANES, CONV_DIM), lambda i: (i, 0, 0)),
                  pl.BlockSpec((1, D_INNER, SSM_STATE), lambda i: (i, 0, 0))] + _const_specs(consts),
        out_specs=[pl.BlockSpec((1, t_new, D_MODEL), lambda i: (i, 0, 0)),
                   pl.BlockSpec((1, D_INNER, SSM_STATE), lambda i: (i, 0, 0)),
                   pl.BlockSpec((1, t, CONV_DIM), lambda i: (i, 0, 0))],
        out_shape=[jax.ShapeDtypeStruct((b, t_new, D_MODEL), F32),
                   jax.ShapeDtypeStruct((b, D_INNER, SSM_STATE), F32),
                   jax.ShapeDtypeStruct((b, t, CONV_DIM), F32)],
        scratch_shapes=[pltpu.VMEM((2 * SUBLANES, CONV_DIM), F32),
                        pltpu.VMEM((SSM_STATE, D_INNER), F32),
                        pltpu.VMEM((t, D_INNER), F32)],
        compiler_params=_params(1),
        name="ssm_sample",
    )(proj8, xres, cst8, ssm_state, *consts)
    return y, st, cv[:, t_new - (CONV_WIDTH - 1):t_new]


def _ffn_weights(w_in, w_out):
    def chunks(w):
        return w.reshape(D_MODEL, N_FF_CHUNKS, FF_CHUNK).transpose(1, 0, 2).astype(BF16)
    return (chunks(w_in[:, :D_FF]), chunks(w_in[:, D_FF:]),
            w_out.reshape(N_FF_CHUNKS, FF_CHUNK, D_MODEL).astype(BF16))


def _row_tile(rows):
    return 512 if rows % 512 == 0 else 128


def kernel(x_prompt, x_sample, cache_k, cache_v, page_table, state_ssm, state_conv, ffn1_norm, ffn1_w_in, ffn1_w_out, mix_norm, ffn2_norm, ffn2_w_in, ffn2_w_out, attn_w_qkv, attn_w_o, rel_bias, ssm_w_in, ssm_conv_w, ssm_conv_b, ssm_dt_bias, ssm_A_log, ssm_D, ssm_norm, ssm_w_out, final_norm):
    bp, s, _ = x_prompt.shape
    bs, t_new, _ = x_sample.shape
    assert bp == 1 and attn_w_qkv.shape[0] == 1 and ssm_w_in.shape[0] == 1
    n_pool = cache_k.shape[1]
    xp = x_prompt.reshape(s, D_MODEL)
    xs = x_sample.reshape(bs * t_new, D_MODEL)
    tp, ts = _row_tile(s), _row_tile(bs * t_new)

    w1 = _ffn_weights(ffn1_w_in[0], ffn1_w_out[0])
    xp = _ffn(xp, ffn1_norm[0], w1, tp)
    xs = _ffn(xs, ffn1_norm[0], w1, ts)

    wqkv = attn_w_qkv[0].astype(BF16)
    wq, wk, wv = (wqkv[:, i * D_MODEL:(i + 1) * D_MODEL] for i in range(3))
    wo = attn_w_o[0].astype(BF16)
    k_p, v_p, kb, qt, vt, ksum = _qkv_prompt(xp, mix_norm[0], wq.T, wk, wv, wv.T, tp)
    o_p = _moba_prompt(qt, kb, vt, ksum.reshape(s // MOBA_BLOCK, D_MODEL), rel_bias)
    xp = _matres(o_p, wo, xp, tp)

    qkv_s = _proj(xs, mix_norm[0], wqkv, ts).reshape(bs, t_new, 3 * D_MODEL)
    q_s, k_s, v_s = (qkv_s[..., i * D_MODEL:(i + 1) * D_MODEL] for i in range(3))
    ck = cache_k[0].reshape(n_pool, PAGE_SIZE, D_MODEL)
    cv = cache_v[0].reshape(n_pool, PAGE_SIZE, D_MODEL)
    sel = _sample_gate(q_s, _page_sums(ck, page_table))
    o_s = _sample_attn(q_s, k_s, v_s, ck, cv, sel, page_table, rel_bias)
    xs = _matres(o_s.reshape(bs * t_new, D_MODEL), wo, xs, ts)

    w2 = _ffn_weights(ffn2_w_in[0], ffn2_w_out[0])
    xp = _ffn(xp, ffn2_norm[0], w2, tp)
    xs = _ffn(xs, ffn2_norm[0], w2, ts)

    w1 = _ffn_weights(ffn1_w_in[1], ffn1_w_out[1])
    xp = _ffn(xp, ffn1_norm[1], w1, tp)
    xs = _ffn(xs, ffn1_norm[1], w1, ts)

    w_in = jnp.pad(ssm_w_in[0], ((0, 0), (0, SSM_PROJ_PAD - SSM_IN_DIM))).astype(BF16)
    ssm_w = (ssm_conv_w[0], ssm_conv_b[0], ssm_dt_bias[0], ssm_A_log[0], ssm_D[0], ssm_norm[0],
             ssm_w_out[0].astype(BF16))
    xp, st_p, cv_p = _ssm_prompt(_proj(xp, mix_norm[1], w_in, tp), xp, ssm_w)
    proj_s = _proj(xs, mix_norm[1], w_in, ts).reshape(bs, t_new, SSM_PROJ_PAD)
    xs3, st_s, cv_s = _ssm_sample(proj_s, xs.reshape(bs, t_new, D_MODEL), state_conv[0],
                                  state_ssm[0].reshape(bs, D_INNER, SSM_STATE), ssm_w)
    xs = xs3.reshape(bs * t_new, D_MODEL)

    w2 = _ffn_weights(ffn2_w_in[1], ffn2_w_out[1])
    y_p = _ffn(xp, ffn2_norm[1], w2, tp, final_g=final_norm)
    y_s = _ffn(xs, ffn2_norm[1], w2, ts, final_g=final_norm)

    head_shape = (N_HEADS, HEAD_DIM)
    state_shape = (SSM_HEADS, SSM_HEAD_DIM, SSM_STATE)
    return (y_p.reshape(1, s, D_MODEL), y_s.reshape(bs, t_new, D_MODEL),
            k_p.reshape(1, 1, s // PAGE_SIZE, PAGE_SIZE, *head_shape),
            v_p.reshape(1, 1, s // PAGE_SIZE, PAGE_SIZE, *head_shape),
            k_s.reshape(1, bs, t_new, *head_shape), v_s.reshape(1, bs, t_new, *head_shape),
            st_p.reshape(1, 1, *state_shape), cv_p.reshape(1, 1, CONV_WIDTH - 1, CONV_DIM),
            st_s.reshape(1, bs, *state_shape), cv_s.reshape(1, bs, CONV_WIDTH - 1, CONV_DIM))
```

```python
import functools
import math

import jax
import jax.numpy as jnp
from jax import lax
from jax.experimental import pallas as pl
from jax.experimental.pallas import tpu as pltpu

F32 = jnp.float32
BF16 = jnp.bfloat16
I32 = jnp.int32
NEG_INF = float("-inf")

D_MODEL = 1024
N_HEADS = 16
HEAD_DIM = 64
MOBA_BLOCK = 256
MOBA_TOPK = 3
PAGE_SIZE = 128
REL_BUCKETS = 32
REL_MAX_DIST = 128
D_INNER = 2048
SSM_HEADS = 32
SSM_HEAD_DIM = 64
SSM_GROUPS = 4
SSM_STATE = 128
CONV_WIDTH = 4
CONV_DIM = D_INNER + 2 * SSM_GROUPS * SSM_STATE
SSM_IN_DIM = 2 * D_INNER + 2 * SSM_GROUPS * SSM_STATE + SSM_HEADS
SSD_CHUNK = 256
D_FF = 2816
RMS_EPS = 1e-6

LANES = 128
SUBLANES = 8
VMEM_LIMIT_BYTES = 56 * 1024 * 1024

FF_CHUNK = 256
N_FF_CHUNKS = D_FF // FF_CHUNK
HEADS_PER_SLAB = LANES // HEAD_DIM
N_SLABS = N_HEADS // HEADS_PER_SLAB
GROUP_DIM = D_INNER // SSM_GROUPS
HEADS_PER_GROUP = SSM_HEADS // SSM_GROUPS
SSM_PROJ_PAD = 5248
DT_OFF = D_INNER + CONV_DIM
SAMPLE_T = 8
PAGES_PER_BLOCK = MOBA_BLOCK // PAGE_SIZE
KV_GROUP = 4
ATTN_HEADS = 4
ATTN_WIDTH = ATTN_HEADS * HEAD_DIM


def _params(n_axes, vmem=VMEM_LIMIT_BYTES):
    return pltpu.CompilerParams(dimension_semantics=("arbitrary",) * n_axes,
                                vmem_limit_bytes=vmem)


def _whole(shape):
    nd = len(shape)
    return pl.BlockSpec(shape, lambda *_: (0,) * nd)


def _smem():
    return pl.BlockSpec(memory_space=pltpu.SMEM)


def _rms(x, g):
    ms = jnp.mean(x * x, axis=-1, keepdims=True)
    return x * lax.rsqrt(ms + RMS_EPS) * g


def _silu(x):
    return x * jax.nn.sigmoid(x)


def _softplus(x):
    return jnp.maximum(x, 0.0) + jnp.log1p(jnp.exp(-jnp.abs(x)))


def _split3(x):
    hi = x.astype(BF16)
    r1 = x - hi.astype(F32)
    mid = r1.astype(BF16)
    lo = (r1 - mid.astype(F32)).astype(BF16)
    return hi, mid, lo


def _select_matmul(x, sel):
    hi, mid, lo = _split3(x)
    dot = functools.partial(jnp.dot, preferred_element_type=F32)
    return dot(hi, sel) + dot(mid, sel) + dot(lo, sel)


def _top3_rows(s, n_iota, sentinel):
    picks = []
    for _ in range(MOBA_TOPK):
        m = jnp.max(s, axis=0, keepdims=True)
        idx = jnp.min(jnp.where(s == m, n_iota, sentinel), axis=0, keepdims=True)
        picks.append((idx, m))
        s = jnp.where(n_iota == idx, NEG_INF, s)
    return picks


def _bias_from_buckets(bk, rb_ref, h):
    out = jnp.full(bk.shape, NEG_INF, F32)
    for b in range(REL_BUCKETS):
        out = jnp.where(bk == b, rb_ref[b, h], out)
    return out


def _rel_bucket(dist):
    n = jnp.maximum(dist, 0)
    max_exact = REL_BUCKETS // 2
    nf = jnp.maximum(n, 1).astype(F32)
    large = max_exact + (jnp.log(nf / max_exact) / math.log(REL_MAX_DIST / max_exact)
                         * (REL_BUCKETS - max_exact)).astype(I32)
    large = jnp.minimum(large, REL_BUCKETS - 1)
    return jnp.where(n < max_exact, n, large).astype(I32)


def _ffn_kernel(*refs, final):
    if final:
        x_ref, g_ref, wa_ref, wb_ref, wo_ref, fg_ref, o_ref, xn_ref, acc_ref = refs
    else:
        x_ref, g_ref, wa_ref, wb_ref, wo_ref, o_ref, xn_ref, acc_ref = refs
    xn_ref[...] = _rms(x_ref[...], g_ref[...]).astype(BF16)
    acc_ref[...] = jnp.zeros_like(acc_ref)

    def chunk(c, carry):
        xn = xn_ref[...]
        a = jnp.dot(xn, wa_ref[c], preferred_element_type=F32)
        b = jnp.dot(xn, wb_ref[c], preferred_element_type=F32)
        h = (_silu(a) * b).astype(BF16)
        acc_ref[...] += jnp.dot(h, wo_ref[c], preferred_element_type=F32)
        return carry

    lax.fori_loop(0, N_FF_CHUNKS, chunk, 0)
    y = x_ref[...] + 0.5 * acc_ref[...]
    if final:
        y = _rms(y, fg_ref[...])
    o_ref[...] = y


def _ffn(x, g, w, tm, final_g=None):
    rows = x.shape[0]
    wa, wb, wo = w
    row_spec = pl.BlockSpec((tm, D_MODEL), lambda i: (i, 0))
    in_specs = [row_spec, _whole((1, D_MODEL)), _whole(wa.shape), _whole(wb.shape), _whole(wo.shape)]
    args = [x, g.reshape(1, D_MODEL), wa, wb, wo]
    if final_g is not None:
        in_specs.append(_whole((1, D_MODEL)))
        args.append(final_g.reshape(1, D_MODEL))
    return pl.pallas_call(
        functools.partial(_ffn_kernel, final=final_g is not None),
        grid=(rows // tm,),
        in_specs=in_specs,
        out_specs=row_spec,
        out_shape=jax.ShapeDtypeStruct((rows, D_MODEL), F32),
        scratch_shapes=[pltpu.VMEM((tm, D_MODEL), BF16), pltpu.VMEM((tm, D_MODEL), F32)],
        compiler_params=_params(1),
        name="ffn",
    )(*args)


def _proj_kernel(x_ref, g_ref, w_ref, o_ref, xn_ref, *, widths):
    xn_ref[...] = _rms(x_ref[...], g_ref[...]).astype(BF16)
    lo = 0
    for wd in widths:
        o_ref[:, lo:lo + wd] = jnp.dot(xn_ref[...], w_ref[:, lo:lo + wd], preferred_element_type=F32)
        lo += wd


def _proj(x, g, w, tm):
    rows, n = x.shape[0], w.shape[1]
    col_chunk = 4 * LANES
    widths = [col_chunk] * (n // col_chunk) + ([n % col_chunk] if n % col_chunk else [])
    return pl.pallas_call(
        functools.partial(_proj_kernel, widths=tuple(widths)),
        grid=(rows // tm,),
        in_specs=[pl.BlockSpec((tm, D_MODEL), lambda i: (i, 0)), _whole((1, D_MODEL)), _whole(w.shape)],
        out_specs=pl.BlockSpec((tm, n), lambda i: (i, 0)),
        out_shape=jax.ShapeDtypeStruct((rows, n), F32),
        scratch_shapes=[pltpu.VMEM((tm, D_MODEL), BF16)],
        compiler_params=_params(1),
        name="norm_proj",
    )(x, g.reshape(1, D_MODEL), w)


def _matres_kernel(a_ref, w_ref, r_ref, o_ref):
    o_ref[...] = r_ref[...] + jnp.dot(a_ref[...].astype(BF16), w_ref[...], preferred_element_type=F32)


def _matres(a, w, res, tm):
    rows, k = a.shape
    n = w.shape[1]
    return pl.pallas_call(
        _matres_kernel,
        grid=(rows // tm,),
        in_specs=[pl.BlockSpec((tm, k), lambda i: (i, 0)), _whole(w.shape),
                  pl.BlockSpec((tm, n), lambda i: (i, 0))],
        out_specs=pl.BlockSpec((tm, n), lambda i: (i, 0)),
        out_shape=jax.ShapeDtypeStruct((rows, n), F32),
        compiler_params=_params(1),
        name="matmul_residual",
    )(a, w, res)


def _qkv_prompt_kernel(x_ref, g_ref, wqt_ref, wk_ref, wv_ref, wvt_ref,
                       k_ref, v_ref, kb_ref, qt_ref, vt_ref, ks_ref, xn_ref, *, tm):
    xn_ref[...] = _rms(x_ref[...], g_ref[...]).astype(BF16)
    xn = xn_ref[...]
    k = jnp.dot(xn, wk_ref[...], preferred_element_type=F32)
    k_ref[...] = k
    kb_ref[...] = k.astype(BF16)
    for j in range(tm // MOBA_BLOCK):
        ks_ref[0, j:j + 1, :] = jnp.sum(k[j * MOBA_BLOCK:(j + 1) * MOBA_BLOCK], axis=0, keepdims=True)
    v_ref[...] = jnp.dot(xn, wv_ref[...], preferred_element_type=F32)
    nt = (((1,), (1,)), ((), ()))
    qt = lax.dot_general(wqt_ref[...], xn, nt, preferred_element_type=F32)
    qt_ref[...] = (qt * (HEAD_DIM ** -0.5)).astype(BF16)
    vt = lax.dot_general(wvt_ref[...], xn, nt, preferred_element_type=F32)
    for j in range(tm // MOBA_BLOCK):
        vt_ref[j] = vt[:, j * MOBA_BLOCK:(j + 1) * MOBA_BLOCK].astype(BF16)


def _qkv_prompt(x, g, wqt, wk, wv, wvt, tm):
    s = x.shape[0]
    nblk = s // MOBA_BLOCK
    bpt = tm // MOBA_BLOCK
    row_spec = pl.BlockSpec((tm, D_MODEL), lambda i: (i, 0))
    sq = _whole((D_MODEL, D_MODEL))
    return pl.pallas_call(
        functools.partial(_qkv_prompt_kernel, tm=tm),
        grid=(s // tm,),
        in_specs=[row_spec, _whole((1, D_MODEL)), sq, sq, sq, sq],
        out_specs=[row_spec, row_spec, row_spec,
                   pl.BlockSpec((D_MODEL, tm), lambda i: (0, i)),
                   pl.BlockSpec((bpt, D_MODEL, MOBA_BLOCK), lambda i: (i, 0, 0)),
                   pl.BlockSpec((1, bpt, D_MODEL), lambda i: (i, 0, 0))],
        out_shape=[jax.ShapeDtypeStruct((s, D_MODEL), F32),
                   jax.ShapeDtypeStruct((s, D_MODEL), F32),
                   jax.ShapeDtypeStruct((s, D_MODEL), BF16),
                   jax.ShapeDtypeStruct((D_MODEL, s), BF16),
                   jax.ShapeDtypeStruct((nblk, D_MODEL, MOBA_BLOCK), BF16),
                   jax.ShapeDtypeStruct((s // tm, bpt, D_MODEL), F32)],
        scratch_shapes=[pltpu.VMEM((tm, D_MODEL), BF16)],
        compiler_params=_params(1),
        name="qkv_prompt",
    )(x, g.reshape(1, D_MODEL), wqt, wk, wv, wvt)


def _softmax_group_step(tiles, v_tiles, state):
    m, l, acc = state
    m_new = m
    for s, keep, shift in tiles:
        bm = jnp.max(s, axis=0, keepdims=True) + shift
        m_new = jnp.maximum(m_new, bm if keep is None else jnp.where(keep > 0.0, bm, NEG_INF))
    alpha = jnp.exp(m - m_new)
    l = alpha * l
    acc = alpha * acc
    for (s, keep, shift), v_t in zip(tiles, v_tiles):
        off = m_new - shift
        if keep is not None:
            off = jnp.where(keep > 0.0, off, jnp.inf)
        p = jnp.exp(s - off)
        l = l + jnp.sum(p, axis=0, keepdims=True)
        acc = acc + jnp.dot(v_t, p.astype(BF16), preferred_element_type=F32)
    return m_new, l, acc


def _moba_prompt_kernel(rb_ref, qt_ref, kb_ref, vt_ref, ks_ref, bkt_ref, o_ref, sel_ref, bias_ref, *, nblk):
    slab = pl.program_id(0)
    qi = pl.program_id(1)
    blk = MOBA_BLOCK
    heads = range(ATTN_HEADS)

    @pl.when(qi == 0)
    def _build_bias():
        for hh in heads:
            for kind in range(2):
                bias_ref[hh, kind] = _bias_from_buckets(bkt_ref[kind], rb_ref, slab * ATTN_HEADS + hh)

    qt = qt_ref[...].astype(F32)
    row_head = lax.broadcasted_iota(I32, (ATTN_WIDTH, blk), 0) // HEAD_DIM
    n_iota = lax.broadcasted_iota(I32, (nblk, blk), 0)
    kmean = (ks_ref[...] * (1.0 / blk)).astype(BF16)
    far_bias = [rb_ref[REL_BUCKETS - 1, slab * ATTN_HEADS + hh] for hh in heads]
    qz = [jnp.where(row_head == hh, qt, 0.0).astype(BF16) for hh in heads]
    for hh in heads:
        scores = jnp.dot(kmean, qz[hh], preferred_element_type=F32)
        scores = jnp.where(n_iota < qi, scores, NEG_INF)
        sel = jnp.zeros((nblk, blk), F32)
        for idx, m in _top3_rows(scores, n_iota, nblk):
            sel = jnp.where((n_iota == idx) & (m > NEG_INF), 1.0, sel)
        sel_ref[hh] = sel

    def block_tiles(hh, n, kind):
        nc = jnp.maximum(n, 0)
        k_n = kb_ref[pl.ds(pl.multiple_of(nc * blk, blk), blk), :]
        s = jnp.dot(k_n, qz[hh], preferred_element_type=F32)
        v_t = vt_ref[nc, hh * HEAD_DIM:(hh + 1) * HEAD_DIM, :]
        if kind == "own":
            return (s + bias_ref[hh, 0], None, 0.0), v_t
        keep = jnp.where(n >= 0, sel_ref[hh, pl.ds(nc, 1), :], 0.0)
        if kind == "prev":
            return (s + bias_ref[hh, 1], keep, 0.0), v_t
        return (s, keep, far_bias[hh]), v_t

    def group_step(blocks, states):
        out = []
        for hh in heads:
            tiles, v_tiles = zip(*[block_tiles(hh, n, kind) for n, kind in blocks])
            out.append(_softmax_group_step(tiles, v_tiles, states[hh]))
        return tuple(out)

    init = tuple((jnp.full((1, blk), NEG_INF, F32), jnp.zeros((1, blk), F32),
                  jnp.zeros((HEAD_DIM, blk), F32)) for _ in heads)
    near = [(qi - j, "far") for j in range(KV_GROUP - 1, 1, -1)] + [(qi - 1, "prev"), (qi, "own")]
    states = group_step(near, init)

    def far_group(g, states):
        base = qi - (KV_GROUP - 1) - KV_GROUP * (g + 1)
        return group_step([(base + j, "far") for j in range(KV_GROUP)], states)

    n_far = jnp.maximum(qi - (KV_GROUP - 1), 0)
    states = lax.fori_loop(0, (n_far + KV_GROUP - 1) // KV_GROUP, far_group, states)
    outs = [acc / l for _, l, acc in states]
    o_ref[...] = jnp.concatenate(outs, axis=0).T.astype(BF16)


def _prompt_bucket_maps():
    kk = jnp.arange(MOBA_BLOCK, dtype=I32)[:, None]
    tt = jnp.arange(MOBA_BLOCK, dtype=I32)[None, :]
    own = jnp.where(tt - kk >= 0, _rel_bucket(tt - kk), -1)
    prev = _rel_bucket(MOBA_BLOCK + tt - kk)
    return jnp.stack([own, prev]).astype(I32)


def _moba_prompt(qt, kb, vt, ksum, rel_bias):
    s = kb.shape[0]
    nblk = s // MOBA_BLOCK
    return pl.pallas_call(
        functools.partial(_moba_prompt_kernel, nblk=nblk),
        grid=(N_HEADS // ATTN_HEADS, nblk),
        in_specs=[_smem(),
                  pl.BlockSpec((ATTN_WIDTH, MOBA_BLOCK), lambda sl, qi: (sl, qi)),
                  pl.BlockSpec((s, ATTN_WIDTH), lambda sl, qi: (0, sl), pipeline_mode=pl.Buffered(1)),
                  pl.BlockSpec((nblk, ATTN_WIDTH, MOBA_BLOCK), lambda sl, qi: (0, sl, 0),
                               pipeline_mode=pl.Buffered(1)),
                  pl.BlockSpec((nblk, ATTN_WIDTH), lambda sl, qi: (0, sl)),
                  _whole((2, MOBA_BLOCK, MOBA_BLOCK))],
        out_specs=pl.BlockSpec((MOBA_BLOCK, ATTN_WIDTH), lambda sl, qi: (qi, sl)),
        out_shape=jax.ShapeDtypeStruct((s, D_MODEL), BF16),
        scratch_shapes=[pltpu.VMEM((ATTN_HEADS, nblk, MOBA_BLOCK), F32),
                        pltpu.VMEM((ATTN_HEADS, 2, MOBA_BLOCK, MOBA_BLOCK), F32)],
        compiler_params=_params(2),
        name="moba_prompt",
    )(rel_bias, qt, kb, vt, ksum, _prompt_bucket_maps())


def _page_sum_kernel(pt_ref, *refs, pps):
    del pt_ref
    pages, o_ref = refs[:pps], refs[pps]
    for i in range(pps // PAGES_PER_BLOCK):
        tot = jnp.sum(pages[PAGES_PER_BLOCK * i][0, 0], axis=0)
        for j in range(1, PAGES_PER_BLOCK):
            tot = tot + jnp.sum(pages[PAGES_PER_BLOCK * i + j][0, 0], axis=0)
        o_ref[0, i] = tot


def _page_sums(cache, page_table):
    b, npg = page_table.shape
    pps = min(8, npg)
    bps = pps // PAGES_PER_BLOCK

    def page_spec(i):
        return pl.BlockSpec((1, 1, PAGE_SIZE, N_HEADS, HEAD_DIM),
                            lambda bi, j, pt: (0, pt[bi * npg + j * pps + i], 0, 0, 0))

    sums = pl.pallas_call(
        functools.partial(_page_sum_kernel, pps=pps),
        grid_spec=pltpu.PrefetchScalarGridSpec(
            num_scalar_prefetch=1,
            grid=(b, npg // pps),
            in_specs=[page_spec(i) for i in range(pps)],
            out_specs=pl.BlockSpec((1, bps, N_HEADS, HEAD_DIM),
                                   lambda bi, j, pt: (bi * (npg // pps) + j, 0, 0, 0)),
        ),
        out_shape=jax.ShapeDtypeStruct((b * (npg // pps), bps, N_HEADS, HEAD_DIM), F32),
        compiler_params=_params(2),
        name="page_sums",
    )(page_table.reshape(-1), *([cache] * pps))
    return sums.reshape(b, npg // PAGES_PER_BLOCK, D_MODEL)


def _sample_gate_kernel(q_ref, bs_ref, seg_ref, sel_ref, *, t_new, nblk):
    kmean = bs_ref[0] * (1.0 / MOBA_BLOCK)
    n_iota = lax.broadcasted_iota(I32, (nblk, LANES), 0)
    sel_ref[...] = jnp.zeros_like(sel_ref)
    for t in range(t_new):
        prod = kmean * q_ref[0, t:t + 1, :]
        scores = _select_matmul(prod, seg_ref[...])
        for r, (idx, _) in enumerate(_top3_rows(scores, n_iota, nblk)):
            sel_ref[0, t, r:r + 1, :] = idx


def _sample_gate(q, bsum):
    b, t_new, _ = q.shape
    nblk = bsum.shape[1]
    seg = (jnp.arange(D_MODEL, dtype=I32)[:, None] // HEAD_DIM
           == jnp.arange(LANES, dtype=I32)[None, :]).astype(BF16)
    sel = pl.pallas_call(
        functools.partial(_sample_gate_kernel, t_new=t_new, nblk=nblk),
        grid=(b,),
        in_specs=[pl.BlockSpec((1, t_new, D_MODEL), lambda i: (i, 0, 0)),
                  pl.BlockSpec((1, nblk, D_MODEL), lambda i: (i, 0, 0)),
                  _whole((D_MODEL, LANES))],
        out_specs=pl.BlockSpec((1, t_new, SUBLANES, LANES), lambda i: (i, 0, 0, 0)),
        out_shape=jax.ShapeDtypeStruct((b, t_new, SUBLANES, LANES), I32),
        compiler_params=_params(1),
        name="sample_gate",
    )(q, bsum, seg)
    return sel[:, :, :MOBA_TOPK, :N_HEADS]


SLABS_PER_HEAD = MOBA_TOPK * PAGES_PER_BLOCK


def _sample_attn_kernel(sel_ref, pt_ref, rb_ref, q_ref, kn_ref, vn_ref, pb_ref, ck_hbm, cv_hbm, o_ref,
                        kbuf, vbuf, sems, o_scr, *, t_new, npg):
    bi, t = pl.program_id(0), pl.program_id(1)
    step = bi * t_new + t
    n_steps = pl.num_programs(0) * t_new
    slot = lax.rem(step, 2)
    last_blk = npg // PAGES_PER_BLOCK - 1

    def selected_block(b_, t_, j, h):
        return sel_ref[((b_ * t_new + t_) * MOBA_TOPK + j) * N_HEADS + h]

    def slab_copies(b_, t_, slot_):
        copies = []
        for h in range(N_HEADS):
            for j in range(MOBA_TOPK):
                blk = selected_block(b_, t_, j, h)
                for pg in range(PAGES_PER_BLOCK):
                    page = pt_ref[b_ * npg + blk * PAGES_PER_BLOCK + pg]
                    i = j * PAGES_PER_BLOCK + pg
                    copies.append(pltpu.make_async_copy(ck_hbm.at[0, page, :, h, :], kbuf.at[slot_, h, i],
                                                        sems.at[0, slot_]))
                    copies.append(pltpu.make_async_copy(cv_hbm.at[0, page, :, h, :], vbuf.at[slot_, h, i],
                                                        sems.at[1, slot_]))
        return copies

    @pl.when(step == 0)
    def _first_fetch():
        for c in slab_copies(bi, t, slot):
            c.start()

    @pl.when(step + 1 < n_steps)
    def _prefetch_next():
        nxt = step + 1
        for c in slab_copies(lax.div(nxt, t_new), lax.rem(nxt, t_new), 1 - slot):
            c.start()

    for c in slab_copies(bi, t, slot):
        c.wait()

    row = lax.broadcasted_iota(I32, (SUBLANES, 1), 0)
    q = q_ref[0, 0] * (HEAD_DIM ** -0.5)
    for h in range(N_HEADS):
        cols = slice(h * HEAD_DIM, (h + 1) * HEAD_DIM)
        far_bias = rb_ref[REL_BUCKETS - 1, h]
        qh = q[:, cols]
        q8 = jnp.broadcast_to(qh, (SUBLANES, HEAD_DIM)).astype(BF16)
        keys = kbuf[slot, h].reshape(SLABS_PER_HEAD * PAGE_SIZE, HEAD_DIM).astype(BF16)
        l1 = lax.dot_general(q8, keys, (((1,), (1,)), ((), ())), preferred_element_type=F32)[0:1]
        bias = []
        for j in range(MOBA_TOPK):
            blk_j = selected_block(bi, t, j, h)
            for pg in range(PAGES_PER_BLOCK):
                near = _bias_from_buckets(pb_ref[pl.ds(t * PAGES_PER_BLOCK + pg, 1), :], rb_ref, h)
                bias.append(jnp.where(blk_j == last_blk, near, far_bias))
        l1 = l1 + jnp.concatenate(bias, axis=1)
        l2 = jnp.sum(kn_ref[0, :, cols] * qh, axis=-1, keepdims=True)
        bias2 = jnp.zeros((SUBLANES, 1), F32)
        for u in range(t_new):
            bias2 = jnp.where(row == u, rb_ref[jnp.maximum(t - u, 0), h], bias2)
        l2 = jnp.where(row <= t, l2 + bias2, NEG_INF)
        m = jnp.maximum(jnp.max(l1, axis=-1, keepdims=True), jnp.max(l2, axis=0, keepdims=True))
        p1 = jnp.exp(l1 - m)
        p2 = jnp.exp(l2 - m)
        denom = jnp.sum(p1, axis=-1, keepdims=True) + jnp.sum(p2, axis=0, keepdims=True)
        vals = vbuf[slot, h].reshape(SLABS_PER_HEAD * PAGE_SIZE, HEAD_DIM).astype(BF16)
        p8 = jnp.broadcast_to(p1, (SUBLANES, SLABS_PER_HEAD * PAGE_SIZE)).astype(BF16)
        o = jnp.dot(p8, vals, preferred_element_type=F32)[0:1]
        o = o + jnp.sum(p2 * vn_ref[0, :, cols], axis=0, keepdims=True)
        o_scr[:, cols] = jnp.broadcast_to(o / denom, (SUBLANES, HEAD_DIM))
    o_ref[0, 0] = o_scr[0:1, :]


def _sample_attn(q, k_new, v_new, cache_k, cache_v, sel, page_table, rel_bias):
    b, t_new, _ = q.shape
    npg = page_table.shape[1]
    pad = ((0, 0), (0, SUBLANES - t_new), (0, 0))
    tt = jnp.arange(t_new, dtype=I32)[:, None, None]
    pg = jnp.arange(PAGES_PER_BLOCK, dtype=I32)[None, :, None]
    rr = jnp.arange(PAGE_SIZE, dtype=I32)[None, None, :]
    near_bkt = _rel_bucket(MOBA_BLOCK + tt - pg * PAGE_SIZE - rr).reshape(t_new * PAGES_PER_BLOCK, PAGE_SIZE)

    new_spec = pl.BlockSpec((1, SUBLANES, D_MODEL), lambda bi, t, *_: (bi, 0, 0))
    tok_spec = pl.BlockSpec((1, 1, 1, D_MODEL), lambda bi, t, *_: (bi, t, 0, 0))
    slab_buf = pltpu.VMEM((2, N_HEADS, SLABS_PER_HEAD, PAGE_SIZE, HEAD_DIM), F32)
    out = pl.pallas_call(
        functools.partial(_sample_attn_kernel, t_new=t_new, npg=npg),
        grid_spec=pltpu.PrefetchScalarGridSpec(
            num_scalar_prefetch=2,
            grid=(b, t_new),
            in_specs=[_smem(), tok_spec, new_spec, new_spec,
                      pl.BlockSpec((t_new * PAGES_PER_BLOCK, PAGE_SIZE), lambda *_: (0, 0)),
                      pl.BlockSpec(memory_space=pl.ANY), pl.BlockSpec(memory_space=pl.ANY)],
            out_specs=tok_spec,
            scratch_shapes=[slab_buf, slab_buf, pltpu.SemaphoreType.DMA((2, 2)),
                            pltpu.VMEM((SUBLANES, D_MODEL), F32)],
        ),
        out_shape=jax.ShapeDtypeStruct((b, t_new, 1, D_MODEL), F32),
        compiler_params=_params(2),
        name="sample_attn",
    )(sel.reshape(-1), page_table.reshape(-1), rel_bias,
      q.reshape(b, t_new, 1, D_MODEL), jnp.pad(k_new, pad), jnp.pad(v_new, pad), near_bkt,
      cache_k, cache_v)
    return out.reshape(b, t_new, D_MODEL)


def _ssd_chunk(t, ext_ref, p_ref, st_ref, gy_ref, cw_ref, cb_ref, dtb_ref, alog_ref, dsk_ref,
               nw_ref, e_ref, wout_ref, n_valid):
    def conv_act(lo, width):
        acc = cb_ref[:, lo:lo + width]
        for j in range(CONV_WIDTH):
            acc = acc + cw_ref[j:j + 1, lo:lo + width] * ext_ref[5 + j:5 + j + t, lo:lo + width]
        return _silu(acc)

    dt = _softplus(p_ref[:, DT_OFF:DT_OFF + LANES] + dtb_ref[...])
    if n_valid < t:
        dt = jnp.where(lax.broadcasted_iota(I32, (t, LANES), 0) < n_valid, dt, 0.0)
    a = dt * -jnp.exp(alog_ref[...])
    r_i = lax.broadcasted_iota(I32, (t, t), 0)
    c_i = lax.broadcasted_iota(I32, (t, t), 1)
    causal = r_i >= c_i
    acs = jnp.dot(causal.astype(F32), a, preferred_element_type=F32,
                  precision=lax.Precision.HIGHEST)
    if t % LANES == 0:
        acs_t = acs.T
    else:
        acs_t = lax.dot_general(jnp.eye(LANES, dtype=F32), acs, (((1,), (1,)), ((), ())),
                                preferred_element_type=F32, precision=lax.Precision.HIGHEST)
    lane_hi = lax.broadcasted_iota(I32, (t, LANES), 1) >= SSM_HEAD_DIM

    for g in range(SSM_GROUPS):
        ch = slice(g * GROUP_DIM, (g + 1) * GROUP_DIM)
        e_g = e_ref[:, ch]
        xs = conv_act(g * GROUP_DIM, GROUP_DIM)
        b_g = conv_act(D_INNER + g * SSM_STATE, SSM_STATE).astype(BF16)
        c_g = conv_act(D_INNER + (SSM_GROUPS + g) * SSM_STATE, SSM_STATE).astype(BF16)
        dt_g = _select_matmul(dt, e_g)
        acs_g = _select_matmul(acs, e_g)
        last_g = acs_g[t - 1:t, :]
        xdt = xs * dt_g
        xdt_b = xdt.astype(BF16)
        xdt_dec = (xdt * jnp.exp(last_g - acs_g)).astype(BF16)
        cb = lax.dot_general(c_g, b_g, (((1,), (1,)), ((), ())), preferred_element_type=F32)
        st_in = st_ref[:, ch]
        y_off = jnp.dot(c_g, st_in.astype(BF16), preferred_element_type=F32) * jnp.exp(acs_g)
        st_new = lax.dot_general(b_g, xdt_dec, (((0,), (0,)), ((), ())), preferred_element_type=F32)
        st_ref[:, ch] = st_in * jnp.exp(last_g) + st_new
        for j in range(HEADS_PER_GROUP // HEADS_PER_SLAB):
            cols = slice(j * LANES, (j + 1) * LANES)
            pair = xdt_b[:, cols]
            ys = []
            for hh in range(HEADS_PER_SLAB):
                h = g * HEADS_PER_GROUP + j * HEADS_PER_SLAB + hh
                decay = jnp.exp(jnp.where(causal, acs[:, h:h + 1] - acs_t[h:h + 1, :], NEG_INF))
                ys.append(jnp.dot((cb * decay).astype(BF16), pair, preferred_element_type=F32))
            y_diag = jnp.where(lane_hi, ys[1], ys[0])
            lo = g * GROUP_DIM + j * LANES
            y = y_diag + y_off[:, cols] + dsk_ref[:, lo:lo + LANES] * xs[:, cols]
            gy_ref[:, lo:lo + LANES] = y * _silu(p_ref[:, lo:lo + LANES])
        gy = gy_ref[:, ch]
        ms = jnp.mean(gy * gy, axis=-1, keepdims=True)
        gy_ref[:, ch] = gy * lax.rsqrt(ms + RMS_EPS) * nw_ref[:, ch]
    return jnp.dot(gy_ref[...].astype(BF16), wout_ref[...], preferred_element_type=F32)


def _ssm_prompt_kernel(p_ref, xres_ref, cw_ref, cb_ref, dtb_ref, alog_ref, dsk_ref, nw_ref, e_ref,
                       wout_ref, y_ref, st_out_ref, cv_out_ref, ext_ref, st_ref, gy_ref):
    c = pl.program_id(0)
    t = SSD_CHUNK

    @pl.when(c == 0)
    def _init():
        ext_ref[t:t + SUBLANES, :] = jnp.zeros((SUBLANES, CONV_DIM), F32)
        st_ref[...] = jnp.zeros_like(st_ref)

    ext_ref[0:SUBLANES, :] = ext_ref[t:t + SUBLANES, :]
    ext_ref[SUBLANES:SUBLANES + t, :] = p_ref[:, D_INNER:D_INNER + CONV_DIM]
    out = _ssd_chunk(t, ext_ref, p_ref, st_ref, gy_ref, cw_ref, cb_ref, dtb_ref, alog_ref, dsk_ref,
                     nw_ref, e_ref, wout_ref, t)
    y_ref[...] = xres_ref[...] + out

    @pl.when(c == pl.num_programs(0) - 1)
    def _finish():
        st_out_ref[...] = st_ref[...].T
        cv_out_ref[...] = ext_ref[t:t + SUBLANES, :]


def _ssm_consts(w):
    conv_w, conv_b, dt_bias, a_log, d_skip, norm_w, w_out = w
    lane_pad = (0, LANES - SSM_HEADS)
    expand = (jnp.arange(LANES, dtype=I32)[:, None]
              == jnp.arange(D_INNER, dtype=I32)[None, :] // SSM_HEAD_DIM).astype(BF16)
    return (conv_w, conv_b.reshape(1, CONV_DIM),
            jnp.pad(dt_bias, lane_pad).reshape(1, LANES), jnp.pad(a_log, lane_pad).reshape(1, LANES),
            jnp.repeat(d_skip, SSM_HEAD_DIM).reshape(1, D_INNER), norm_w.reshape(1, D_INNER),
            expand, w_out)


def _const_specs(consts):
    return [_whole(c.shape) for c in consts]


def _ssm_prompt(proj, xres, w):
    s = proj.shape[0]
    t = SSD_CHUNK
    consts = _ssm_consts(w)
    y, st, cv = pl.pallas_call(
        _ssm_prompt_kernel,
        grid=(s // t,),
        in_specs=[pl.BlockSpec((t, SSM_PROJ_PAD), lambda c: (c, 0)),
                  pl.BlockSpec((t, D_MODEL), lambda c: (c, 0))] + _const_specs(consts),
        out_specs=[pl.BlockSpec((t, D_MODEL), lambda c: (c, 0)),
                   _whole((D_INNER, SSM_STATE)), _whole((SUBLANES, CONV_DIM))],
        out_shape=[jax.ShapeDtypeStruct((s, D_MODEL), F32),
                   jax.ShapeDtypeStruct((D_INNER, SSM_STATE), F32),
                   jax.ShapeDtypeStruct((SUBLANES, CONV_DIM), F32)],
        scratch_shapes=[pltpu.VMEM((t + 2 * SUBLANES, CONV_DIM), F32),
                        pltpu.VMEM((SSM_STATE, D_INNER), F32),
                        pltpu.VMEM((t, D_INNER), F32)],
        compiler_params=_params(1),
        name="ssm_prompt",
    )(proj, xres, *consts)
    return y, st, cv[SUBLANES - (CONV_WIDTH - 1):]


def _ssm_sample_kernel(p_ref, xres_ref, cst_ref, st_in_ref, cw_ref, cb_ref, dtb_ref, alog_ref,
                       dsk_ref, nw_ref, e_ref, wout_ref, y_ref, st_out_ref, cv_out_ref,
                       ext_ref, st_ref, gy_ref, *, t_new):
    t = SAMPLE_T
    ext_ref[0:SUBLANES, :] = cst_ref[0]
    ext_ref[SUBLANES:SUBLANES + t, :] = p_ref[0, :, D_INNER:D_INNER + CONV_DIM]
    st_ref[...] = st_in_ref[0].T
    out = _ssd_chunk(t, ext_ref, p_ref.at[0], st_ref, gy_ref, cw_ref, cb_ref, dtb_ref, alog_ref,
                     dsk_ref, nw_ref, e_ref, wout_ref, t_new)
    y_ref[0] = xres_ref[0] + out[0:t_new]
    st_out_ref[0] = st_ref[...].T
    cv_out_ref[0] = ext_ref[SUBLANES:SUBLANES + t, :]


def _ssm_sample(proj, xres, conv_state, ssm_state, w):
    b, t_new, _ = proj.shape
    t = SAMPLE_T
    consts = _ssm_consts(w)
    proj8 = jnp.pad(proj, ((0, 0), (0, t - t_new), (0, 0)))
    cst8 = jnp.pad(conv_state, ((0, 0), (SUBLANES - (CONV_WIDTH - 1), 0), (0, 0)))
    y, st, cv = pl.pallas_call(
        functools.partial(_ssm_sample_kernel, t_new=t_new),
        grid=(b,),
        in_specs=[pl.BlockSpec((1, t, SSM_PROJ_PAD), lambda i: (i, 0, 0)),
                  pl.BlockSpec((1, t_new, D_MODEL), lambda i: (i, 0, 0)),
                  pl.BlockSpec((1, SUBLANES, CONV_DIM), lambda i: (i, 0, 0)),
                  pl.BlockSpec((1, D_INNER, SSM_STATE), lambda i: (i, 0, 0))] + _const_specs(consts),
        out_specs=[pl.BlockSpec((1, t_new, D_MODEL), lambda i: (i, 0, 0)),
                   pl.BlockSpec((1, D_INNER, SSM_STATE), lambda i: (i, 0, 0)),
                   pl.BlockSpec((1, t, CONV_DIM), lambda i: (i, 0, 0))],
        out_shape=[jax.ShapeDtypeStruct((b, t_new, D_MODEL), F32),
                   jax.ShapeDtypeStruct((b, D_INNER, SSM_STATE), F32),
                   jax.ShapeDtypeStruct((b, t, CONV_DIM), F32)],
        scratch_shapes=[pltpu.VMEM((2 * SUBLANES, CONV_DIM), F32),
                        pltpu.VMEM((SSM_STATE, D_INNER), F32),
                        pltpu.VMEM((t, D_INNER), F32)],
        compiler_params=_params(1),
        name="ssm_sample",
    )(proj8, xres, cst8, ssm_state, *consts)
    return y, st, cv[:, t_new - (CONV_WIDTH - 1):t_new]


def _ffn_weights(w_in, w_out):
    def chunks(w):
        return w.reshape(D_MODEL, N_FF_CHUNKS, FF_CHUNK).transpose(1, 0, 2).astype(BF16)
    return (chunks(w_in[:, :D_FF]), chunks(w_in[:, D_FF:]),
            w_out.reshape(N_FF_CHUNKS, FF_CHUNK, D_MODEL).astype(BF16))


def _row_tile(rows):
    return 512 if rows % 512 == 0 else 128


def kernel(x_prompt, x_sample, cache_k, cache_v, page_table, state_ssm, state_conv, ffn1_norm, ffn1_w_in, ffn1_w_out, mix_norm, ffn2_norm, ffn2_w_in, ffn2_w_out, attn_w_qkv, attn_w_o, rel_bias, ssm_w_in, ssm_conv_w, ssm_conv_b, ssm_dt_bias, ssm_A_log, ssm_D, ssm_norm, ssm_w_out, final_norm):
    bp, s, _ = x_prompt.shape
    bs, t_new, _ = x_sample.shape
    assert bp == 1 and attn_w_qkv.shape[0] == 1 and ssm_w_in.shape[0] == 1
    xp = x_prompt.reshape(s, D_MODEL)
    xs = x_sample.reshape(bs * t_new, D_MODEL)
    tp, ts = _row_tile(s), _row_tile(bs * t_new)

    w1 = _ffn_weights(ffn1_w_in[0], ffn1_w_out[0])
    xp = _ffn(xp, ffn1_norm[0], w1, tp)
    xs = _ffn(xs, ffn1_norm[0], w1, ts)

    wqkv = attn_w_qkv[0].astype(BF16)
    wq, wk, wv = (wqkv[:, i * D_MODEL:(i + 1) * D_MODEL] for i in range(3))
    wo = attn_w_o[0].astype(BF16)
    k_p, v_p, kb, qt, vt, ksum = _qkv_prompt(xp, mix_norm[0], wq.T, wk, wv, wv.T, tp)
    o_p = _moba_prompt(qt, kb, vt, ksum.reshape(s // MOBA_BLOCK, D_MODEL), rel_bias)
    xp = _matres(o_p, wo, xp, tp)

    qkv_s = _proj(xs, mix_norm[0], wqkv, ts).reshape(bs, t_new, 3 * D_MODEL)
    q_s, k_s, v_s = (qkv_s[..., i * D_MODEL:(i + 1) * D_MODEL] for i in range(3))
    sel = _sample_gate(q_s, _page_sums(cache_k, page_table))
    o_s = _sample_attn(q_s, k_s, v_s, cache_k, cache_v, sel, page_table, rel_bias)
    xs = _matres(o_s.reshape(bs * t_new, D_MODEL), wo, xs, ts)

    w2 = _ffn_weights(ffn2_w_in[0], ffn2_w_out[0])
    xp = _ffn(xp, ffn2_norm[0], w2, tp)
    xs = _ffn(xs, ffn2_norm[0], w2, ts)

    w1 = _ffn_weights(ffn1_w_in[1], ffn1_w_out[1])
    xp = _ffn(xp, ffn1_norm[1], w1, tp)
    xs = _ffn(xs, ffn1_norm[1], w1, ts)

    w_in = jnp.pad(ssm_w_in[0], ((0, 0), (0, SSM_PROJ_PAD - SSM_IN_DIM))).astype(BF16)
    ssm_w = (ssm_conv_w[0], ssm_conv_b[0], ssm_dt_bias[0], ssm_A_log[0], ssm_D[0], ssm_norm[0],
             ssm_w_out[0].astype(BF16))
    xp, st_p, cv_p = _ssm_prompt(_proj(xp, mix_norm[1], w_in, tp), xp, ssm_w)
    proj_s = _proj(xs, mix_norm[1], w_in, ts).reshape(bs, t_new, SSM_PROJ_PAD)
    xs3, st_s, cv_s = _ssm_sample(proj_s, xs.reshape(bs, t_new, D_MODEL), state_conv[0],
                                  state_ssm[0].reshape(bs, D_INNER, SSM_STATE), ssm_w)
    xs = xs3.reshape(bs * t_new, D_MODEL)

    w2 = _ffn_weights(ffn2_w_in[1], ffn2_w_out[1])
    y_p = _ffn(xp, ffn2_norm[1], w2, tp, final_g=final_norm)
    y_s = _ffn(xs, ffn2_norm[1], w2, ts, final_g=final_norm)

    head_shape = (N_HEADS, HEAD_DIM)
    state_shape = (SSM_HEADS, SSM_HEAD_DIM, SSM_STATE)
    return (y_p.reshape(1, s, D_MODEL), y_s.reshape(bs, t_new, D_MODEL),
            k_p.reshape(1, 1, s // PAGE_SIZE, PAGE_SIZE, *head_shape),
            v_p.reshape(1, 1, s // PAGE_SIZE, PAGE_SIZE, *head_shape),
            k_s.reshape(1, bs, t_new, *head_shape), v_s.reshape(1, bs, t_new, *head_shape),
            st_p.reshape(1, 1, *state_shape), cv_p.reshape(1, 1, CONV_WIDTH - 1, CONV_DIM),
            st_s.reshape(1, bs, *state_shape), cv_s.reshape(1, bs, CONV_WIDTH - 1, CONV_DIM))
```

```python
import functools
import math

import jax
import jax.numpy as jnp
from jax import lax
from jax.experimental import pallas as pl
from jax.experimental.pallas import tpu as pltpu

F32 = jnp.float32
BF16 = jnp.bfloat16
I32 = jnp.int32
NEG_INF = float("-inf")
LOG2_E = math.log2(math.e)

D_MODEL = 1024
N_HEADS = 16
HEAD_DIM = 64
MOBA_BLOCK = 256
MOBA_TOPK = 3
PAGE_SIZE = 128
REL_BUCKETS = 32
REL_MAX_DIST = 128
D_INNER = 2048
SSM_HEADS = 32
SSM_HEAD_DIM = 64
SSM_GROUPS = 4
SSM_STATE = 128
CONV_WIDTH = 4
CONV_DIM = D_INNER + 2 * SSM_GROUPS * SSM_STATE
SSM_IN_DIM = 2 * D_INNER + 2 * SSM_GROUPS * SSM_STATE + SSM_HEADS
SSD_CHUNK = 256
D_FF = 2816
RMS_EPS = 1e-6

LANES = 128
SUBLANES = 8
VMEM_LIMIT_BYTES = 56 * 1024 * 1024

FF_CHUNK = 256
N_FF_CHUNKS = D_FF // FF_CHUNK
HEADS_PER_SLAB = LANES // HEAD_DIM
N_SLABS = N_HEADS // HEADS_PER_SLAB
GROUP_DIM = D_INNER // SSM_GROUPS
HEADS_PER_GROUP = SSM_HEADS // SSM_GROUPS
SSM_PROJ_PAD = 5248
DT_OFF = D_INNER + CONV_DIM
SAMPLE_T = 8
PAGES_PER_BLOCK = MOBA_BLOCK // PAGE_SIZE
KV_GROUP = 4
ATTN_HEADS = 4
ATTN_WIDTH = ATTN_HEADS * HEAD_DIM


def _params(n_axes, vmem=VMEM_LIMIT_BYTES):
    return pltpu.CompilerParams(dimension_semantics=("arbitrary",) * n_axes,
                                vmem_limit_bytes=vmem)


def _whole(shape):
    nd = len(shape)
    return pl.BlockSpec(shape, lambda *_: (0,) * nd)


def _smem():
    return pl.BlockSpec(memory_space=pltpu.SMEM)


def _rms(x, g):
    ms = jnp.mean(x * x, axis=-1, keepdims=True)
    return x * lax.rsqrt(ms + RMS_EPS) * g


def _silu(x):
    return x * jax.nn.sigmoid(x)


def _softplus(x):
    return jnp.maximum(x, 0.0) + jnp.log1p(jnp.exp(-jnp.abs(x)))


def _split3(x):
    hi = x.astype(BF16)
    r1 = x - hi.astype(F32)
    mid = r1.astype(BF16)
    lo = (r1 - mid.astype(F32)).astype(BF16)
    return hi, mid, lo


def _select_matmul(x, sel):
    hi, mid, lo = _split3(x)
    dot = functools.partial(jnp.dot, preferred_element_type=F32)
    return dot(hi, sel) + dot(mid, sel) + dot(lo, sel)


def _top3_rows(s, n_iota, sentinel):
    picks = []
    for _ in range(MOBA_TOPK):
        m = jnp.max(s, axis=0, keepdims=True)
        idx = jnp.min(jnp.where(s == m, n_iota, sentinel), axis=0, keepdims=True)
        picks.append((idx, m))
        s = jnp.where(n_iota == idx, NEG_INF, s)
    return picks


def _bias_from_buckets(bk, rb_ref, h):
    out = jnp.full(bk.shape, NEG_INF, F32)
    for b in range(REL_BUCKETS):
        out = jnp.where(bk == b, rb_ref[b, h], out)
    return out


def _rel_bucket(dist):
    n = jnp.maximum(dist, 0)
    max_exact = REL_BUCKETS // 2
    nf = jnp.maximum(n, 1).astype(F32)
    large = max_exact + (jnp.log(nf / max_exact) / math.log(REL_MAX_DIST / max_exact)
                         * (REL_BUCKETS - max_exact)).astype(I32)
    large = jnp.minimum(large, REL_BUCKETS - 1)
    return jnp.where(n < max_exact, n, large).astype(I32)


def _ffn_kernel(*refs, final):
    if final:
        x_ref, g_ref, wa_ref, wb_ref, wo_ref, fg_ref, o_ref, xn_ref, acc_ref = refs
    else:
        x_ref, g_ref, wa_ref, wb_ref, wo_ref, o_ref, xn_ref, acc_ref = refs
    xn_ref[...] = _rms(x_ref[...], g_ref[...]).astype(BF16)
    acc_ref[...] = jnp.zeros_like(acc_ref)

    def chunk(c, carry):
        xn = xn_ref[...]
        a = jnp.dot(xn, wa_ref[c], preferred_element_type=F32)
        b = jnp.dot(xn, wb_ref[c], preferred_element_type=F32)
        h = (_silu(a) * b).astype(BF16)
        acc_ref[...] += jnp.dot(h, wo_ref[c], preferred_element_type=F32)
        return carry

    lax.fori_loop(0, N_FF_CHUNKS, chunk, 0)
    y = x_ref[...] + 0.5 * acc_ref[...]
    if final:
        y = _rms(y, fg_ref[...])
    o_ref[...] = y


def _ffn(x, g, w, tm, final_g=None):
    rows = x.shape[0]
    wa, wb, wo = w
    row_spec = pl.BlockSpec((tm, D_MODEL), lambda i: (i, 0))
    in_specs = [row_spec, _whole((1, D_MODEL)), _whole(wa.shape), _whole(wb.shape), _whole(wo.shape)]
    args = [x, g.reshape(1, D_MODEL), wa, wb, wo]
    if final_g is not None:
        in_specs.append(_whole((1, D_MODEL)))
        args.append(final_g.reshape(1, D_MODEL))
    return pl.pallas_call(
        functools.partial(_ffn_kernel, final=final_g is not None),
        grid=(rows // tm,),
        in_specs=in_specs,
        out_specs=row_spec,
        out_shape=jax.ShapeDtypeStruct((rows, D_MODEL), F32),
        scratch_shapes=[pltpu.VMEM((tm, D_MODEL), BF16), pltpu.VMEM((tm, D_MODEL), F32)],
        compiler_params=_params(1),
        name="ffn",
    )(*args)


def _proj_kernel(x_ref, g_ref, w_ref, o_ref, xn_ref, *, widths):
    xn_ref[...] = _rms(x_ref[...], g_ref[...]).astype(BF16)
    lo = 0
    for wd in widths:
        o_ref[:, lo:lo + wd] = jnp.dot(xn_ref[...], w_ref[:, lo:lo + wd], preferred_element_type=F32)
        lo += wd


def _proj(x, g, w, tm):
    rows, n = x.shape[0], w.shape[1]
    col_chunk = 4 * LANES
    widths = [col_chunk] * (n // col_chunk) + ([n % col_chunk] if n % col_chunk else [])
    return pl.pallas_call(
        functools.partial(_proj_kernel, widths=tuple(widths)),
        grid=(rows // tm,),
        in_specs=[pl.BlockSpec((tm, D_MODEL), lambda i: (i, 0)), _whole((1, D_MODEL)), _whole(w.shape)],
        out_specs=pl.BlockSpec((tm, n), lambda i: (i, 0)),
        out_shape=jax.ShapeDtypeStruct((rows, n), F32),
        scratch_shapes=[pltpu.VMEM((tm, D_MODEL), BF16)],
        compiler_params=_params(1),
        name="norm_proj",
    )(x, g.reshape(1, D_MODEL), w)


def _matres_kernel(a_ref, w_ref, r_ref, o_ref):
    o_ref[...] = r_ref[...] + jnp.dot(a_ref[...].astype(BF16), w_ref[...], preferred_element_type=F32)


def _matres(a, w, res, tm):
    rows, k = a.shape
    n = w.shape[1]
    return pl.pallas_call(
        _matres_kernel,
        grid=(rows // tm,),
        in_specs=[pl.BlockSpec((tm, k), lambda i: (i, 0)), _whole(w.shape),
                  pl.BlockSpec((tm, n), lambda i: (i, 0))],
        out_specs=pl.BlockSpec((tm, n), lambda i: (i, 0)),
        out_shape=jax.ShapeDtypeStruct((rows, n), F32),
        compiler_params=_params(1),
        name="matmul_residual",
    )(a, w, res)


def _qkv_prompt_kernel(x_ref, g_ref, wqt_ref, wk_ref, wv_ref, wvt_ref,
                       k_ref, v_ref, kb_ref, qt_ref, vt_ref, ks_ref, xn_ref, *, tm):
    xn_ref[...] = _rms(x_ref[...], g_ref[...]).astype(BF16)
    xn = xn_ref[...]
    k = jnp.dot(xn, wk_ref[...], preferred_element_type=F32)
    k_ref[...] = k
    kb_ref[...] = k.astype(BF16)
    for j in range(tm // MOBA_BLOCK):
        ks_ref[0, j:j + 1, :] = jnp.sum(k[j * MOBA_BLOCK:(j + 1) * MOBA_BLOCK], axis=0, keepdims=True)
    v_ref[...] = jnp.dot(xn, wv_ref[...], preferred_element_type=F32)
    nt = (((1,), (1,)), ((), ()))
    qt = lax.dot_general(wqt_ref[...], xn, nt, preferred_element_type=F32)
    qt_ref[...] = (qt * (LOG2_E * HEAD_DIM ** -0.5)).astype(BF16)
    vt = lax.dot_general(wvt_ref[...], xn, nt, preferred_element_type=F32)
    for j in range(tm // MOBA_BLOCK):
        vt_ref[j] = vt[:, j * MOBA_BLOCK:(j + 1) * MOBA_BLOCK].astype(BF16)


def _qkv_prompt(x, g, wqt, wk, wv, wvt, tm):
    s = x.shape[0]
    nblk = s // MOBA_BLOCK
    bpt = tm // MOBA_BLOCK
    row_spec = pl.BlockSpec((tm, D_MODEL), lambda i: (i, 0))
    sq = _whole((D_MODEL, D_MODEL))
    return pl.pallas_call(
        functools.partial(_qkv_prompt_kernel, tm=tm),
        grid=(s // tm,),
        in_specs=[row_spec, _whole((1, D_MODEL)), sq, sq, sq, sq],
        out_specs=[row_spec, row_spec, row_spec,
                   pl.BlockSpec((D_MODEL, tm), lambda i: (0, i)),
                   pl.BlockSpec((bpt, D_MODEL, MOBA_BLOCK), lambda i: (i, 0, 0)),
                   pl.BlockSpec((1, bpt, D_MODEL), lambda i: (i, 0, 0))],
        out_shape=[jax.ShapeDtypeStruct((s, D_MODEL), F32),
                   jax.ShapeDtypeStruct((s, D_MODEL), F32),
                   jax.ShapeDtypeStruct((s, D_MODEL), BF16),
                   jax.ShapeDtypeStruct((D_MODEL, s), BF16),
                   jax.ShapeDtypeStruct((nblk, D_MODEL, MOBA_BLOCK), BF16),
                   jax.ShapeDtypeStruct((s // tm, bpt, D_MODEL), F32)],
        scratch_shapes=[pltpu.VMEM((tm, D_MODEL), BF16)],
        compiler_params=_params(1),
        name="qkv_prompt",
    )(x, g.reshape(1, D_MODEL), wqt, wk, wv, wvt)


def _softmax_group_step(tiles, v_tiles, state):
    m, l, acc = state
    m_new = m
    for s, keep, shift in tiles:
        bm = jnp.max(s, axis=0, keepdims=True) + shift
        m_new = jnp.maximum(m_new, bm if keep is None else jnp.where(keep > 0.0, bm, NEG_INF))
    alpha = jnp.exp2(m - m_new)
    l = alpha * l
    acc = alpha * acc
    for (s, keep, shift), v_t in zip(tiles, v_tiles):
        off = m_new - shift
        if keep is not None:
            off = jnp.where(keep > 0.0, off, jnp.inf)
        p = jnp.exp2(s - off)
        l = l + jnp.sum(p, axis=0, keepdims=True)
        acc = acc + jnp.dot(v_t, p.astype(BF16), preferred_element_type=F32)
    return m_new, l, acc


def _moba_prompt_kernel(rb_ref, qt_ref, kb_ref, vt_ref, ks_ref, bkt_ref, o_ref,
                        sel_ref, bias_ref, sa_ref, sb_ref, *, nblk):
    slab = pl.program_id(0)
    qi = pl.program_id(1)
    blk = MOBA_BLOCK
    heads = range(ATTN_HEADS)

    @pl.when(qi == 0)
    def _build_bias():
        for hh in heads:
            for kind in range(2):
                bias_ref[hh, kind] = LOG2_E * _bias_from_buckets(bkt_ref[kind], rb_ref,
                                                                 slab * ATTN_HEADS + hh)

    qt = qt_ref[...].astype(F32)
    row_head = lax.broadcasted_iota(I32, (ATTN_WIDTH, blk), 0) // HEAD_DIM
    n_iota = lax.broadcasted_iota(I32, (nblk, blk), 0)
    kmean = (ks_ref[...] * (1.0 / blk)).astype(BF16)
    far_bias = [LOG2_E * rb_ref[REL_BUCKETS - 1, slab * ATTN_HEADS + hh] for hh in heads]
    qz = [jnp.where(row_head == hh, qt, 0.0).astype(BF16) for hh in heads]
    for hh in heads:
        scores = jnp.dot(kmean, qz[hh], preferred_element_type=F32)
        scores = jnp.where(n_iota < qi, scores, NEG_INF)
        sel = jnp.zeros((nblk, blk), F32)
        for idx, m in _top3_rows(scores, n_iota, nblk):
            sel = jnp.where((n_iota == idx) & (m > NEG_INF), 1.0, sel)
        sel_ref[hh] = sel

    def fill(buf, base):
        for j in range(KV_GROUP):
            nc = jnp.maximum(base + j, 0)
            k_n = kb_ref[pl.ds(pl.multiple_of(nc * blk, blk), blk), :]
            for hh in heads:
                buf[hh * KV_GROUP + j] = jnp.dot(k_n, qz[hh], preferred_element_type=F32)

    def consume(buf, base, kinds, states):
        out = []
        for hh in heads:
            tiles, v_tiles = [], []
            for j, kind in enumerate(kinds):
                n = base + j
                nc = jnp.maximum(n, 0)
                s = buf[hh * KV_GROUP + j]
                v_tiles.append(vt_ref[nc, hh * HEAD_DIM:(hh + 1) * HEAD_DIM, :])
                if kind == "own":
                    tiles.append((s + bias_ref[hh, 0], None, 0.0))
                    continue
                keep = jnp.where(n >= 0, sel_ref[hh, pl.ds(nc, 1), :], 0.0)
                if kind == "prev":
                    tiles.append((s + bias_ref[hh, 1], keep, 0.0))
                else:
                    tiles.append((s, keep, far_bias[hh]))
            out.append(_softmax_group_step(tiles, v_tiles, states[hh]))
        return tuple(out)

    near_base = qi - (KV_GROUP - 1)
    near_kinds = ["far"] * (KV_GROUP - 2) + ["prev", "own"]
    far_kinds = ["far"] * KV_GROUP

    def far_base(g):
        return near_base - KV_GROUP * (g + 1)

    init = tuple((jnp.full((1, blk), NEG_INF, F32), jnp.zeros((1, blk), F32),
                  jnp.zeros((HEAD_DIM, blk), F32)) for _ in heads)
    fill(sa_ref, near_base)
    fill(sb_ref, far_base(0))
    states = consume(sa_ref, near_base, near_kinds, init)

    def far_pair(i, states):
        g = 2 * i
        fill(sa_ref, far_base(g + 1))
        states = consume(sb_ref, far_base(g), far_kinds, states)
        fill(sb_ref, far_base(g + 2))
        return consume(sa_ref, far_base(g + 1), far_kinds, states)

    n_far_groups = (jnp.maximum(near_base, 0) + KV_GROUP - 1) // KV_GROUP
    states = lax.fori_loop(0, (n_far_groups + 1) // 2, far_pair, states)
    outs = [acc / l for _, l, acc in states]
    o_ref[...] = jnp.concatenate(outs, axis=0).T.astype(BF16)


def _prompt_bucket_maps():
    kk = jnp.arange(MOBA_BLOCK, dtype=I32)[:, None]
    tt = jnp.arange(MOBA_BLOCK, dtype=I32)[None, :]
    own = jnp.where(tt - kk >= 0, _rel_bucket(tt - kk), -1)
    prev = _rel_bucket(MOBA_BLOCK + tt - kk)
    return jnp.stack([own, prev]).astype(I32)


def _moba_prompt(qt, kb, vt, ksum, rel_bias):
    s = kb.shape[0]
    nblk = s // MOBA_BLOCK
    return pl.pallas_call(
        functools.partial(_moba_prompt_kernel, nblk=nblk),
        grid=(N_HEADS // ATTN_HEADS, nblk),
        in_specs=[_smem(),
                  pl.BlockSpec((ATTN_WIDTH, MOBA_BLOCK), lambda sl, qi: (sl, qi)),
                  pl.BlockSpec((s, ATTN_WIDTH), lambda sl, qi: (0, sl), pipeline_mode=pl.Buffered(1)),
                  pl.BlockSpec((nblk, ATTN_WIDTH, MOBA_BLOCK), lambda sl, qi: (0, sl, 0),
                               pipeline_mode=pl.Buffered(1)),
                  pl.BlockSpec((nblk, ATTN_WIDTH), lambda sl, qi: (0, sl)),
                  _whole((2, MOBA_BLOCK, MOBA_BLOCK))],
        out_specs=pl.BlockSpec((MOBA_BLOCK, ATTN_WIDTH), lambda sl, qi: (qi, sl)),
        out_shape=jax.ShapeDtypeStruct((s, D_MODEL), BF16),
        scratch_shapes=[pltpu.VMEM((ATTN_HEADS, nblk, MOBA_BLOCK), F32),
                        pltpu.VMEM((ATTN_HEADS, 2, MOBA_BLOCK, MOBA_BLOCK), F32),
                        pltpu.VMEM((ATTN_HEADS * KV_GROUP, MOBA_BLOCK, MOBA_BLOCK), F32),
                        pltpu.VMEM((ATTN_HEADS * KV_GROUP, MOBA_BLOCK, MOBA_BLOCK), F32)],
        compiler_params=_params(2),
        name="moba_prompt",
    )(rel_bias, qt, kb, vt, ksum, _prompt_bucket_maps())


def _page_sum_kernel(pt_ref, *refs, pps):
    del pt_ref
    pages, o_ref = refs[:pps], refs[pps]
    for i in range(pps // PAGES_PER_BLOCK):
        tot = pages[PAGES_PER_BLOCK * i][0]
        for j in range(1, PAGES_PER_BLOCK):
            tot = tot + pages[PAGES_PER_BLOCK * i + j][0]
        o_ref[0, i] = jnp.sum(tot, axis=-1)


def _page_sums(cache, page_table):
    b, npg = page_table.shape
    pps = min(16, npg)
    bps = pps // PAGES_PER_BLOCK

    def page_spec(i):
        return pl.BlockSpec((1, N_HEADS, HEAD_DIM, PAGE_SIZE),
                            lambda bi, j, pt: (pt[bi * npg + j * pps + i], 0, 0, 0))

    sums = pl.pallas_call(
        functools.partial(_page_sum_kernel, pps=pps),
        grid_spec=pltpu.PrefetchScalarGridSpec(
            num_scalar_prefetch=1,
            grid=(b, npg // pps),
            in_specs=[page_spec(i) for i in range(pps)],
            out_specs=pl.BlockSpec((1, bps, N_HEADS, HEAD_DIM),
                                   lambda bi, j, pt: (bi * (npg // pps) + j, 0, 0, 0)),
        ),
        out_shape=jax.ShapeDtypeStruct((b * (npg // pps), bps, N_HEADS, HEAD_DIM), F32),
        compiler_params=_params(2),
        name="page_sums",
    )(page_table.reshape(-1), *([cache] * pps))
    return sums.reshape(b, npg // PAGES_PER_BLOCK, D_MODEL)


def _sample_gate_kernel(q_ref, bs_ref, seg_ref, sel_ref, *, t_new, nblk):
    kmean = bs_ref[0] * (1.0 / MOBA_BLOCK)
    n_iota = lax.broadcasted_iota(I32, (nblk, LANES), 0)
    sel_ref[...] = jnp.zeros_like(sel_ref)
    for t in range(t_new):
        prod = kmean * q_ref[0, t:t + 1, :]
        scores = _select_matmul(prod, seg_ref[...])
        for r, (idx, _) in enumerate(_top3_rows(scores, n_iota, nblk)):
            sel_ref[0, t, r:r + 1, :] = idx


def _sample_gate(q, bsum):
    b, t_new, _ = q.shape
    nblk = bsum.shape[1]
    seg = (jnp.arange(D_MODEL, dtype=I32)[:, None] // HEAD_DIM
           == jnp.arange(LANES, dtype=I32)[None, :]).astype(BF16)
    sel = pl.pallas_call(
        functools.partial(_sample_gate_kernel, t_new=t_new, nblk=nblk),
        grid=(b,),
        in_specs=[pl.BlockSpec((1, t_new, D_MODEL), lambda i: (i, 0, 0)),
                  pl.BlockSpec((1, nblk, D_MODEL), lambda i: (i, 0, 0)),
                  _whole((D_MODEL, LANES))],
        out_specs=pl.BlockSpec((1, t_new, SUBLANES, LANES), lambda i: (i, 0, 0, 0)),
        out_shape=jax.ShapeDtypeStruct((b, t_new, SUBLANES, LANES), I32),
        compiler_params=_params(1),
        name="sample_gate",
    )(q, bsum, seg)
    return sel[:, :, :MOBA_TOPK, :N_HEADS]


SLABS_PER_HEAD = MOBA_TOPK * PAGES_PER_BLOCK


def _sample_attn_kernel(sel_ref, pt_ref, rb_ref, q_ref, kn_ref, vn_ref, pb_ref, ck_hbm, cv_hbm, o_ref,
                        kbuf, vbuf, sems, o_scr, *, t_new, npg):
    bi, t = pl.program_id(0), pl.program_id(1)
    step = bi * t_new + t
    n_steps = pl.num_programs(0) * t_new
    slot = lax.rem(step, 2)
    last_blk = npg // PAGES_PER_BLOCK - 1

    def selected_block(b_, t_, j, h):
        return sel_ref[((b_ * t_new + t_) * MOBA_TOPK + j) * N_HEADS + h]

    def slab_copies(b_, t_, slot_):
        copies = []
        for h in range(N_HEADS):
            for j in range(MOBA_TOPK):
                blk = selected_block(b_, t_, j, h)
                for pg in range(PAGES_PER_BLOCK):
                    page = pt_ref[b_ * npg + blk * PAGES_PER_BLOCK + pg]
                    i = j * PAGES_PER_BLOCK + pg
                    copies.append(pltpu.make_async_copy(ck_hbm.at[page, h], kbuf.at[slot_, h, i],
                                                        sems.at[0, slot_]))
                    copies.append(pltpu.make_async_copy(cv_hbm.at[page, h], vbuf.at[slot_, h, i],
                                                        sems.at[1, slot_]))
        return copies

    @pl.when(step == 0)
    def _first_fetch():
        for c in slab_copies(bi, t, slot):
            c.start()

    @pl.when(step + 1 < n_steps)
    def _prefetch_next():
        nxt = step + 1
        for c in slab_copies(lax.div(nxt, t_new), lax.rem(nxt, t_new), 1 - slot):
            c.start()

    for c in slab_copies(bi, t, slot):
        c.wait()

    row = lax.broadcasted_iota(I32, (SUBLANES, 1), 0)
    q = q_ref[0, 0] * (HEAD_DIM ** -0.5)
    for h in range(N_HEADS):
        cols = slice(h * HEAD_DIM, (h + 1) * HEAD_DIM)
        far_bias = rb_ref[REL_BUCKETS - 1, h]
        qh = q[:, cols]
        q8 = jnp.broadcast_to(qh, (SUBLANES, HEAD_DIM)).astype(BF16)
        l1 = []
        for j in range(MOBA_TOPK):
            blk_j = selected_block(bi, t, j, h)
            for pg in range(PAGES_PER_BLOCK):
                k_t = kbuf[slot, h, j * PAGES_PER_BLOCK + pg].astype(BF16)
                near = _bias_from_buckets(pb_ref[pl.ds(t * PAGES_PER_BLOCK + pg, 1), :], rb_ref, h)
                l1.append(jnp.dot(q8, k_t, preferred_element_type=F32)[0:1]
                          + jnp.where(blk_j == last_blk, near, far_bias))
        l1 = jnp.concatenate(l1, axis=1)
        l2 = jnp.sum(kn_ref[0, :, cols] * qh, axis=-1, keepdims=True)
        bias2 = jnp.zeros((SUBLANES, 1), F32)
        for u in range(t_new):
            bias2 = jnp.where(row == u, rb_ref[jnp.maximum(t - u, 0), h], bias2)
        l2 = jnp.where(row <= t, l2 + bias2, NEG_INF)
        m = jnp.maximum(jnp.max(l1, axis=-1, keepdims=True), jnp.max(l2, axis=0, keepdims=True))
        p1 = jnp.exp(l1 - m)
        p2 = jnp.exp(l2 - m)
        denom = jnp.sum(p1, axis=-1, keepdims=True) + jnp.sum(p2, axis=0, keepdims=True)
        p8 = jnp.broadcast_to(p1, (SUBLANES, SLABS_PER_HEAD * PAGE_SIZE)).astype(BF16)
        o = jnp.sum(p2 * vn_ref[0, :, cols], axis=0, keepdims=True)
        for i in range(SLABS_PER_HEAD):
            v_t = vbuf[slot, h, i].astype(BF16)
            o = o + lax.dot_general(p8[:, i * PAGE_SIZE:(i + 1) * PAGE_SIZE], v_t, (((1,), (1,)), ((), ())),
                                    preferred_element_type=F32)[0:1]
        o_scr[:, cols] = jnp.broadcast_to(o / denom, (SUBLANES, HEAD_DIM))
    o_ref[0, 0] = o_scr[0:1, :]


def _sample_attn(q, k_new, v_new, cache_k, cache_v, sel, page_table, rel_bias):
    b, t_new, _ = q.shape
    npg = page_table.shape[1]
    pad = ((0, 0), (0, SUBLANES - t_new), (0, 0))
    tt = jnp.arange(t_new, dtype=I32)[:, None, None]
    pg = jnp.arange(PAGES_PER_BLOCK, dtype=I32)[None, :, None]
    rr = jnp.arange(PAGE_SIZE, dtype=I32)[None, None, :]
    near_bkt = _rel_bucket(MOBA_BLOCK + tt - pg * PAGE_SIZE - rr).reshape(t_new * PAGES_PER_BLOCK, PAGE_SIZE)

    new_spec = pl.BlockSpec((1, SUBLANES, D_MODEL), lambda bi, t, *_: (bi, 0, 0))
    tok_spec = pl.BlockSpec((1, 1, 1, D_MODEL), lambda bi, t, *_: (bi, t, 0, 0))
    slab_buf = pltpu.VMEM((2, N_HEADS, SLABS_PER_HEAD, HEAD_DIM, PAGE_SIZE), F32)
    out = pl.pallas_call(
        functools.partial(_sample_attn_kernel, t_new=t_new, npg=npg),
        grid_spec=pltpu.PrefetchScalarGridSpec(
            num_scalar_prefetch=2,
            grid=(b, t_new),
            in_specs=[_smem(), tok_spec, new_spec, new_spec,
                      pl.BlockSpec((t_new * PAGES_PER_BLOCK, PAGE_SIZE), lambda *_: (0, 0)),
                      pl.BlockSpec(memory_space=pl.ANY), pl.BlockSpec(memory_space=pl.ANY)],
            out_specs=tok_spec,
            scratch_shapes=[slab_buf, slab_buf, pltpu.SemaphoreType.DMA((2, 2)),
                            pltpu.VMEM((SUBLANES, D_MODEL), F32)],
        ),
        out_shape=jax.ShapeDtypeStruct((b, t_new, 1, D_MODEL), F32),
        compiler_params=_params(2),
        name="sample_attn",
    )(sel.reshape(-1), page_table.reshape(-1), rel_bias,
      q.reshape(b, t_new, 1, D_MODEL), jnp.pad(k_new, pad), jnp.pad(v_new, pad), near_bkt,
      cache_k, cache_v)
    return out.reshape(b, t_new, D_MODEL)


def _ssd_chunk(t, ext_ref, p_ref, st_ref, gy_ref, cw_ref, cb_ref, dtb_ref, alog_ref, dsk_ref,
               nw_ref, e_ref, wout_ref, n_valid):
    def conv_act(lo, width):
        acc = cb_ref[:, lo:lo + width]
        for j in range(CONV_WIDTH):
            acc = acc + cw_ref[j:j + 1, lo:lo + width] * ext_ref[5 + j:5 + j + t, lo:lo + width]
        return _silu(acc)

    dt = _softplus(p_ref[:, DT_OFF:DT_OFF + LANES] + dtb_ref[...])
    if n_valid < t:
        dt = jnp.where(lax.broadcasted_iota(I32, (t, LANES), 0) < n_valid, dt, 0.0)
    a = dt * -jnp.exp(alog_ref[...])
    r_i = lax.broadcasted_iota(I32, (t, t), 0)
    c_i = lax.broadcasted_iota(I32, (t, t), 1)
    causal = r_i >= c_i
    acs = jnp.dot(causal.astype(F32), a, preferred_element_type=F32,
                  precision=lax.Precision.HIGHEST)
    if t % LANES == 0:
        acs_t = acs.T
    else:
        acs_t = lax.dot_general(jnp.eye(LANES, dtype=F32), acs, (((1,), (1,)), ((), ())),
                                preferred_element_type=F32, precision=lax.Precision.HIGHEST)
    lane_hi = lax.broadcasted_iota(I32, (t, LANES), 1) >= SSM_HEAD_DIM

    for g in range(SSM_GROUPS):
        ch = slice(g * GROUP_DIM, (g + 1) * GROUP_DIM)
        e_g = e_ref[:, ch]
        xs = conv_act(g * GROUP_DIM, GROUP_DIM)
        b_g = conv_act(D_INNER + g * SSM_STATE, SSM_STATE).astype(BF16)
        c_g = conv_act(D_INNER + (SSM_GROUPS + g) * SSM_STATE, SSM_STATE).astype(BF16)
        dt_g = _select_matmul(dt, e_g)
        acs_g = _select_matmul(acs, e_g)
        last_g = acs_g[t - 1:t, :]
        xdt = xs * dt_g
        xdt_b = xdt.astype(BF16)
        xdt_dec = (xdt * jnp.exp(last_g - acs_g)).astype(BF16)
        cb = lax.dot_general(c_g, b_g, (((1,), (1,)), ((), ())), preferred_element_type=F32)
        st_in = st_ref[:, ch]
        y_off = jnp.dot(c_g, st_in.astype(BF16), preferred_element_type=F32) * jnp.exp(acs_g)
        st_new = lax.dot_general(b_g, xdt_dec, (((0,), (0,)), ((), ())), preferred_element_type=F32)
        st_ref[:, ch] = st_in * jnp.exp(last_g) + st_new
        for j in range(HEADS_PER_GROUP // HEADS_PER_SLAB):
            cols = slice(j * LANES, (j + 1) * LANES)
            pair = xdt_b[:, cols]
            ys = []
            for hh in range(HEADS_PER_SLAB):
                h = g * HEADS_PER_GROUP + j * HEADS_PER_SLAB + hh
                decay = jnp.exp(jnp.where(causal, acs[:, h:h + 1] - acs_t[h:h + 1, :], NEG_INF))
                ys.append(jnp.dot((cb * decay).astype(BF16), pair, preferred_element_type=F32))
            y_diag = jnp.where(lane_hi, ys[1], ys[0])
            lo = g * GROUP_DIM + j * LANES
            y = y_diag + y_off[:, cols] + dsk_ref[:, lo:lo + LANES] * xs[:, cols]
            gy_ref[:, lo:lo + LANES] = y * _silu(p_ref[:, lo:lo + LANES])
        gy = gy_ref[:, ch]
        ms = jnp.mean(gy * gy, axis=-1, keepdims=True)
        gy_ref[:, ch] = gy * lax.rsqrt(ms + RMS_EPS) * nw_ref[:, ch]
    return jnp.dot(gy_ref[...].astype(BF16), wout_ref[...], preferred_element_type=F32)


def _ssm_prompt_kernel(p_ref, xres_ref, cw_ref, cb_ref, dtb_ref, alog_ref, dsk_ref, nw_ref, e_ref,
                       wout_ref, y_ref, st_out_ref, cv_out_ref, ext_ref, st_ref, gy_ref):
    c = pl.program_id(0)
    t = SSD_CHUNK

    @pl.when(c == 0)
    def _init():
        ext_ref[t:t + SUBLANES, :] = jnp.zeros((SUBLANES, CONV_DIM), F32)
        st_ref[...] = jnp.zeros_like(st_ref)

    ext_ref[0:SUBLANES, :] = ext_ref[t:t + SUBLANES, :]
    ext_ref[SUBLANES:SUBLANES + t, :] = p_ref[:, D_INNER:D_INNER + CONV_DIM]
    out = _ssd_chunk(t, ext_ref, p_ref, st_ref, gy_ref, cw_ref, cb_ref, dtb_ref, alog_ref, dsk_ref,
                     nw_ref, e_ref, wout_ref, t)
    y_ref[...] = xres_ref[...] + out

    @pl.when(c == pl.num_programs(0) - 1)
    def _finish():
        st_out_ref[...] = st_ref[...].T
        cv_out_ref[...] = ext_ref[t:t + SUBLANES, :]


def _ssm_consts(w):
    conv_w, conv_b, dt_bias, a_log, d_skip, norm_w, w_out = w
    lane_pad = (0, LANES - SSM_HEADS)
    expand = (jnp.arange(LANES, dtype=I32)[:, None]
              == jnp.arange(D_INNER, dtype=I32)[None, :] // SSM_HEAD_DIM).astype(BF16)
    return (conv_w, conv_b.reshape(1, CONV_DIM),
            jnp.pad(dt_bias, lane_pad).reshape(1, LANES), jnp.pad(a_log, lane_pad).reshape(1, LANES),
            jnp.repeat(d_skip, SSM_HEAD_DIM).reshape(1, D_INNER), norm_w.reshape(1, D_INNER),
            expand, w_out)


def _const_specs(consts):
    return [_whole(c.shape) for c in consts]


def _ssm_prompt(proj, xres, w):
    s = proj.shape[0]
    t = SSD_CHUNK
    consts = _ssm_consts(w)
    y, st, cv = pl.pallas_call(
        _ssm_prompt_kernel,
        grid=(s // t,),
        in_specs=[pl.BlockSpec((t, SSM_PROJ_PAD), lambda c: (c, 0)),
                  pl.BlockSpec((t, D_MODEL), lambda c: (c, 0))] + _const_specs(consts),
        out_specs=[pl.BlockSpec((t, D_MODEL), lambda c: (c, 0)),
                   _whole((D_INNER, SSM_STATE)), _whole((SUBLANES, CONV_DIM))],
        out_shape=[jax.ShapeDtypeStruct((s, D_MODEL), F32),
                   jax.ShapeDtypeStruct((D_INNER, SSM_STATE), F32),
                   jax.ShapeDtypeStruct((SUBLANES, CONV_DIM), F32)],
        scratch_shapes=[pltpu.VMEM((t + 2 * SUBLANES, CONV_DIM), F32),
                        pltpu.VMEM((SSM_STATE, D_INNER), F32),
                        pltpu.VMEM((t, D_INNER), F32)],
        compiler_params=_params(1),
        name="ssm_prompt",
    )(proj, xres, *consts)
    return y, st, cv[SUBLANES - (CONV_WIDTH - 1):]


def _ssm_sample_kernel(p_ref, xres_ref, cst_ref, st_in_ref, cw_ref, cb_ref, dtb_ref, alog_ref,
                       dsk_ref, nw_ref, e_ref, wout_ref, y_ref, st_out_ref, cv_out_ref,
                       ext_ref, st_ref, gy_ref, *, t_new):
    t = SAMPLE_T
    ext_ref[0:SUBLANES, :] = cst_ref[0]
    ext_ref[SUBLANES:SUBLANES + t, :] = p_ref[0, :, D_INNER:D_INNER + CONV_DIM]
    st_ref[...] = st_in_ref[0].T
    out = _ssd_chunk(t, ext_ref, p_ref.at[0], st_ref, gy_ref, cw_ref, cb_ref, dtb_ref, alog_ref,
                     dsk_ref, nw_ref, e_ref, wout_ref, t_new)
    y_ref[0] = xres_ref[0] + out[0:t_new]
    st_out_ref[0] = st_ref[...].T
    cv_out_ref[0] = ext_ref[SUBLANES:SUBLANES + t, :]


def _ssm_sample(proj, xres, conv_state, ssm_state, w):
    b, t_new, _ = proj.shape
    t = SAMPLE_T
    consts = _ssm_consts(w)
    proj8 = jnp.pad(proj, ((0, 0), (0, t - t_new), (0, 0)))
    cst8 = jnp.pad(conv_state, ((0, 0), (SUBLANES - (CONV_WIDTH - 1), 0), (0, 0)))
    y, st, cv = pl.pallas_call(
        functools.partial(_ssm_sample_kernel, t_new=t_new),
        grid=(b,),
        in_specs=[pl.BlockSpec((1, t, SSM_PROJ_PAD), lambda i: (i, 0, 0)),
                  pl.BlockSpec((1, t_new, D_MODEL), lambda i: (i, 0, 0)),
                  pl.BlockSpec((1, SUBLANES, CONV_DIM), lambda i: (i, 0, 0)),
                  pl.BlockSpec((1, D_INNER, SSM_STATE), lambda i: (i, 0, 0))] + _const_specs(consts),
        out_specs=[pl.BlockSpec((1, t_new, D_MODEL), lambda i: (i, 0, 0)),
                   pl.BlockSpec((1, D_INNER, SSM_STATE), lambda i: (i, 0, 0)),
                   pl.BlockSpec((1, t, CONV_DIM), lambda i: (i, 0, 0))],
        out_shape=[jax.ShapeDtypeStruct((b, t_new, D_MODEL), F32),
                   jax.ShapeDtypeStruct((b, D_INNER, SSM_STATE), F32),
                   jax.ShapeDtypeStruct((b, t, CONV_DIM), F32)],
        scratch_shapes=[pltpu.VMEM((2 * SUBLANES, CONV_DIM), F32),
                        pltpu.VMEM((SSM_STATE, D_INNER), F32),
                        pltpu.VMEM((t, D_INNER), F32)],
        compiler_params=_params(1),
        name="ssm_sample",
    )(proj8, xres, cst8, ssm_state, *consts)
    return y, st, cv[:, t_new - (CONV_WIDTH - 1):t_new]


def _ffn_weights(w_in, w_out):
    def chunks(w):
        return w.reshape(D_MODEL, N_FF_CHUNKS, FF_CHUNK).transpose(1, 0, 2).astype(BF16)
    return (chunks(w_in[:, :D_FF]), chunks(w_in[:, D_FF:]),
            w_out.reshape(N_FF_CHUNKS, FF_CHUNK, D_MODEL).astype(BF16))


def _row_tile(rows):
    return 512 if rows % 512 == 0 else 128


def kernel(x_prompt, x_sample, cache_k, cache_v, page_table, state_ssm, state_conv, ffn1_norm, ffn1_w_in, ffn1_w_out, mix_norm, ffn2_norm, ffn2_w_in, ffn2_w_out, attn_w_qkv, attn_w_o, rel_bias, ssm_w_in, ssm_conv_w, ssm_conv_b, ssm_dt_bias, ssm_A_log, ssm_D, ssm_norm, ssm_w_out, final_norm):
    bp, s, _ = x_prompt.shape
    bs, t_new, _ = x_sample.shape
    assert bp == 1 and attn_w_qkv.shape[0] == 1 and ssm_w_in.shape[0] == 1
    xp = x_prompt.reshape(s, D_MODEL)
    xs = x_sample.reshape(bs * t_new, D_MODEL)
    tp, ts = _row_tile(s), _row_tile(bs * t_new)

    w1 = _ffn_weights(ffn1_w_in[0], ffn1_w_out[0])
    xp = _ffn(xp, ffn1_norm[0], w1, tp)
    xs = _ffn(xs, ffn1_norm[0], w1, ts)

    wqkv = attn_w_qkv[0].astype(BF16)
    wq, wk, wv = (wqkv[:, i * D_MODEL:(i + 1) * D_MODEL] for i in range(3))
    wo = attn_w_o[0].astype(BF16)
    k_p, v_p, kb, qt, vt, ksum = _qkv_prompt(xp, mix_norm[0], wq.T, wk, wv, wv.T, tp)
    o_p = _moba_prompt(qt, kb, vt, ksum.reshape(s // MOBA_BLOCK, D_MODEL), rel_bias)
    xp = _matres(o_p, wo, xp, tp)

    qkv_s = _proj(xs, mix_norm[0], wqkv, ts).reshape(bs, t_new, 3 * D_MODEL)
    q_s, k_s, v_s = (qkv_s[..., i * D_MODEL:(i + 1) * D_MODEL] for i in range(3))
    ck = jnp.transpose(cache_k[0], (0, 2, 3, 1))
    cv = jnp.transpose(cache_v[0], (0, 2, 3, 1))
    sel = _sample_gate(q_s, _page_sums(ck, page_table))
    o_s = _sample_attn(q_s, k_s, v_s, ck, cv, sel, page_table, rel_bias)
    xs = _matres(o_s.reshape(bs * t_new, D_MODEL), wo, xs, ts)

    w2 = _ffn_weights(ffn2_w_in[0], ffn2_w_out[0])
    xp = _ffn(xp, ffn2_norm[0], w2, tp)
    xs = _ffn(xs, ffn2_norm[0], w2, ts)

    w1 = _ffn_weights(ffn1_w_in[1], ffn1_w_out[1])
    xp = _ffn(xp, ffn1_norm[1], w1, tp)
    xs = _ffn(xs, ffn1_norm[1], w1, ts)

    w_in = jnp.pad(ssm_w_in[0], ((0, 0), (0, SSM_PROJ_PAD - SSM_IN_DIM))).astype(BF16)
    ssm_w = (ssm_conv_w[0], ssm_conv_b[0], ssm_dt_bias[0], ssm_A_log[0], ssm_D[0], ssm_norm[0],
             ssm_w_out[0].astype(BF16))
    xp, st_p, cv_p = _ssm_prompt(_proj(xp, mix_norm[1], w_in, tp), xp, ssm_w)
    proj_s = _proj(xs, mix_norm[1], w_in, ts).reshape(bs, t_new, SSM_PROJ_PAD)
    xs3, st_s, cv_s = _ssm_sample(proj_s, xs.reshape(bs, t_new, D_MODEL), state_conv[0],
                                  state_ssm[0].reshape(bs, D_INNER, SSM_STATE), ssm_w)
    xs = xs3.reshape(bs * t_new, D_MODEL)

    w2 = _ffn_weights(ffn2_w_in[1], ffn2_w_out[1])
    y_p = _ffn(xp, ffn2_norm[1], w2, tp, final_g=final_norm)
    y_s = _ffn(xs, ffn2_norm[1], w2, ts, final_g=final_norm)

    head_shape = (N_HEADS, HEAD_DIM)
    state_shape = (SSM_HEADS, SSM_HEAD_DIM, SSM_STATE)
    return (y_p.reshape(1, s, D_MODEL), y_s.reshape(bs, t_new, D_MODEL),
            k_p.reshape(1, 1, s // PAGE_SIZE, PAGE_SIZE, *head_shape),
            v_p.reshape(1, 1, s // PAGE_SIZE, PAGE_SIZE, *head_shape),
            k_s.reshape(1, bs, t_new, *head_shape), v_s.reshape(1, bs, t_new, *head_shape),
            st_p.reshape(1, 1, *state_shape), cv_p.reshape(1, 1, CONV_WIDTH - 1, CONV_DIM),
            st_s.reshape(1, bs, *state_shape), cv_s.reshape(1, bs, CONV_WIDTH - 1, CONV_DIM))
```

```python
import functools
import math

import jax
import jax.numpy as jnp
from jax import lax
from jax.experimental import pallas as pl
from jax.experimental.pallas import tpu as pltpu

F32 = jnp.float32
BF16 = jnp.bfloat16
I32 = jnp.int32
NEG_INF = float("-inf")
LOG2_E = math.log2(math.e)

D_MODEL = 1024
N_HEADS = 16
HEAD_DIM = 64
MOBA_BLOCK = 256
MOBA_TOPK = 3
PAGE_SIZE = 128
REL_BUCKETS = 32
REL_MAX_DIST = 128
D_INNER = 2048
SSM_HEADS = 32
SSM_HEAD_DIM = 64
SSM_GROUPS = 4
SSM_STATE = 128
CONV_WIDTH = 4
CONV_DIM = D_INNER + 2 * SSM_GROUPS * SSM_STATE
SSM_IN_DIM = 2 * D_INNER + 2 * SSM_GROUPS * SSM_STATE + SSM_HEADS
SSD_CHUNK = 256
D_FF = 2816
RMS_EPS = 1e-6

LANES = 128
SUBLANES = 8
VMEM_LIMIT_BYTES = 56 * 1024 * 1024

FF_CHUNK = 256
N_FF_CHUNKS = D_FF // FF_CHUNK
HEADS_PER_SLAB = LANES // HEAD_DIM
N_SLABS = N_HEADS // HEADS_PER_SLAB
GROUP_DIM = D_INNER // SSM_GROUPS
HEADS_PER_GROUP = SSM_HEADS // SSM_GROUPS
SSM_PROJ_PAD = 5248
DT_OFF = D_INNER + CONV_DIM
SAMPLE_T = 8
PAGES_PER_BLOCK = MOBA_BLOCK // PAGE_SIZE
KV_GROUP = 4
ATTN_HEADS = 4
ATTN_WIDTH = ATTN_HEADS * HEAD_DIM
SUM_ROWS = 16


def _params(n_axes, vmem=VMEM_LIMIT_BYTES):
    return pltpu.CompilerParams(dimension_semantics=("arbitrary",) * n_axes,
                                vmem_limit_bytes=vmem)


def _whole(shape):
    nd = len(shape)
    return pl.BlockSpec(shape, lambda *_: (0,) * nd)


def _smem():
    return pl.BlockSpec(memory_space=pltpu.SMEM)


def _rms(x, g):
    ms = jnp.mean(x * x, axis=-1, keepdims=True)
    return x * lax.rsqrt(ms + RMS_EPS) * g


def _silu(x):
    return x * jax.nn.sigmoid(x)


def _softplus(x):
    return jnp.maximum(x, 0.0) + jnp.log1p(jnp.exp(-jnp.abs(x)))


def _split3(x):
    hi = x.astype(BF16)
    r1 = x - hi.astype(F32)
    mid = r1.astype(BF16)
    lo = (r1 - mid.astype(F32)).astype(BF16)
    return hi, mid, lo


def _select_matmul(x, sel):
    hi, mid, lo = _split3(x)
    dot = functools.partial(jnp.dot, preferred_element_type=F32)
    return dot(hi, sel) + dot(mid, sel) + dot(lo, sel)


def _top3_rows(s, n_iota, sentinel):
    picks = []
    for _ in range(MOBA_TOPK):
        m = jnp.max(s, axis=0, keepdims=True)
        idx = jnp.min(jnp.where(s == m, n_iota, sentinel), axis=0, keepdims=True)
        picks.append((idx, m))
        s = jnp.where(n_iota == idx, NEG_INF, s)
    return picks


def _bias_from_buckets(bk, rb_ref, h):
    out = jnp.full(bk.shape, NEG_INF, F32)
    for b in range(REL_BUCKETS):
        out = jnp.where(bk == b, rb_ref[b, h], out)
    return out


def _rel_bucket(dist):
    n = jnp.maximum(dist, 0)
    max_exact = REL_BUCKETS // 2
    nf = jnp.maximum(n, 1).astype(F32)
    large = max_exact + (jnp.log(nf / max_exact) / math.log(REL_MAX_DIST / max_exact)
                         * (REL_BUCKETS - max_exact)).astype(I32)
    large = jnp.minimum(large, REL_BUCKETS - 1)
    return jnp.where(n < max_exact, n, large).astype(I32)


def _ffn_kernel(*refs, final):
    if final:
        x_ref, g_ref, wa_ref, wb_ref, wo_ref, fg_ref, o_ref, xn_ref, acc_ref = refs
    else:
        x_ref, g_ref, wa_ref, wb_ref, wo_ref, o_ref, xn_ref, acc_ref = refs
    xn_ref[...] = _rms(x_ref[...], g_ref[...]).astype(BF16)
    acc_ref[...] = jnp.zeros_like(acc_ref)

    def chunk(c, carry):
        xn = xn_ref[...]
        a = jnp.dot(xn, wa_ref[c], preferred_element_type=F32)
        b = jnp.dot(xn, wb_ref[c], preferred_element_type=F32)
        h = (_silu(a) * b).astype(BF16)
        acc_ref[...] += jnp.dot(h, wo_ref[c], preferred_element_type=F32)
        return carry

    lax.fori_loop(0, N_FF_CHUNKS, chunk, 0)
    y = x_ref[...] + 0.5 * acc_ref[...]
    if final:
        y = _rms(y, fg_ref[...])
    o_ref[...] = y


def _ffn(x, g, w, tm, final_g=None):
    rows = x.shape[0]
    wa, wb, wo = w
    row_spec = pl.BlockSpec((tm, D_MODEL), lambda i: (i, 0))
    in_specs = [row_spec, _whole((1, D_MODEL)), _whole(wa.shape), _whole(wb.shape), _whole(wo.shape)]
    args = [x, g.reshape(1, D_MODEL), wa, wb, wo]
    if final_g is not None:
        in_specs.append(_whole((1, D_MODEL)))
        args.append(final_g.reshape(1, D_MODEL))
    return pl.pallas_call(
        functools.partial(_ffn_kernel, final=final_g is not None),
        grid=(rows // tm,),
        in_specs=in_specs,
        out_specs=row_spec,
        out_shape=jax.ShapeDtypeStruct((rows, D_MODEL), F32),
        scratch_shapes=[pltpu.VMEM((tm, D_MODEL), BF16), pltpu.VMEM((tm, D_MODEL), F32)],
        compiler_params=_params(1),
        name="ffn",
    )(*args)


def _proj_kernel(x_ref, g_ref, w_ref, o_ref, xn_ref, *, widths):
    xn_ref[...] = _rms(x_ref[...], g_ref[...]).astype(BF16)
    lo = 0
    for wd in widths:
        o_ref[:, lo:lo + wd] = jnp.dot(xn_ref[...], w_ref[:, lo:lo + wd], preferred_element_type=F32)
        lo += wd


def _proj(x, g, w, tm):
    rows, n = x.shape[0], w.shape[1]
    col_chunk = 4 * LANES
    widths = [col_chunk] * (n // col_chunk) + ([n % col_chunk] if n % col_chunk else [])
    return pl.pallas_call(
        functools.partial(_proj_kernel, widths=tuple(widths)),
        grid=(rows // tm,),
        in_specs=[pl.BlockSpec((tm, D_MODEL), lambda i: (i, 0)), _whole((1, D_MODEL)), _whole(w.shape)],
        out_specs=pl.BlockSpec((tm, n), lambda i: (i, 0)),
        out_shape=jax.ShapeDtypeStruct((rows, n), F32),
        scratch_shapes=[pltpu.VMEM((tm, D_MODEL), BF16)],
        compiler_params=_params(1),
        name="norm_proj",
    )(x, g.reshape(1, D_MODEL), w)


def _matres_kernel(a_ref, w_ref, r_ref, o_ref):
    o_ref[...] = r_ref[...] + jnp.dot(a_ref[...].astype(BF16), w_ref[...], preferred_element_type=F32)


def _matres(a, w, res, tm):
    rows, k = a.shape
    n = w.shape[1]
    return pl.pallas_call(
        _matres_kernel,
        grid=(rows // tm,),
        in_specs=[pl.BlockSpec((tm, k), lambda i: (i, 0)), _whole(w.shape),
                  pl.BlockSpec((tm, n), lambda i: (i, 0))],
        out_specs=pl.BlockSpec((tm, n), lambda i: (i, 0)),
        out_shape=jax.ShapeDtypeStruct((rows, n), F32),
        compiler_params=_params(1),
        name="matmul_residual",
    )(a, w, res)


def _qkv_prompt_kernel(x_ref, g_ref, wqt_ref, wk_ref, wkt_ref, wvt_ref,
                       kp_ref, vp_ref, kb_ref, qt_ref, vt_ref, ks_ref, xn_ref, *, tm):
    xn_ref[...] = _rms(x_ref[...], g_ref[...]).astype(BF16)
    xn = xn_ref[...]
    k = jnp.dot(xn, wk_ref[...], preferred_element_type=F32)
    kb_ref[...] = k.astype(BF16)
    for j in range(tm // MOBA_BLOCK):
        ks_ref[0, j:j + 1, :] = jnp.sum(k[j * MOBA_BLOCK:(j + 1) * MOBA_BLOCK], axis=0, keepdims=True)
    nt = (((1,), (1,)), ((), ()))
    qt = lax.dot_general(wqt_ref[...], xn, nt, preferred_element_type=F32)
    qt_ref[...] = (qt * (LOG2_E * HEAD_DIM ** -0.5)).astype(BF16)
    kt = lax.dot_general(wkt_ref[...], xn, nt, preferred_element_type=F32)
    vt = lax.dot_general(wvt_ref[...], xn, nt, preferred_element_type=F32)
    for j in range(tm // MOBA_BLOCK):
        vt_ref[j] = vt[:, j * MOBA_BLOCK:(j + 1) * MOBA_BLOCK].astype(BF16)
    for pg in range(tm // PAGE_SIZE):
        rows = slice(pg * PAGE_SIZE, (pg + 1) * PAGE_SIZE)
        kp_ref[pg] = kt[:, rows].reshape(N_HEADS, HEAD_DIM, PAGE_SIZE)
        vp_ref[pg] = vt[:, rows].reshape(N_HEADS, HEAD_DIM, PAGE_SIZE)


def _qkv_prompt(x, g, wqt, wk, wkt, wvt, tm):
    s = x.shape[0]
    nblk = s // MOBA_BLOCK
    bpt = tm // MOBA_BLOCK
    ppt = tm // PAGE_SIZE
    row_spec = pl.BlockSpec((tm, D_MODEL), lambda i: (i, 0))
    page_spec = pl.BlockSpec((ppt, N_HEADS, HEAD_DIM, PAGE_SIZE), lambda i: (i, 0, 0, 0))
    pages = jax.ShapeDtypeStruct((s // PAGE_SIZE, N_HEADS, HEAD_DIM, PAGE_SIZE), F32)
    sq = _whole((D_MODEL, D_MODEL))
    return pl.pallas_call(
        functools.partial(_qkv_prompt_kernel, tm=tm),
        grid=(s // tm,),
        in_specs=[row_spec, _whole((1, D_MODEL)), sq, sq, sq, sq],
        out_specs=[page_spec, page_spec, row_spec,
                   pl.BlockSpec((D_MODEL, tm), lambda i: (0, i)),
                   pl.BlockSpec((bpt, D_MODEL, MOBA_BLOCK), lambda i: (i, 0, 0)),
                   pl.BlockSpec((1, bpt, D_MODEL), lambda i: (i, 0, 0))],
        out_shape=[pages, pages,
                   jax.ShapeDtypeStruct((s, D_MODEL), BF16),
                   jax.ShapeDtypeStruct((D_MODEL, s), BF16),
                   jax.ShapeDtypeStruct((nblk, D_MODEL, MOBA_BLOCK), BF16),
                   jax.ShapeDtypeStruct((s // tm, bpt, D_MODEL), F32)],
        scratch_shapes=[pltpu.VMEM((tm, D_MODEL), BF16)],
        compiler_params=_params(1),
        name="qkv_prompt",
    )(x, g.reshape(1, D_MODEL), wqt, wk, wkt, wvt)


def _softmax_group_step(tiles, v_tiles, state):
    m, acc = state
    m_new = m
    for s, keep, shift in tiles:
        bm = jnp.max(s, axis=0, keepdims=True) + shift
        m_new = jnp.maximum(m_new, bm if keep is None else jnp.where(keep > 0.0, bm, NEG_INF))
    acc = jnp.exp2(m - m_new) * acc
    for (s, keep, shift), v_t in zip(tiles, v_tiles):
        off = m_new - shift
        if keep is not None:
            off = jnp.where(keep > 0.0, off, jnp.inf)
        p = jnp.exp2(s - off)
        acc = acc + jnp.dot(v_t, p.astype(BF16), preferred_element_type=F32)
    return m_new, acc


def _moba_prompt_kernel(rb_ref, qt_ref, kb_ref, vt_ref, ks_ref, bkt_ref, o_ref,
                        sel_ref, bias_ref, sa_ref, sb_ref, *, nblk):
    slab = pl.program_id(0)
    qi = pl.program_id(1)
    blk = MOBA_BLOCK
    heads = range(ATTN_HEADS)

    @pl.when(qi == 0)
    def _build_bias():
        for hh in heads:
            for kind in range(2):
                bias_ref[hh, kind] = LOG2_E * _bias_from_buckets(bkt_ref[kind], rb_ref,
                                                                 slab * ATTN_HEADS + hh)

    qt = qt_ref[...].astype(F32)
    row_head = lax.broadcasted_iota(I32, (ATTN_WIDTH, blk), 0) // HEAD_DIM
    n_iota = lax.broadcasted_iota(I32, (nblk, blk), 0)
    kmean = (ks_ref[...] * (1.0 / blk)).astype(BF16)
    far_bias = [LOG2_E * rb_ref[REL_BUCKETS - 1, slab * ATTN_HEADS + hh] for hh in heads]
    qz = [jnp.where(row_head == hh, qt, 0.0).astype(BF16) for hh in heads]
    for hh in heads:
        scores = jnp.dot(kmean, qz[hh], preferred_element_type=F32)
        scores = jnp.where(n_iota < qi, scores, NEG_INF)
        sel = jnp.zeros((nblk, blk), F32)
        for idx, m in _top3_rows(scores, n_iota, nblk):
            sel = jnp.where((n_iota == idx) & (m > NEG_INF), 1.0, sel)
        sel_ref[hh] = sel

    def fill(buf, base):
        for j in range(KV_GROUP):
            nc = jnp.maximum(base + j, 0)
            k_n = kb_ref[pl.ds(pl.multiple_of(nc * blk, blk), blk), :]
            for hh in heads:
                buf[hh * KV_GROUP + j] = jnp.dot(k_n, qz[hh], preferred_element_type=F32)

    def consume(buf, base, kinds, states):
        out = []
        for hh in heads:
            tiles, v_tiles = [], []
            for j, kind in enumerate(kinds):
                n = base + j
                nc = jnp.maximum(n, 0)
                s = buf[hh * KV_GROUP + j]
                v_tiles.append(jnp.concatenate([vt_ref[nc, hh * HEAD_DIM:(hh + 1) * HEAD_DIM, :], ones], axis=0))
                if kind == "own":
                    tiles.append((s + bias_ref[hh, 0], None, 0.0))
                    continue
                keep = jnp.where(n >= 0, sel_ref[hh, pl.ds(nc, 1), :], 0.0)
                if kind == "prev":
                    tiles.append((s + bias_ref[hh, 1], keep, 0.0))
                else:
                    tiles.append((s, keep, far_bias[hh]))
            out.append(_softmax_group_step(tiles, v_tiles, states[hh]))
        return tuple(out)

    near_base = qi - (KV_GROUP - 1)
    near_kinds = ["far"] * (KV_GROUP - 2) + ["prev", "own"]
    far_kinds = ["far"] * KV_GROUP

    def far_base(g):
        return near_base - KV_GROUP * (g + 1)

    ones = jnp.ones((SUM_ROWS, blk), BF16)
    init = tuple((jnp.full((1, blk), NEG_INF, F32), jnp.zeros((HEAD_DIM + SUM_ROWS, blk), F32))
                 for _ in heads)
    fill(sa_ref, near_base)
    fill(sb_ref, far_base(0))
    states = consume(sa_ref, near_base, near_kinds, init)

    def far_pair(i, states):
        g = 2 * i
        fill(sa_ref, far_base(g + 1))
        states = consume(sb_ref, far_base(g), far_kinds, states)
        fill(sb_ref, far_base(g + 2))
        return consume(sa_ref, far_base(g + 1), far_kinds, states)

    n_far_groups = (jnp.maximum(near_base, 0) + KV_GROUP - 1) // KV_GROUP
    states = lax.fori_loop(0, (n_far_groups + 1) // 2, far_pair, states)
    outs = [acc[:HEAD_DIM] / acc[HEAD_DIM:HEAD_DIM + 1] for _, acc in states]
    o_ref[...] = jnp.concatenate(outs, axis=0).T.astype(BF16)


def _prompt_bucket_maps():
    kk = jnp.arange(MOBA_BLOCK, dtype=I32)[:, None]
    tt = jnp.arange(MOBA_BLOCK, dtype=I32)[None, :]
    own = jnp.where(tt - kk >= 0, _rel_bucket(tt - kk), -1)
    prev = _rel_bucket(MOBA_BLOCK + tt - kk)
    return jnp.stack([own, prev]).astype(I32)


def _moba_prompt(qt, kb, vt, ksum, rel_bias):
    s = kb.shape[0]
    nblk = s // MOBA_BLOCK
    return pl.pallas_call(
        functools.partial(_moba_prompt_kernel, nblk=nblk),
        grid=(N_HEADS // ATTN_HEADS, nblk),
        in_specs=[_smem(),
                  pl.BlockSpec((ATTN_WIDTH, MOBA_BLOCK), lambda sl, qi: (sl, qi)),
                  pl.BlockSpec((s, ATTN_WIDTH), lambda sl, qi: (0, sl), pipeline_mode=pl.Buffered(1)),
                  pl.BlockSpec((nblk, ATTN_WIDTH, MOBA_BLOCK), lambda sl, qi: (0, sl, 0),
                               pipeline_mode=pl.Buffered(1)),
                  pl.BlockSpec((nblk, ATTN_WIDTH), lambda sl, qi: (0, sl)),
                  _whole((2, MOBA_BLOCK, MOBA_BLOCK))],
        out_specs=pl.BlockSpec((MOBA_BLOCK, ATTN_WIDTH), lambda sl, qi: (qi, sl)),
        out_shape=jax.ShapeDtypeStruct((s, D_MODEL), BF16),
        scratch_shapes=[pltpu.VMEM((ATTN_HEADS, nblk, MOBA_BLOCK), F32),
                        pltpu.VMEM((ATTN_HEADS, 2, MOBA_BLOCK, MOBA_BLOCK), F32),
                        pltpu.VMEM((ATTN_HEADS * KV_GROUP, MOBA_BLOCK, MOBA_BLOCK), F32),
                        pltpu.VMEM((ATTN_HEADS * KV_GROUP, MOBA_BLOCK, MOBA_BLOCK), F32)],
        compiler_params=_params(2),
        name="moba_prompt",
    )(rel_bias, qt, kb, vt, ksum, _prompt_bucket_maps())


def _page_sum_kernel(pt_ref, *refs, pps):
    del pt_ref
    pages, o_ref = refs[:pps], refs[pps]
    for i in range(pps // PAGES_PER_BLOCK):
        tot = pages[PAGES_PER_BLOCK * i][0]
        for j in range(1, PAGES_PER_BLOCK):
            tot = tot + pages[PAGES_PER_BLOCK * i + j][0]
        o_ref[0, i] = jnp.sum(tot, axis=-1)


def _page_sums(cache, page_table):
    b, npg = page_table.shape
    pps = min(16, npg)
    bps = pps // PAGES_PER_BLOCK

    def page_spec(i):
        return pl.BlockSpec((1, N_HEADS, HEAD_DIM, PAGE_SIZE),
                            lambda bi, j, pt: (pt[bi * npg + j * pps + i], 0, 0, 0))

    sums = pl.pallas_call(
        functools.partial(_page_sum_kernel, pps=pps),
        grid_spec=pltpu.PrefetchScalarGridSpec(
            num_scalar_prefetch=1,
            grid=(b, npg // pps),
            in_specs=[page_spec(i) for i in range(pps)],
            out_specs=pl.BlockSpec((1, bps, N_HEADS, HEAD_DIM),
                                   lambda bi, j, pt: (bi * (npg // pps) + j, 0, 0, 0)),
        ),
        out_shape=jax.ShapeDtypeStruct((b * (npg // pps), bps, N_HEADS, HEAD_DIM), F32),
        compiler_params=_params(2),
        name="page_sums",
    )(page_table.reshape(-1), *([cache] * pps))
    return sums.reshape(b, npg // PAGES_PER_BLOCK, D_MODEL)


def _sample_gate_kernel(q_ref, bs_ref, seg_ref, sel_ref, *, t_new, nblk):
    kmean = bs_ref[0] * (1.0 / MOBA_BLOCK)
    n_iota = lax.broadcasted_iota(I32, (nblk, LANES), 0)
    sel_ref[...] = jnp.zeros_like(sel_ref)
    for t in range(t_new):
        prod = kmean * q_ref[0, t:t + 1, :]
        scores = _select_matmul(prod, seg_ref[...])
        for r, (idx, _) in enumerate(_top3_rows(scores, n_iota, nblk)):
            sel_ref[0, t, r:r + 1, :] = idx


def _sample_gate(q, bsum):
    b, t_new, _ = q.shape
    nblk = bsum.shape[1]
    seg = (jnp.arange(D_MODEL, dtype=I32)[:, None] // HEAD_DIM
           == jnp.arange(LANES, dtype=I32)[None, :]).astype(BF16)
    sel = pl.pallas_call(
        functools.partial(_sample_gate_kernel, t_new=t_new, nblk=nblk),
        grid=(b,),
        in_specs=[pl.BlockSpec((1, t_new, D_MODEL), lambda i: (i, 0, 0)),
                  pl.BlockSpec((1, nblk, D_MODEL), lambda i: (i, 0, 0)),
                  _whole((D_MODEL, LANES))],
        out_specs=pl.BlockSpec((1, t_new, SUBLANES, LANES), lambda i: (i, 0, 0, 0)),
        out_shape=jax.ShapeDtypeStruct((b, t_new, SUBLANES, LANES), I32),
        compiler_params=_params(1),
        name="sample_gate",
    )(q, bsum, seg)
    return sel[:, :, :MOBA_TOPK, :N_HEADS]


SLABS_PER_HEAD = MOBA_TOPK * PAGES_PER_BLOCK


def _sample_attn_kernel(sel_ref, pt_ref, rb_ref, q_ref, kn_ref, vn_ref, pb_ref, ck_hbm, cv_hbm, o_ref,
                        kbuf, vbuf, sems, o_scr, *, t_new, npg):
    bi, t = pl.program_id(0), pl.program_id(1)
    step = bi * t_new + t
    n_steps = pl.num_programs(0) * t_new
    slot = lax.rem(step, 2)
    last_blk = npg // PAGES_PER_BLOCK - 1

    def selected_block(b_, t_, j, h):
        return sel_ref[((b_ * t_new + t_) * MOBA_TOPK + j) * N_HEADS + h]

    def slab_copies(b_, t_, slot_):
        copies = []
        for h in range(N_HEADS):
            for j in range(MOBA_TOPK):
                blk = selected_block(b_, t_, j, h)
                for pg in range(PAGES_PER_BLOCK):
                    page = pt_ref[b_ * npg + blk * PAGES_PER_BLOCK + pg]
                    i = j * PAGES_PER_BLOCK + pg
                    copies.append(pltpu.make_async_copy(ck_hbm.at[page, h], kbuf.at[slot_, h, i],
                                                        sems.at[0, slot_]))
                    copies.append(pltpu.make_async_copy(cv_hbm.at[page, h], vbuf.at[slot_, h, i],
                                                        sems.at[1, slot_]))
        return copies

    @pl.when(step == 0)
    def _first_fetch():
        for c in slab_copies(bi, t, slot):
            c.start()

    @pl.when(step + 1 < n_steps)
    def _prefetch_next():
        nxt = step + 1
        for c in slab_copies(lax.div(nxt, t_new), lax.rem(nxt, t_new), 1 - slot):
            c.start()

    for c in slab_copies(bi, t, slot):
        c.wait()

    row = lax.broadcasted_iota(I32, (SUBLANES, 1), 0)
    q = q_ref[0, 0] * (HEAD_DIM ** -0.5)
    for h in range(N_HEADS):
        cols = slice(h * HEAD_DIM, (h + 1) * HEAD_DIM)
        far_bias = rb_ref[REL_BUCKETS - 1, h]
        qh = q[:, cols]
        q8 = jnp.broadcast_to(qh, (SUBLANES, HEAD_DIM)).astype(BF16)
        l1 = []
        for j in range(MOBA_TOPK):
            blk_j = selected_block(bi, t, j, h)
            for pg in range(PAGES_PER_BLOCK):
                k_t = kbuf[slot, h, j * PAGES_PER_BLOCK + pg].astype(BF16)
                near = _bias_from_buckets(pb_ref[pl.ds(t * PAGES_PER_BLOCK + pg, 1), :], rb_ref, h)
                l1.append(jnp.dot(q8, k_t, preferred_element_type=F32)[0:1]
                          + jnp.where(blk_j == last_blk, near, far_bias))
        l1 = jnp.concatenate(l1, axis=1)
        l2 = jnp.sum(kn_ref[0, :, cols] * qh, axis=-1, keepdims=True)
        bias2 = jnp.zeros((SUBLANES, 1), F32)
        for u in range(t_new):
            bias2 = jnp.where(row == u, rb_ref[jnp.maximum(t - u, 0), h], bias2)
        l2 = jnp.where(row <= t, l2 + bias2, NEG_INF)
        m = jnp.maximum(jnp.max(l1, axis=-1, keepdims=True), jnp.max(l2, axis=0, keepdims=True))
        p1 = jnp.exp(l1 - m)
        p2 = jnp.exp(l2 - m)
        denom = jnp.sum(p1, axis=-1, keepdims=True) + jnp.sum(p2, axis=0, keepdims=True)
        p8 = jnp.broadcast_to(p1, (SUBLANES, SLABS_PER_HEAD * PAGE_SIZE)).astype(BF16)
        o = jnp.sum(p2 * vn_ref[0, :, cols], axis=0, keepdims=True)
        for i in range(SLABS_PER_HEAD):
            v_t = vbuf[slot, h, i].astype(BF16)
            o = o + lax.dot_general(p8[:, i * PAGE_SIZE:(i + 1) * PAGE_SIZE], v_t, (((1,), (1,)), ((), ())),
                                    preferred_element_type=F32)[0:1]
        o_scr[:, cols] = jnp.broadcast_to(o / denom, (SUBLANES, HEAD_DIM))
    o_ref[0, 0] = o_scr[0:1, :]


def _sample_attn(q, k_new, v_new, cache_k, cache_v, sel, page_table, rel_bias):
    b, t_new, _ = q.shape
    npg = page_table.shape[1]
    pad = ((0, 0), (0, SUBLANES - t_new), (0, 0))
    tt = jnp.arange(t_new, dtype=I32)[:, None, None]
    pg = jnp.arange(PAGES_PER_BLOCK, dtype=I32)[None, :, None]
    rr = jnp.arange(PAGE_SIZE, dtype=I32)[None, None, :]
    near_bkt = _rel_bucket(MOBA_BLOCK + tt - pg * PAGE_SIZE - rr).reshape(t_new * PAGES_PER_BLOCK, PAGE_SIZE)

    new_spec = pl.BlockSpec((1, SUBLANES, D_MODEL), lambda bi, t, *_: (bi, 0, 0))
    tok_spec = pl.BlockSpec((1, 1, 1, D_MODEL), lambda bi, t, *_: (bi, t, 0, 0))
    slab_buf = pltpu.VMEM((2, N_HEADS, SLABS_PER_HEAD, HEAD_DIM, PAGE_SIZE), F32)
    out = pl.pallas_call(
        functools.partial(_sample_attn_kernel, t_new=t_new, npg=npg),
        grid_spec=pltpu.PrefetchScalarGridSpec(
            num_scalar_prefetch=2,
            grid=(b, t_new),
            in_specs=[_smem(), tok_spec, new_spec, new_spec,
                      pl.BlockSpec((t_new * PAGES_PER_BLOCK, PAGE_SIZE), lambda *_: (0, 0)),
                      pl.BlockSpec(memory_space=pl.ANY), pl.BlockSpec(memory_space=pl.ANY)],
            out_specs=tok_spec,
            scratch_shapes=[slab_buf, slab_buf, pltpu.SemaphoreType.DMA((2, 2)),
                            pltpu.VMEM((SUBLANES, D_MODEL), F32)],
        ),
        out_shape=jax.ShapeDtypeStruct((b, t_new, 1, D_MODEL), F32),
        compiler_params=_params(2),
        name="sample_attn",
    )(sel.reshape(-1), page_table.reshape(-1), rel_bias,
      q.reshape(b, t_new, 1, D_MODEL), jnp.pad(k_new, pad), jnp.pad(v_new, pad), near_bkt,
      cache_k, cache_v)
    return out.reshape(b, t_new, D_MODEL)


def _ssd_chunk(t, ext_ref, p_ref, st_ref, gy_ref, cw_ref, cb_ref, dtb_ref, alog_ref, dsk_ref,
               nw_ref, e_ref, wout_ref, n_valid):
    def conv_act(lo, width):
        acc = cb_ref[:, lo:lo + width]
        for j in range(CONV_WIDTH):
            acc = acc + cw_ref[j:j + 1, lo:lo + width] * ext_ref[5 + j:5 + j + t, lo:lo + width]
        return _silu(acc)

    dt = _softplus(p_ref[:, DT_OFF:DT_OFF + LANES] + dtb_ref[...])
    if n_valid < t:
        dt = jnp.where(lax.broadcasted_iota(I32, (t, LANES), 0) < n_valid, dt, 0.0)
    a = dt * -jnp.exp(alog_ref[...])
    r_i = lax.broadcasted_iota(I32, (t, t), 0)
    c_i = lax.broadcasted_iota(I32, (t, t), 1)
    causal = r_i >= c_i
    acs = jnp.dot(causal.astype(F32), a, preferred_element_type=F32,
                  precision=lax.Precision.HIGHEST)
    if t % LANES == 0:
        acs_t = acs.T
    else:
        acs_t = lax.dot_general(jnp.eye(LANES, dtype=F32), acs, (((1,), (1,)), ((), ())),
                                preferred_element_type=F32, precision=lax.Precision.HIGHEST)
    lane_hi = lax.broadcasted_iota(I32, (t, LANES), 1) >= SSM_HEAD_DIM

    for g in range(SSM_GROUPS):
        ch = slice(g * GROUP_DIM, (g + 1) * GROUP_DIM)
        e_g = e_ref[:, ch]
        xs = conv_act(g * GROUP_DIM, GROUP_DIM)
        b_g = conv_act(D_INNER + g * SSM_STATE, SSM_STATE).astype(BF16)
        c_g = conv_act(D_INNER + (SSM_GROUPS + g) * SSM_STATE, SSM_STATE).astype(BF16)
        dt_g = _select_matmul(dt, e_g)
        acs_g = _select_matmul(acs, e_g)
        last_g = acs_g[t - 1:t, :]
        xdt = xs * dt_g
        xdt_b = xdt.astype(BF16)
        xdt_dec = (xdt * jnp.exp(last_g - acs_g)).astype(BF16)
        cb = lax.dot_general(c_g, b_g, (((1,), (1,)), ((), ())), preferred_element_type=F32)
        st_in = st_ref[:, ch]
        y_off = jnp.dot(c_g, st_in.astype(BF16), preferred_element_type=F32) * jnp.exp(acs_g)
        st_new = lax.dot_general(b_g, xdt_dec, (((0,), (0,)), ((), ())), preferred_element_type=F32)
        st_ref[:, ch] = st_in * jnp.exp(last_g) + st_new
        for j in range(HEADS_PER_GROUP // HEADS_PER_SLAB):
            cols = slice(j * LANES, (j + 1) * LANES)
            pair = xdt_b[:, cols]
            ys = []
            for hh in range(HEADS_PER_SLAB):
                h = g * HEADS_PER_GROUP + j * HEADS_PER_SLAB + hh
                decay = jnp.exp(jnp.where(causal, acs[:, h:h + 1] - acs_t[h:h + 1, :], NEG_INF))
                ys.append(jnp.dot((cb * decay).astype(BF16), pair, preferred_element_type=F32))
            y_diag = jnp.where(lane_hi, ys[1], ys[0])
            lo = g * GROUP_DIM + j * LANES
            y = y_diag + y_off[:, cols] + dsk_ref[:, lo:lo + LANES] * xs[:, cols]
            gy_ref[:, lo:lo + LANES] = y * _silu(p_ref[:, lo:lo + LANES])
        gy = gy_ref[:, ch]
        ms = jnp.mean(gy * gy, axis=-1, keepdims=True)
        gy_ref[:, ch] = gy * lax.rsqrt(ms + RMS_EPS) * nw_ref[:, ch]
    return jnp.dot(gy_ref[...].astype(BF16), wout_ref[...], preferred_element_type=F32)


def _ssm_prompt_kernel(p_ref, xres_ref, cw_ref, cb_ref, dtb_ref, alog_ref, dsk_ref, nw_ref, e_ref,
                       wout_ref, y_ref, st_out_ref, cv_out_ref, ext_ref, st_ref, gy_ref):
    c = pl.program_id(0)
    t = SSD_CHUNK

    @pl.when(c == 0)
    def _init():
        ext_ref[t:t + SUBLANES, :] = jnp.zeros((SUBLANES, CONV_DIM), F32)
        st_ref[...] = jnp.zeros_like(st_ref)

    ext_ref[0:SUBLANES, :] = ext_ref[t:t + SUBLANES, :]
    ext_ref[SUBLANES:SUBLANES + t, :] = p_ref[:, D_INNER:D_INNER + CONV_DIM]
    out = _ssd_chunk(t, ext_ref, p_ref, st_ref, gy_ref, cw_ref, cb_ref, dtb_ref, alog_ref, dsk_ref,
                     nw_ref, e_ref, wout_ref, t)
    y_ref[...] = xres_ref[...] + out

    @pl.when(c == pl.num_programs(0) - 1)
    def _finish():
        st_out_ref[...] = st_ref[...].T
        cv_out_ref[...] = ext_ref[t:t + SUBLANES, :]


def _ssm_consts(w):
    conv_w, conv_b, dt_bias, a_log, d_skip, norm_w, w_out = w
    lane_pad = (0, LANES - SSM_HEADS)
    expand = (jnp.arange(LANES, dtype=I32)[:, None]
              == jnp.arange(D_INNER, dtype=I32)[None, :] // SSM_HEAD_DIM).astype(BF16)
    return (conv_w, conv_b.reshape(1, CONV_DIM),
            jnp.pad(dt_bias, lane_pad).reshape(1, LANES), jnp.pad(a_log, lane_pad).reshape(1, LANES),
            jnp.repeat(d_skip, SSM_HEAD_DIM).reshape(1, D_INNER), norm_w.reshape(1, D_INNER),
            expand, w_out)


def _const_specs(consts):
    return [_whole(c.shape) for c in consts]


def _ssm_prompt(proj, xres, w):
    s = proj.shape[0]
    t = SSD_CHUNK
    consts = _ssm_consts(w)
    y, st, cv = pl.pallas_call(
        _ssm_prompt_kernel,
        grid=(s // t,),
        in_specs=[pl.BlockSpec((t, SSM_PROJ_PAD), lambda c: (c, 0)),
                  pl.BlockSpec((t, D_MODEL), lambda c: (c, 0))] + _const_specs(consts),
        out_specs=[pl.BlockSpec((t, D_MODEL), lambda c: (c, 0)),
                   _whole((D_INNER, SSM_STATE)), _whole((SUBLANES, CONV_DIM))],
        out_shape=[jax.ShapeDtypeStruct((s, D_MODEL), F32),
                   jax.ShapeDtypeStruct((D_INNER, SSM_STATE), F32),
                   jax.ShapeDtypeStruct((SUBLANES, CONV_DIM), F32)],
        scratch_shapes=[pltpu.VMEM((t + 2 * SUBLANES, CONV_DIM), F32),
                        pltpu.VMEM((SSM_STATE, D_INNER), F32),
                        pltpu.VMEM((t, D_INNER), F32)],
        compiler_params=_params(1),
        name="ssm_prompt",
    )(proj, xres, *consts)
    return y, st, cv[SUBLANES - (CONV_WIDTH - 1):]


def _ssm_sample_kernel(p_ref, xres_ref, cst_ref, st_in_ref, cw_ref, cb_ref, dtb_ref, alog_ref,
                       dsk_ref, nw_ref, e_ref, wout_ref, y_ref, st_out_ref, cv_out_ref,
                       ext_ref, st_ref, gy_ref, *, t_new):
    t = SAMPLE_T
    ext_ref[0:SUBLANES, :] = cst_ref[0]
    ext_ref[SUBLANES:SUBLANES + t, :] = p_ref[0, :, D_INNER:D_INNER + CONV_DIM]
    st_ref[...] = st_in_ref[0].T
    out = _ssd_chunk(t, ext_ref, p_ref.at[0], st_ref, gy_ref, cw_ref, cb_ref, dtb_ref, alog_ref,
                     dsk_ref, nw_ref, e_ref, wout_ref, t_new)
    y_ref[0] = xres_ref[0] + out[0:t_new]
    st_out_ref[0] = st_ref[...].T
    cv_out_ref[0] = ext_ref[SUBLANES:SUBLANES + t, :]


def _ssm_sample(proj, xres, conv_state, ssm_state, w):
    b, t_new, _ = proj.shape
    t = SAMPLE_T
    consts = _ssm_consts(w)
    proj8 = jnp.pad(proj, ((0, 0), (0, t - t_new), (0, 0)))
    cst8 = jnp.pad(conv_state, ((0, 0), (SUBLANES - (CONV_WIDTH - 1), 0), (0, 0)))
    y, st, cv = pl.pallas_call(
        functools.partial(_ssm_sample_kernel, t_new=t_new),
        grid=(b,),
        in_specs=[pl.BlockSpec((1, t, SSM_PROJ_PAD), lambda i: (i, 0, 0)),
                  pl.BlockSpec((1, t_new, D_MODEL), lambda i: (i, 0, 0)),
                  pl.BlockSpec((1, SUBLANES, CONV_DIM), lambda i: (i, 0, 0)),
                  pl.BlockSpec((1, D_INNER, SSM_STATE), lambda i: (i, 0, 0))] + _const_specs(consts),
        out_specs=[pl.BlockSpec((1, t_new, D_MODEL), lambda i: (i, 0, 0)),
                   pl.BlockSpec((1, D_INNER, SSM_STATE), lambda i: (i, 0, 0)),
                   pl.BlockSpec((1, t, CONV_DIM), lambda i: (i, 0, 0))],
        out_shape=[jax.ShapeDtypeStruct((b, t_new, D_MODEL), F32),
                   jax.ShapeDtypeStruct((b, D_INNER, SSM_STATE), F32),
                   jax.ShapeDtypeStruct((b, t, CONV_DIM), F32)],
        scratch_shapes=[pltpu.VMEM((2 * SUBLANES, CONV_DIM), F32),
                        pltpu.VMEM((SSM_STATE, D_INNER), F32),
                        pltpu.VMEM((t, D_INNER), F32)],
        compiler_params=_params(1),
        name="ssm_sample",
    )(proj8, xres, cst8, ssm_state, *consts)
    return y, st, cv[:, t_new - (CONV_WIDTH - 1):t_new]


def _ffn_weights(w_in, w_out):
    def chunks(w):
        return w.reshape(D_MODEL, N_FF_CHUNKS, FF_CHUNK).transpose(1, 0, 2).astype(BF16)
    return (chunks(w_in[:, :D_FF]), chunks(w_in[:, D_FF:]),
            w_out.reshape(N_FF_CHUNKS, FF_CHUNK, D_MODEL).astype(BF16))


def _row_tile(rows, want=512):
    return want if rows % want == 0 else 128


def kernel(x_prompt, x_sample, cache_k, cache_v, page_table, state_ssm, state_conv, ffn1_norm, ffn1_w_in, ffn1_w_out, mix_norm, ffn2_norm, ffn2_w_in, ffn2_w_out, attn_w_qkv, attn_w_o, rel_bias, ssm_w_in, ssm_conv_w, ssm_conv_b, ssm_dt_bias, ssm_A_log, ssm_D, ssm_norm, ssm_w_out, final_norm):
    bp, s, _ = x_prompt.shape
    bs, t_new, _ = x_sample.shape
    assert bp == 1 and attn_w_qkv.shape[0] == 1 and ssm_w_in.shape[0] == 1
    xp = x_prompt.reshape(s, D_MODEL)
    xs = x_sample.reshape(bs * t_new, D_MODEL)
    tp, ts = _row_tile(s), _row_tile(bs * t_new)
    tf = _row_tile(s, 1024)

    w1 = _ffn_weights(ffn1_w_in[0], ffn1_w_out[0])
    xp = _ffn(xp, ffn1_norm[0], w1, tf)
    xs = _ffn(xs, ffn1_norm[0], w1, ts)

    wqkv = attn_w_qkv[0].astype(BF16)
    wq, wk, wv = (wqkv[:, i * D_MODEL:(i + 1) * D_MODEL] for i in range(3))
    wo = attn_w_o[0].astype(BF16)
    k_p, v_p, kb, qt, vt, ksum = _qkv_prompt(xp, mix_norm[0], wq.T, wk, wk.T, wv.T, tp)
    o_p = _moba_prompt(qt, kb, vt, ksum.reshape(s // MOBA_BLOCK, D_MODEL), rel_bias)
    xp = _matres(o_p, wo, xp, tp)

    qkv_s = _proj(xs, mix_norm[0], wqkv, ts).reshape(bs, t_new, 3 * D_MODEL)
    q_s, k_s, v_s = (qkv_s[..., i * D_MODEL:(i + 1) * D_MODEL] for i in range(3))
    ck = jnp.transpose(cache_k[0], (0, 2, 3, 1))
    cv = jnp.transpose(cache_v[0], (0, 2, 3, 1))
    sel = _sample_gate(q_s, _page_sums(ck, page_table))
    o_s = _sample_attn(q_s, k_s, v_s, ck, cv, sel, page_table, rel_bias)
    xs = _matres(o_s.reshape(bs * t_new, D_MODEL), wo, xs, ts)

    w2 = _ffn_weights(ffn2_w_in[0], ffn2_w_out[0])
    xp = _ffn(xp, ffn2_norm[0], w2, tf)
    xs = _ffn(xs, ffn2_norm[0], w2, ts)

    w1 = _ffn_weights(ffn1_w_in[1], ffn1_w_out[1])
    xp = _ffn(xp, ffn1_norm[1], w1, tf)
    xs = _ffn(xs, ffn1_norm[1], w1, ts)

    w_in = jnp.pad(ssm_w_in[0], ((0, 0), (0, SSM_PROJ_PAD - SSM_IN_DIM))).astype(BF16)
    ssm_w = (ssm_conv_w[0], ssm_conv_b[0], ssm_dt_bias[0], ssm_A_log[0], ssm_D[0], ssm_norm[0],
             ssm_w_out[0].astype(BF16))
    xp, st_p, cv_p = _ssm_prompt(_proj(xp, mix_norm[1], w_in, tp), xp, ssm_w)
    proj_s = _proj(xs, mix_norm[1], w_in, ts).reshape(bs, t_new, SSM_PROJ_PAD)
    xs3, st_s, cv_s = _ssm_sample(proj_s, xs.reshape(bs, t_new, D_MODEL), state_conv[0],
                                  state_ssm[0].reshape(bs, D_INNER, SSM_STATE), ssm_w)
    xs = xs3.reshape(bs * t_new, D_MODEL)

    w2 = _ffn_weights(ffn2_w_in[1], ffn2_w_out[1])
    y_p = _ffn(xp, ffn2_norm[1], w2, tf, final_g=final_norm)
    y_s = _ffn(xs, ffn2_norm[1], w2, ts, final_g=final_norm)

    head_shape = (N_HEADS, HEAD_DIM)
    state_shape = (SSM_HEADS, SSM_HEAD_DIM, SSM_STATE)
    return (y_p.reshape(1, s, D_MODEL), y_s.reshape(bs, t_new, D_MODEL),
            jnp.transpose(k_p, (0, 3, 1, 2))[None, None], jnp.transpose(v_p, (0, 3, 1, 2))[None, None],
            k_s.reshape(1, bs, t_new, *head_shape), v_s.reshape(1, bs, t_new, *head_shape),
            st_p.reshape(1, 1, *state_shape), cv_p.reshape(1, 1, CONV_WIDTH - 1, CONV_DIM),
            st_s.reshape(1, bs, *state_shape), cv_s.reshape(1, bs, CONV_WIDTH - 1, CONV_DIM))
```

```python
import functools
import math

import jax
import jax.numpy as jnp
from jax import lax
from jax.experimental import pallas as pl
from jax.experimental.pallas import tpu as pltpu

F32 = jnp.float32
BF16 = jnp.bfloat16
I32 = jnp.int32
NEG_INF = float("-inf")
LOG2_E = math.log2(math.e)

D_MODEL = 1024
N_HEADS = 16
HEAD_DIM = 64
MOBA_BLOCK = 256
MOBA_TOPK = 3
PAGE_SIZE = 128
REL_BUCKETS = 32
REL_MAX_DIST = 128
D_INNER = 2048
SSM_HEADS = 32
SSM_HEAD_DIM = 64
SSM_GROUPS = 4
SSM_STATE = 128
CONV_WIDTH = 4
CONV_DIM = D_INNER + 2 * SSM_GROUPS * SSM_STATE
SSM_IN_DIM = 2 * D_INNER + 2 * SSM_GROUPS * SSM_STATE + SSM_HEADS
SSD_CHUNK = 256
D_FF = 2816
RMS_EPS = 1e-6

LANES = 128
SUBLANES = 8
VMEM_LIMIT_BYTES = 56 * 1024 * 1024

FF_CHUNK = 256
N_FF_CHUNKS = D_FF // FF_CHUNK
HEADS_PER_SLAB = LANES // HEAD_DIM
N_SLABS = N_HEADS // HEADS_PER_SLAB
GROUP_DIM = D_INNER // SSM_GROUPS
HEADS_PER_GROUP = SSM_HEADS // SSM_GROUPS
SSM_PROJ_PAD = 5248
DT_OFF = D_INNER + CONV_DIM
SAMPLE_T = 8
PAGES_PER_BLOCK = MOBA_BLOCK // PAGE_SIZE
KV_GROUP = 4
ATTN_HEADS = 4
ATTN_WIDTH = ATTN_HEADS * HEAD_DIM
SUM_ROWS = 16


def _params(n_axes, vmem=VMEM_LIMIT_BYTES):
    return pltpu.CompilerParams(dimension_semantics=("arbitrary",) * n_axes,
                                vmem_limit_bytes=vmem)


def _whole(shape):
    nd = len(shape)
    return pl.BlockSpec(shape, lambda *_: (0,) * nd)


def _smem():
    return pl.BlockSpec(memory_space=pltpu.SMEM)


def _rms(x, g):
    ms = jnp.mean(x * x, axis=-1, keepdims=True)
    return x * lax.rsqrt(ms + RMS_EPS) * g


def _silu(x):
    return x * jax.nn.sigmoid(x)


def _softplus(x):
    return jnp.maximum(x, 0.0) + jnp.log1p(jnp.exp(-jnp.abs(x)))


def _split3(x):
    hi = x.astype(BF16)
    r1 = x - hi.astype(F32)
    mid = r1.astype(BF16)
    lo = (r1 - mid.astype(F32)).astype(BF16)
    return hi, mid, lo


def _select_matmul(x, sel):
    hi, mid, lo = _split3(x)
    dot = functools.partial(jnp.dot, preferred_element_type=F32)
    return dot(hi, sel) + dot(mid, sel) + dot(lo, sel)


def _top3_rows(s, n_iota, sentinel):
    picks = []
    for _ in range(MOBA_TOPK):
        m = jnp.max(s, axis=0, keepdims=True)
        idx = jnp.min(jnp.where(s == m, n_iota, sentinel), axis=0, keepdims=True)
        picks.append((idx, m))
        s = jnp.where(n_iota == idx, NEG_INF, s)
    return picks


def _bias_from_buckets(bk, rb_ref, h):
    out = jnp.full(bk.shape, NEG_INF, F32)
    for b in range(REL_BUCKETS):
        out = jnp.where(bk == b, rb_ref[b, h], out)
    return out


def _rel_bucket(dist):
    n = jnp.maximum(dist, 0)
    max_exact = REL_BUCKETS // 2
    nf = jnp.maximum(n, 1).astype(F32)
    large = max_exact + (jnp.log(nf / max_exact) / math.log(REL_MAX_DIST / max_exact)
                         * (REL_BUCKETS - max_exact)).astype(I32)
    large = jnp.minimum(large, REL_BUCKETS - 1)
    return jnp.where(n < max_exact, n, large).astype(I32)


def _ffn_kernel(*refs, final):
    if final:
        x_ref, g_ref, wa_ref, wb_ref, wo_ref, fg_ref, o_ref, xn_ref, acc_ref = refs
    else:
        x_ref, g_ref, wa_ref, wb_ref, wo_ref, o_ref, xn_ref, acc_ref = refs
    xn_ref[...] = _rms(x_ref[...], g_ref[...]).astype(BF16)
    acc_ref[...] = jnp.zeros_like(acc_ref)

    def chunk(c, carry):
        xn = xn_ref[...]
        a = jnp.dot(xn, wa_ref[c], preferred_element_type=F32)
        b = jnp.dot(xn, wb_ref[c], preferred_element_type=F32)
        h = (_silu(a) * b).astype(BF16)
        acc_ref[...] += jnp.dot(h, wo_ref[c], preferred_element_type=F32)
        return carry

    lax.fori_loop(0, N_FF_CHUNKS, chunk, 0)
    y = x_ref[...] + 0.5 * acc_ref[...]
    if final:
        y = _rms(y, fg_ref[...])
    o_ref[...] = y


def _ffn(x, g, w, tm, final_g=None):
    rows = x.shape[0]
    wa, wb, wo = w
    row_spec = pl.BlockSpec((tm, D_MODEL), lambda i: (i, 0))
    in_specs = [row_spec, _whole((1, D_MODEL)), _whole(wa.shape), _whole(wb.shape), _whole(wo.shape)]
    args = [x, g.reshape(1, D_MODEL), wa, wb, wo]
    if final_g is not None:
        in_specs.append(_whole((1, D_MODEL)))
        args.append(final_g.reshape(1, D_MODEL))
    return pl.pallas_call(
        functools.partial(_ffn_kernel, final=final_g is not None),
        grid=(rows // tm,),
        in_specs=in_specs,
        out_specs=row_spec,
        out_shape=jax.ShapeDtypeStruct((rows, D_MODEL), F32),
        scratch_shapes=[pltpu.VMEM((tm, D_MODEL), BF16), pltpu.VMEM((tm, D_MODEL), F32)],
        compiler_params=_params(1),
        name="ffn",
    )(*args)


def _proj_kernel(x_ref, g_ref, w_ref, o_ref, xn_ref, *, widths):
    xn_ref[...] = _rms(x_ref[...], g_ref[...]).astype(BF16)
    lo = 0
    for wd in widths:
        o_ref[:, lo:lo + wd] = jnp.dot(xn_ref[...], w_ref[:, lo:lo + wd], preferred_element_type=F32)
        lo += wd


def _proj(x, g, w, tm):
    rows, n = x.shape[0], w.shape[1]
    col_chunk = 4 * LANES
    widths = [col_chunk] * (n // col_chunk) + ([n % col_chunk] if n % col_chunk else [])
    return pl.pallas_call(
        functools.partial(_proj_kernel, widths=tuple(widths)),
        grid=(rows // tm,),
        in_specs=[pl.BlockSpec((tm, D_MODEL), lambda i: (i, 0)), _whole((1, D_MODEL)), _whole(w.shape)],
        out_specs=pl.BlockSpec((tm, n), lambda i: (i, 0)),
        out_shape=jax.ShapeDtypeStruct((rows, n), F32),
        scratch_shapes=[pltpu.VMEM((tm, D_MODEL), BF16)],
        compiler_params=_params(1),
        name="norm_proj",
    )(x, g.reshape(1, D_MODEL), w)


def _matres_kernel(a_ref, w_ref, r_ref, o_ref):
    o_ref[...] = r_ref[...] + jnp.dot(a_ref[...].astype(BF16), w_ref[...], preferred_element_type=F32)


def _matres(a, w, res, tm):
    rows, k = a.shape
    n = w.shape[1]
    return pl.pallas_call(
        _matres_kernel,
        grid=(rows // tm,),
        in_specs=[pl.BlockSpec((tm, k), lambda i: (i, 0)), _whole(w.shape),
                  pl.BlockSpec((tm, n), lambda i: (i, 0))],
        out_specs=pl.BlockSpec((tm, n), lambda i: (i, 0)),
        out_shape=jax.ShapeDtypeStruct((rows, n), F32),
        compiler_params=_params(1),
        name="matmul_residual",
    )(a, w, res)


def _qkv_prompt_kernel(x_ref, g_ref, wqt_ref, wk_ref, wkt_ref, wvt_ref,
                       kp_ref, vp_ref, kb_ref, qt_ref, vt_ref, ks_ref, xn_ref, *, tm):
    xn_ref[...] = _rms(x_ref[...], g_ref[...]).astype(BF16)
    xn = xn_ref[...]
    k = jnp.dot(xn, wk_ref[...], preferred_element_type=F32)
    kb_ref[...] = k.astype(BF16)
    for j in range(tm // MOBA_BLOCK):
        ks_ref[0, j:j + 1, :] = jnp.sum(k[j * MOBA_BLOCK:(j + 1) * MOBA_BLOCK], axis=0, keepdims=True)
    nt = (((1,), (1,)), ((), ()))
    qt = lax.dot_general(wqt_ref[...], xn, nt, preferred_element_type=F32)
    qt_ref[...] = (qt * (LOG2_E * HEAD_DIM ** -0.5)).astype(BF16)
    kt = lax.dot_general(wkt_ref[...], xn, nt, preferred_element_type=F32)
    vt = lax.dot_general(wvt_ref[...], xn, nt, preferred_element_type=F32)
    for j in range(tm // MOBA_BLOCK):
        vt_ref[j] = vt[:, j * MOBA_BLOCK:(j + 1) * MOBA_BLOCK].astype(BF16)
    for pg in range(tm // PAGE_SIZE):
        rows = slice(pg * PAGE_SIZE, (pg + 1) * PAGE_SIZE)
        kp_ref[pg] = kt[:, rows].reshape(N_HEADS, HEAD_DIM, PAGE_SIZE)
        vp_ref[pg] = vt[:, rows].reshape(N_HEADS, HEAD_DIM, PAGE_SIZE)


def _qkv_prompt(x, g, wqt, wk, wkt, wvt, tm):
    s = x.shape[0]
    nblk = s // MOBA_BLOCK
    bpt = tm // MOBA_BLOCK
    ppt = tm // PAGE_SIZE
    row_spec = pl.BlockSpec((tm, D_MODEL), lambda i: (i, 0))
    page_spec = pl.BlockSpec((ppt, N_HEADS, HEAD_DIM, PAGE_SIZE), lambda i: (i, 0, 0, 0))
    pages = jax.ShapeDtypeStruct((s // PAGE_SIZE, N_HEADS, HEAD_DIM, PAGE_SIZE), F32)
    sq = _whole((D_MODEL, D_MODEL))
    return pl.pallas_call(
        functools.partial(_qkv_prompt_kernel, tm=tm),
        grid=(s // tm,),
        in_specs=[row_spec, _whole((1, D_MODEL)), sq, sq, sq, sq],
        out_specs=[page_spec, page_spec, row_spec,
                   pl.BlockSpec((D_MODEL, tm), lambda i: (0, i)),
                   pl.BlockSpec((bpt, D_MODEL, MOBA_BLOCK), lambda i: (i, 0, 0)),
                   pl.BlockSpec((1, bpt, D_MODEL), lambda i: (i, 0, 0))],
        out_shape=[pages, pages,
                   jax.ShapeDtypeStruct((s, D_MODEL), BF16),
                   jax.ShapeDtypeStruct((D_MODEL, s), BF16),
                   jax.ShapeDtypeStruct((nblk, D_MODEL, MOBA_BLOCK), BF16),
                   jax.ShapeDtypeStruct((s // tm, bpt, D_MODEL), F32)],
        scratch_shapes=[pltpu.VMEM((tm, D_MODEL), BF16)],
        compiler_params=_params(1),
        name="qkv_prompt",
    )(x, g.reshape(1, D_MODEL), wqt, wk, wkt, wvt)


def _block_sums(pages, o_ref):
    for i in range(len(pages) // PAGES_PER_BLOCK):
        tot = pages[PAGES_PER_BLOCK * i][0]
        for j in range(1, PAGES_PER_BLOCK):
            tot = tot + pages[PAGES_PER_BLOCK * i + j][0]
        o_ref[0, i] = jnp.sum(tot, axis=-1)


def _page_specs(n, first_page):
    def spec(i):
        return pl.BlockSpec((1, N_HEADS, HEAD_DIM, PAGE_SIZE),
                            lambda *a: (a[-1][first_page(*a[:-1]) + i], 0, 0, 0))
    return [spec(i) for i in range(n)]


def _softmax_group_step(tiles, v_tiles, state):
    m, acc = state
    m_new = m
    for s, keep, shift in tiles:
        bm = jnp.max(s, axis=0, keepdims=True) + shift
        m_new = jnp.maximum(m_new, bm if keep is None else jnp.where(keep > 0.0, bm, NEG_INF))
    acc = jnp.exp2(m - m_new) * acc
    for (s, keep, shift), v_t in zip(tiles, v_tiles):
        off = m_new - shift
        if keep is not None:
            off = jnp.where(keep > 0.0, off, jnp.inf)
        p = jnp.exp2(s - off)
        acc = acc + jnp.dot(v_t, p.astype(BF16), preferred_element_type=F32)
    return m_new, acc


def _moba_prompt_kernel(*refs, nblk, pps):
    if pps:
        refs = refs[1:]
    rb_ref, qt_ref, kb_ref, vt_ref, ks_ref, bkt_ref = refs[:6]
    pages = refs[6:6 + pps]
    o_ref = refs[6 + pps]
    sel_ref, bias_ref, sa_ref, sb_ref = refs[-4:]
    if pps:
        _block_sums(pages, refs[7 + pps])
    slab = pl.program_id(0)
    qi = pl.program_id(1)
    blk = MOBA_BLOCK
    heads = range(ATTN_HEADS)

    @pl.when(qi == 0)
    def _build_bias():
        for hh in heads:
            for kind in range(2):
                bias_ref[hh, kind] = LOG2_E * _bias_from_buckets(bkt_ref[kind], rb_ref,
                                                                 slab * ATTN_HEADS + hh)

    qt = qt_ref[...].astype(F32)
    row_head = lax.broadcasted_iota(I32, (ATTN_WIDTH, blk), 0) // HEAD_DIM
    n_iota = lax.broadcasted_iota(I32, (nblk, blk), 0)
    kmean = (ks_ref[...] * (1.0 / blk)).astype(BF16)
    far_bias = [LOG2_E * rb_ref[REL_BUCKETS - 1, slab * ATTN_HEADS + hh] for hh in heads]
    qz = [jnp.where(row_head == hh, qt, 0.0).astype(BF16) for hh in heads]
    for hh in heads:
        scores = jnp.dot(kmean, qz[hh], preferred_element_type=F32)
        scores = jnp.where(n_iota < qi, scores, NEG_INF)
        sel = jnp.zeros((nblk, blk), F32)
        for idx, m in _top3_rows(scores, n_iota, nblk):
            sel = jnp.where((n_iota == idx) & (m > NEG_INF), 1.0, sel)
        sel_ref[hh] = sel

    def fill(buf, base):
        for j in range(KV_GROUP):
            nc = jnp.maximum(base + j, 0)
            k_n = kb_ref[pl.ds(pl.multiple_of(nc * blk, blk), blk), :]
            for hh in heads:
                buf[hh * KV_GROUP + j] = jnp.dot(k_n, qz[hh], preferred_element_type=F32)

    def consume(buf, base, kinds, states):
        out = []
        for hh in heads:
            tiles, v_tiles = [], []
            for j, kind in enumerate(kinds):
                n = base + j
                nc = jnp.maximum(n, 0)
                s = buf[hh * KV_GROUP + j]
                v_tiles.append(jnp.concatenate([vt_ref[nc, hh * HEAD_DIM:(hh + 1) * HEAD_DIM, :], ones], axis=0))
                if kind == "own":
                    tiles.append((s + bias_ref[hh, 0], None, 0.0))
                    continue
                keep = jnp.where(n >= 0, sel_ref[hh, pl.ds(nc, 1), :], 0.0)
                if kind == "prev":
                    tiles.append((s + bias_ref[hh, 1], keep, 0.0))
                else:
                    tiles.append((s, keep, far_bias[hh]))
            out.append(_softmax_group_step(tiles, v_tiles, states[hh]))
        return tuple(out)

    near_base = qi - (KV_GROUP - 1)
    near_kinds = ["far"] * (KV_GROUP - 2) + ["prev", "own"]
    far_kinds = ["far"] * KV_GROUP

    def far_base(g):
        return near_base - KV_GROUP * (g + 1)

    ones = jnp.ones((SUM_ROWS, blk), BF16)
    init = tuple((jnp.full((1, blk), NEG_INF, F32), jnp.zeros((HEAD_DIM + SUM_ROWS, blk), F32))
                 for _ in heads)
    fill(sa_ref, near_base)
    fill(sb_ref, far_base(0))
    states = consume(sa_ref, near_base, near_kinds, init)

    def far_pair(i, states):
        g = 2 * i
        fill(sa_ref, far_base(g + 1))
        states = consume(sb_ref, far_base(g), far_kinds, states)
        fill(sb_ref, far_base(g + 2))
        return consume(sa_ref, far_base(g + 1), far_kinds, states)

    n_far_groups = (jnp.maximum(near_base, 0) + KV_GROUP - 1) // KV_GROUP
    states = lax.fori_loop(0, (n_far_groups + 1) // 2, far_pair, states)
    outs = [acc[:HEAD_DIM] / acc[HEAD_DIM:HEAD_DIM + 1] for _, acc in states]
    o_ref[...] = jnp.concatenate(outs, axis=0).T.astype(BF16)


def _prompt_bucket_maps():
    kk = jnp.arange(MOBA_BLOCK, dtype=I32)[:, None]
    tt = jnp.arange(MOBA_BLOCK, dtype=I32)[None, :]
    own = jnp.where(tt - kk >= 0, _rel_bucket(tt - kk), -1)
    prev = _rel_bucket(MOBA_BLOCK + tt - kk)
    return jnp.stack([own, prev]).astype(I32)


def _moba_prompt(qt, kb, vt, ksum, rel_bias, cache=None, page_table=None):
    s = kb.shape[0]
    nblk = s // MOBA_BLOCK
    grid = (N_HEADS // ATTN_HEADS, nblk)
    n_steps = grid[0] * grid[1]
    pps = 0
    if cache is not None and page_table.size % (n_steps * PAGES_PER_BLOCK) == 0:
        pps = page_table.size // n_steps
        if page_table.shape[1] % pps:
            pps = 0

    def imap(f):
        return (lambda sl, qi, pt: f(sl, qi)) if pps else f

    in_specs = [_smem(),
                pl.BlockSpec((ATTN_WIDTH, MOBA_BLOCK), imap(lambda sl, qi: (sl, qi))),
                pl.BlockSpec((s, ATTN_WIDTH), imap(lambda sl, qi: (0, sl)), pipeline_mode=pl.Buffered(1)),
                pl.BlockSpec((nblk, ATTN_WIDTH, MOBA_BLOCK), imap(lambda sl, qi: (0, sl, 0)),
                             pipeline_mode=pl.Buffered(1)),
                pl.BlockSpec((nblk, ATTN_WIDTH), imap(lambda sl, qi: (0, sl))),
                pl.BlockSpec((2, MOBA_BLOCK, MOBA_BLOCK), imap(lambda sl, qi: (0, 0, 0)))]
    out_specs = [pl.BlockSpec((MOBA_BLOCK, ATTN_WIDTH), imap(lambda sl, qi: (qi, sl)))]
    out_shape = [jax.ShapeDtypeStruct((s, D_MODEL), BF16)]
    args = [rel_bias, qt, kb, vt, ksum, _prompt_bucket_maps()]
    if pps:
        bps = pps // PAGES_PER_BLOCK
        in_specs += _page_specs(pps, lambda sl, qi: (sl * nblk + qi) * pps)
        out_specs.append(pl.BlockSpec((1, bps, N_HEADS, HEAD_DIM), lambda sl, qi, pt: (sl * nblk + qi, 0, 0, 0)))
        out_shape.append(jax.ShapeDtypeStruct((n_steps, bps, N_HEADS, HEAD_DIM), F32))
        args = [page_table.reshape(-1)] + args + [cache] * pps
    outs = pl.pallas_call(
        functools.partial(_moba_prompt_kernel, nblk=nblk, pps=pps),
        grid_spec=pltpu.PrefetchScalarGridSpec(
            num_scalar_prefetch=1 if pps else 0,
            grid=grid,
            in_specs=in_specs,
            out_specs=out_specs,
            scratch_shapes=[pltpu.VMEM((ATTN_HEADS, nblk, MOBA_BLOCK), F32),
                            pltpu.VMEM((ATTN_HEADS, 2, MOBA_BLOCK, MOBA_BLOCK), F32),
                            pltpu.VMEM((ATTN_HEADS * KV_GROUP, MOBA_BLOCK, MOBA_BLOCK), F32),
                            pltpu.VMEM((ATTN_HEADS * KV_GROUP, MOBA_BLOCK, MOBA_BLOCK), F32)],
        ),
        out_shape=out_shape,
        compiler_params=_params(2),
        name="moba_prompt",
    )(*args)
    if not pps:
        return outs[0], None
    b, npg = page_table.shape
    return outs[0], outs[1].reshape(b, npg // PAGES_PER_BLOCK, D_MODEL)


def _page_sum_kernel(pt_ref, *refs, pps):
    del pt_ref
    _block_sums(refs[:pps], refs[pps])


def _page_sums(cache, page_table):
    b, npg = page_table.shape
    pps = min(16, npg)
    bps = pps // PAGES_PER_BLOCK
    sums = pl.pallas_call(
        functools.partial(_page_sum_kernel, pps=pps),
        grid_spec=pltpu.PrefetchScalarGridSpec(
            num_scalar_prefetch=1,
            grid=(b, npg // pps),
            in_specs=_page_specs(pps, lambda bi, j: bi * npg + j * pps),
            out_specs=pl.BlockSpec((1, bps, N_HEADS, HEAD_DIM),
                                   lambda bi, j, pt: (bi * (npg // pps) + j, 0, 0, 0)),
        ),
        out_shape=jax.ShapeDtypeStruct((b * (npg // pps), bps, N_HEADS, HEAD_DIM), F32),
        compiler_params=_params(2),
        name="page_sums",
    )(page_table.reshape(-1), *([cache] * pps))
    return sums.reshape(b, npg // PAGES_PER_BLOCK, D_MODEL)


def _sample_gate_kernel(q_ref, bs_ref, seg_ref, sel_ref, *, t_new, nblk):
    kmean = bs_ref[0] * (1.0 / MOBA_BLOCK)
    n_iota = lax.broadcasted_iota(I32, (nblk, LANES), 0)
    sel_ref[...] = jnp.zeros_like(sel_ref)
    for t in range(t_new):
        prod = kmean * q_ref[0, t:t + 1, :]
        scores = _select_matmul(prod, seg_ref[...])
        for r, (idx, _) in enumerate(_top3_rows(scores, n_iota, nblk)):
            sel_ref[0, t, r:r + 1, :] = idx


def _sample_gate(q, bsum):
    b, t_new, _ = q.shape
    nblk = bsum.shape[1]
    seg = (jnp.arange(D_MODEL, dtype=I32)[:, None] // HEAD_DIM
           == jnp.arange(LANES, dtype=I32)[None, :]).astype(BF16)
    sel = pl.pallas_call(
        functools.partial(_sample_gate_kernel, t_new=t_new, nblk=nblk),
        grid=(b,),
        in_specs=[pl.BlockSpec((1, t_new, D_MODEL), lambda i: (i, 0, 0)),
                  pl.BlockSpec((1, nblk, D_MODEL), lambda i: (i, 0, 0)),
                  _whole((D_MODEL, LANES))],
        out_specs=pl.BlockSpec((1, t_new, SUBLANES, LANES), lambda i: (i, 0, 0, 0)),
        out_shape=jax.ShapeDtypeStruct((b, t_new, SUBLANES, LANES), I32),
        compiler_params=_params(1),
        name="sample_gate",
    )(q, bsum, seg)
    return sel[:, :, :MOBA_TOPK, :N_HEADS]


SLABS_PER_HEAD = MOBA_TOPK * PAGES_PER_BLOCK


def _sample_attn_kernel(sel_ref, pt_ref, rb_ref, q_ref, kn_ref, vn_ref, pb_ref, ck_hbm, cv_hbm, o_ref,
                        kbuf, vbuf, sems, o_scr, *, t_new, npg):
    bi, t = pl.program_id(0), pl.program_id(1)
    step = bi * t_new + t
    n_steps = pl.num_programs(0) * t_new
    slot = lax.rem(step, 2)
    last_blk = npg // PAGES_PER_BLOCK - 1

    def selected_block(b_, t_, j, h):
        return sel_ref[((b_ * t_new + t_) * MOBA_TOPK + j) * N_HEADS + h]

    def slab_copies(b_, t_, slot_):
        copies = []
        for h in range(N_HEADS):
            for j in range(MOBA_TOPK):
                blk = selected_block(b_, t_, j, h)
                for pg in range(PAGES_PER_BLOCK):
                    page = pt_ref[b_ * npg + blk * PAGES_PER_BLOCK + pg]
                    i = j * PAGES_PER_BLOCK + pg
                    copies.append(pltpu.make_async_copy(ck_hbm.at[page, h], kbuf.at[slot_, h, i],
                                                        sems.at[0, slot_]))
                    copies.append(pltpu.make_async_copy(cv_hbm.at[page, h], vbuf.at[slot_, h, i],
                                                        sems.at[1, slot_]))
        return copies

    @pl.when(step == 0)
    def _first_fetch():
        for c in slab_copies(bi, t, slot):
            c.start()

    @pl.when(step + 1 < n_steps)
    def _prefetch_next():
        nxt = step + 1
        for c in slab_copies(lax.div(nxt, t_new), lax.rem(nxt, t_new), 1 - slot):
            c.start()

    for c in slab_copies(bi, t, slot):
        c.wait()

    row = lax.broadcasted_iota(I32, (SUBLANES, 1), 0)
    q = q_ref[0, 0] * (HEAD_DIM ** -0.5)
    for h in range(N_HEADS):
        cols = slice(h * HEAD_DIM, (h + 1) * HEAD_DIM)
        far_bias = rb_ref[REL_BUCKETS - 1, h]
        qh = q[:, cols]
        q8 = jnp.broadcast_to(qh, (SUBLANES, HEAD_DIM)).astype(BF16)
        l1 = []
        for j in range(MOBA_TOPK):
            blk_j = selected_block(bi, t, j, h)
            for pg in range(PAGES_PER_BLOCK):
                k_t = kbuf[slot, h, j * PAGES_PER_BLOCK + pg].astype(BF16)
                near = _bias_from_buckets(pb_ref[pl.ds(t * PAGES_PER_BLOCK + pg, 1), :], rb_ref, h)
                l1.append(jnp.dot(q8, k_t, preferred_element_type=F32)[0:1]
                          + jnp.where(blk_j == last_blk, near, far_bias))
        l1 = jnp.concatenate(l1, axis=1)
        l2 = jnp.sum(kn_ref[0, :, cols] * qh, axis=-1, keepdims=True)
        bias2 = jnp.zeros((SUBLANES, 1), F32)
        for u in range(t_new):
            bias2 = jnp.where(row == u, rb_ref[jnp.maximum(t - u, 0), h], bias2)
        l2 = jnp.where(row <= t, l2 + bias2, NEG_INF)
        m = jnp.maximum(jnp.max(l1, axis=-1, keepdims=True), jnp.max(l2, axis=0, keepdims=True))
        p1 = jnp.exp(l1 - m)
        p2 = jnp.exp(l2 - m)
        denom = jnp.sum(p1, axis=-1, keepdims=True) + jnp.sum(p2, axis=0, keepdims=True)
        p8 = jnp.broadcast_to(p1, (SUBLANES, SLABS_PER_HEAD * PAGE_SIZE)).astype(BF16)
        o = jnp.sum(p2 * vn_ref[0, :, cols], axis=0, keepdims=True)
        for i in range(SLABS_PER_HEAD):
            v_t = vbuf[slot, h, i].astype(BF16)
            o = o + lax.dot_general(p8[:, i * PAGE_SIZE:(i + 1) * PAGE_SIZE], v_t, (((1,), (1,)), ((), ())),
                                    preferred_element_type=F32)[0:1]
        o_scr[:, cols] = jnp.broadcast_to(o / denom, (SUBLANES, HEAD_DIM))
    o_ref[0, 0] = o_scr[0:1, :]


def _sample_attn(q, k_new, v_new, cache_k, cache_v, sel, page_table, rel_bias):
    b, t_new, _ = q.shape
    npg = page_table.shape[1]
    pad = ((0, 0), (0, SUBLANES - t_new), (0, 0))
    tt = jnp.arange(t_new, dtype=I32)[:, None, None]
    pg = jnp.arange(PAGES_PER_BLOCK, dtype=I32)[None, :, None]
    rr = jnp.arange(PAGE_SIZE, dtype=I32)[None, None, :]
    near_bkt = _rel_bucket(MOBA_BLOCK + tt - pg * PAGE_SIZE - rr).reshape(t_new * PAGES_PER_BLOCK, PAGE_SIZE)

    new_spec = pl.BlockSpec((1, SUBLANES, D_MODEL), lambda bi, t, *_: (bi, 0, 0))
    tok_spec = pl.BlockSpec((1, 1, 1, D_MODEL), lambda bi, t, *_: (bi, t, 0, 0))
    slab_buf = pltpu.VMEM((2, N_HEADS, SLABS_PER_HEAD, HEAD_DIM, PAGE_SIZE), F32)
    out = pl.pallas_call(
        functools.partial(_sample_attn_kernel, t_new=t_new, npg=npg),
        grid_spec=pltpu.PrefetchScalarGridSpec(
            num_scalar_prefetch=2,
            grid=(b, t_new),
            in_specs=[_smem(), tok_spec, new_spec, new_spec,
                      pl.BlockSpec((t_new * PAGES_PER_BLOCK, PAGE_SIZE), lambda *_: (0, 0)),
                      pl.BlockSpec(memory_space=pl.ANY), pl.BlockSpec(memory_space=pl.ANY)],
            out_specs=tok_spec,
            scratch_shapes=[slab_buf, slab_buf, pltpu.SemaphoreType.DMA((2, 2)),
                            pltpu.VMEM((SUBLANES, D_MODEL), F32)],
        ),
        out_shape=jax.ShapeDtypeStruct((b, t_new, 1, D_MODEL), F32),
        compiler_params=_params(2),
        name="sample_attn",
    )(sel.reshape(-1), page_table.reshape(-1), rel_bias,
      q.reshape(b, t_new, 1, D_MODEL), jnp.pad(k_new, pad), jnp.pad(v_new, pad), near_bkt,
      cache_k, cache_v)
    return out.reshape(b, t_new, D_MODEL)


def _ssd_chunk(t, ext_ref, p_ref, st_ref, gy_ref, cw_ref, cb_ref, dtb_ref, alog_ref, dsk_ref,
               nw_ref, e_ref, wout_ref, n_valid):
    def conv_act(lo, width):
        acc = cb_ref[:, lo:lo + width]
        for j in range(CONV_WIDTH):
            acc = acc + cw_ref[j:j + 1, lo:lo + width] * ext_ref[5 + j:5 + j + t, lo:lo + width]
        return _silu(acc)

    dt = _softplus(p_ref[:, DT_OFF:DT_OFF + LANES] + dtb_ref[...])
    if n_valid < t:
        dt = jnp.where(lax.broadcasted_iota(I32, (t, LANES), 0) < n_valid, dt, 0.0)
    a = dt * -jnp.exp(alog_ref[...])
    r_i = lax.broadcasted_iota(I32, (t, t), 0)
    c_i = lax.broadcasted_iota(I32, (t, t), 1)
    causal = r_i >= c_i
    acs = jnp.dot(causal.astype(F32), a, preferred_element_type=F32,
                  precision=lax.Precision.HIGHEST)
    if t % LANES == 0:
        acs_t = acs.T
    else:
        acs_t = lax.dot_general(jnp.eye(LANES, dtype=F32), acs, (((1,), (1,)), ((), ())),
                                preferred_element_type=F32, precision=lax.Precision.HIGHEST)
    lane_hi = lax.broadcasted_iota(I32, (t, LANES), 1) >= SSM_HEAD_DIM

    for g in range(SSM_GROUPS):
        ch = slice(g * GROUP_DIM, (g + 1) * GROUP_DIM)
        e_g = e_ref[:, ch]
        xs = conv_act(g * GROUP_DIM, GROUP_DIM)
        b_g = conv_act(D_INNER + g * SSM_STATE, SSM_STATE).astype(BF16)
        c_g = conv_act(D_INNER + (SSM_GROUPS + g) * SSM_STATE, SSM_STATE).astype(BF16)
        dt_g = _select_matmul(dt, e_g)
        acs_g = _select_matmul(acs, e_g)
        last_g = acs_g[t - 1:t, :]
        xdt = xs * dt_g
        xdt_b = xdt.astype(BF16)
        xdt_dec = (xdt * jnp.exp(last_g - acs_g)).astype(BF16)
        cb = lax.dot_general(c_g, b_g, (((1,), (1,)), ((), ())), preferred_element_type=F32)
        st_in = st_ref[:, ch]
        y_off = jnp.dot(c_g, st_in.astype(BF16), preferred_element_type=F32) * jnp.exp(acs_g)
        st_new = lax.dot_general(b_g, xdt_dec, (((0,), (0,)), ((), ())), preferred_element_type=F32)
        st_ref[:, ch] = st_in * jnp.exp(last_g) + st_new
        for j in range(HEADS_PER_GROUP // HEADS_PER_SLAB):
            cols = slice(j * LANES, (j + 1) * LANES)
            pair = xdt_b[:, cols]
            ys = []
            for hh in range(HEADS_PER_SLAB):
                h = g * HEADS_PER_GROUP + j * HEADS_PER_SLAB + hh
                decay = jnp.exp(jnp.where(causal, acs[:, h:h + 1] - acs_t[h:h + 1, :], NEG_INF))
                ys.append(jnp.dot((cb * decay).astype(BF16), pair, preferred_element_type=F32))
            y_diag = jnp.where(lane_hi, ys[1], ys[0])
            lo = g * GROUP_DIM + j * LANES
            y = y_diag + y_off[:, cols] + dsk_ref[:, lo:lo + LANES] * xs[:, cols]
            gy_ref[:, lo:lo + LANES] = y * _silu(p_ref[:, lo:lo + LANES])
        gy = gy_ref[:, ch]
        ms = jnp.mean(gy * gy, axis=-1, keepdims=True)
        gy_ref[:, ch] = gy * lax.rsqrt(ms + RMS_EPS) * nw_ref[:, ch]
    return jnp.dot(gy_ref[...].astype(BF16), wout_ref[...], preferred_element_type=F32)


def _ssm_prompt_kernel(p_ref, xres_ref, cw_ref, cb_ref, dtb_ref, alog_ref, dsk_ref, nw_ref, e_ref,
                       wout_ref, y_ref, st_out_ref, cv_out_ref, ext_ref, st_ref, gy_ref):
    c = pl.program_id(0)
    t = SSD_CHUNK

    @pl.when(c == 0)
    def _init():
        ext_ref[t:t + SUBLANES, :] = jnp.zeros((SUBLANES, CONV_DIM), F32)
        st_ref[...] = jnp.zeros_like(st_ref)

    ext_ref[0:SUBLANES, :] = ext_ref[t:t + SUBLANES, :]
    ext_ref[SUBLANES:SUBLANES + t, :] = p_ref[:, D_INNER:D_INNER + CONV_DIM]
    out = _ssd_chunk(t, ext_ref, p_ref, st_ref, gy_ref, cw_ref, cb_ref, dtb_ref, alog_ref, dsk_ref,
                     nw_ref, e_ref, wout_ref, t)
    y_ref[...] = xres_ref[...] + out

    @pl.when(c == pl.num_programs(0) - 1)
    def _finish():
        st_out_ref[...] = st_ref[...].T
        cv_out_ref[...] = ext_ref[t:t + SUBLANES, :]


def _ssm_consts(w):
    conv_w, conv_b, dt_bias, a_log, d_skip, norm_w, w_out = w
    lane_pad = (0, LANES - SSM_HEADS)
    expand = (jnp.arange(LANES, dtype=I32)[:, None]
              == jnp.arange(D_INNER, dtype=I32)[None, :] // SSM_HEAD_DIM).astype(BF16)
    return (conv_w, conv_b.reshape(1, CONV_DIM),
            jnp.pad(dt_bias, lane_pad).reshape(1, LANES), jnp.pad(a_log, lane_pad).reshape(1, LANES),
            jnp.repeat(d_skip, SSM_HEAD_DIM).reshape(1, D_INNER), norm_w.reshape(1, D_INNER),
            expand, w_out)


def _const_specs(consts):
    return [_whole(c.shape) for c in consts]


def _ssm_prompt(proj, xres, w):
    s = proj.shape[0]
    t = SSD_CHUNK
    consts = _ssm_consts(w)
    y, st, cv = pl.pallas_call(
        _ssm_prompt_kernel,
        grid=(s // t,),
        in_specs=[pl.BlockSpec((t, SSM_PROJ_PAD), lambda c: (c, 0)),
                  pl.BlockSpec((t, D_MODEL), lambda c: (c, 0))] + _const_specs(consts),
        out_specs=[pl.BlockSpec((t, D_MODEL), lambda c: (c, 0)),
                   _whole((D_INNER, SSM_STATE)), _whole((SUBLANES, CONV_DIM))],
        out_shape=[jax.ShapeDtypeStruct((s, D_MODEL), F32),
                   jax.ShapeDtypeStruct((D_INNER, SSM_STATE), F32),
                   jax.ShapeDtypeStruct((SUBLANES, CONV_DIM), F32)],
        scratch_shapes=[pltpu.VMEM((t + 2 * SUBLANES, CONV_DIM), F32),
                        pltpu.VMEM((SSM_STATE, D_INNER), F32),
                        pltpu.VMEM((t, D_INNER), F32)],
        compiler_params=_params(1),
        name="ssm_prompt",
    )(proj, xres, *consts)
    return y, st, cv[SUBLANES - (CONV_WIDTH - 1):]


def _ssm_sample_kernel(p_ref, xres_ref, cst_ref, st_in_ref, cw_ref, cb_ref, dtb_ref, alog_ref,
                       dsk_ref, nw_ref, e_ref, wout_ref, y_ref, st_out_ref, cv_out_ref,
                       ext_ref, st_ref, gy_ref, *, t_new):
    t = SAMPLE_T
    ext_ref[0:SUBLANES, :] = cst_ref[0]
    ext_ref[SUBLANES:SUBLANES + t, :] = p_ref[0, :, D_INNER:D_INNER + CONV_DIM]
    st_ref[...] = st_in_ref[0].T
    out = _ssd_chunk(t, ext_ref, p_ref.at[0], st_ref, gy_ref, cw_ref, cb_ref, dtb_ref, alog_ref,
                     dsk_ref, nw_ref, e_ref, wout_ref, t_new)
    y_ref[0] = xres_ref[0] + out[0:t_new]
    st_out_ref[0] = st_ref[...].T
    cv_out_ref[0] = ext_ref[SUBLANES:SUBLANES + t, :]


def _ssm_sample(proj, xres, conv_state, ssm_state, w):
    b, t_new, _ = proj.shape
    t = SAMPLE_T
    consts = _ssm_consts(w)
    proj8 = jnp.pad(proj, ((0, 0), (0, t - t_new), (0, 0)))
    cst8 = jnp.pad(conv_state, ((0, 0), (SUBLANES - (CONV_WIDTH - 1), 0), (0, 0)))
    y, st, cv = pl.pallas_call(
        functools.partial(_ssm_sample_kernel, t_new=t_new),
        grid=(b,),
        in_specs=[pl.BlockSpec((1, t, SSM_PROJ_PAD), lambda i: (i, 0, 0)),
                  pl.BlockSpec((1, t_new, D_MODEL), lambda i: (i, 0, 0)),
                  pl.BlockSpec((1, SUBLANES, CONV_DIM), lambda i: (i, 0, 0)),
                  pl.BlockSpec((1, D_INNER, SSM_STATE), lambda i: (i, 0, 0))] + _const_specs(consts),
        out_specs=[pl.BlockSpec((1, t_new, D_MODEL), lambda i: (i, 0, 0)),
                   pl.BlockSpec((1, D_INNER, SSM_STATE), lambda i: (i, 0, 0)),
                   pl.BlockSpec((1, t, CONV_DIM), lambda i: (i, 0, 0))],
        out_shape=[jax.ShapeDtypeStruct((b, t_new, D_MODEL), F32),
                   jax.ShapeDtypeStruct((b, D_INNER, SSM_STATE), F32),
                   jax.ShapeDtypeStruct((b, t, CONV_DIM), F32)],
        scratch_shapes=[pltpu.VMEM((2 * SUBLANES, CONV_DIM), F32),
                        pltpu.VMEM((SSM_STATE, D_INNER), F32),
                        pltpu.VMEM((t, D_INNER), F32)],
        compiler_params=_params(1),
        name="ssm_sample",
    )(proj8, xres, cst8, ssm_state, *consts)
    return y, st, cv[:, t_new - (CONV_WIDTH - 1):t_new]


def _ffn_weights(w_in, w_out):
    def chunks(w):
        return w.reshape(D_MODEL, N_FF_CHUNKS, FF_CHUNK).transpose(1, 0, 2).astype(BF16)
    return (chunks(w_in[:, :D_FF]), chunks(w_in[:, D_FF:]),
            w_out.reshape(N_FF_CHUNKS, FF_CHUNK, D_MODEL).astype(BF16))


def _row_tile(rows, want=512):
    return want if rows % want == 0 else 128


def kernel(x_prompt, x_sample, cache_k, cache_v, page_table, state_ssm, state_conv, ffn1_norm, ffn1_w_in, ffn1_w_out, mix_norm, ffn2_norm, ffn2_w_in, ffn2_w_out, attn_w_qkv, attn_w_o, rel_bias, ssm_w_in, ssm_conv_w, ssm_conv_b, ssm_dt_bias, ssm_A_log, ssm_D, ssm_norm, ssm_w_out, final_norm):
    bp, s, _ = x_prompt.shape
    bs, t_new, _ = x_sample.shape
    assert bp == 1 and attn_w_qkv.shape[0] == 1 and ssm_w_in.shape[0] == 1
    xp = x_prompt.reshape(s, D_MODEL)
    xs = x_sample.reshape(bs * t_new, D_MODEL)
    tp, ts = _row_tile(s), _row_tile(bs * t_new)
    tf = _row_tile(s, 1024)

    w1 = _ffn_weights(ffn1_w_in[0], ffn1_w_out[0])
    xp = _ffn(xp, ffn1_norm[0], w1, tf)
    xs = _ffn(xs, ffn1_norm[0], w1, ts)

    wqkv = attn_w_qkv[0].astype(BF16)
    wq, wk, wv = (wqkv[:, i * D_MODEL:(i + 1) * D_MODEL] for i in range(3))
    wo = attn_w_o[0].astype(BF16)
    ck = jnp.transpose(cache_k[0], (0, 2, 3, 1))
    cv = jnp.transpose(cache_v[0], (0, 2, 3, 1))
    k_p, v_p, kb, qt, vt, ksum = _qkv_prompt(xp, mix_norm[0], wq.T, wk, wk.T, wv.T, tp)
    o_p, bsum = _moba_prompt(qt, kb, vt, ksum.reshape(s // MOBA_BLOCK, D_MODEL), rel_bias, ck, page_table)
    if bsum is None:
        bsum = _page_sums(ck, page_table)
    xp = _matres(o_p, wo, xp, tp)

    qkv_s = _proj(xs, mix_norm[0], wqkv, ts).reshape(bs, t_new, 3 * D_MODEL)
    q_s, k_s, v_s = (qkv_s[..., i * D_MODEL:(i + 1) * D_MODEL] for i in range(3))
    sel = _sample_gate(q_s, bsum)
    o_s = _sample_attn(q_s, k_s, v_s, ck, cv, sel, page_table, rel_bias)
    xs = _matres(o_s.reshape(bs * t_new, D_MODEL), wo, xs, ts)

    w2 = _ffn_weights(ffn2_w_in[0], ffn2_w_out[0])
    xp = _ffn(xp, ffn2_norm[0], w2, tf)
    xs = _ffn(xs, ffn2_norm[0], w2, ts)

    w1 = _ffn_weights(ffn1_w_in[1], ffn1_w_out[1])
    xp = _ffn(xp, ffn1_norm[1], w1, tf)
    xs = _ffn(xs, ffn1_norm[1], w1, ts)

    w_in = jnp.pad(ssm_w_in[0], ((0, 0), (0, SSM_PROJ_PAD - SSM_IN_DIM))).astype(BF16)
    ssm_w = (ssm_conv_w[0], ssm_conv_b[0], ssm_dt_bias[0], ssm_A_log[0], ssm_D[0], ssm_norm[0],
             ssm_w_out[0].astype(BF16))
    xp, st_p, cv_p = _ssm_prompt(_proj(xp, mix_norm[1], w_in, tp), xp, ssm_w)
    proj_s = _proj(xs, mix_norm[1], w_in, ts).reshape(bs, t_new, SSM_PROJ_PAD)
    xs3, st_s, cv_s = _ssm_sample(proj_s, xs.reshape(bs, t_new, D_MODEL), state_conv[0],
                                  state_ssm[0].reshape(bs, D_INNER, SSM_STATE), ssm_w)
    xs = xs3.reshape(bs * t_new, D_MODEL)

    w2 = _ffn_weights(ffn2_w_in[1], ffn2_w_out[1])
    y_p = _ffn(xp, ffn2_norm[1], w2, tf, final_g=final_norm)
    y_s = _ffn(xs, ffn2_norm[1], w2, ts, final_g=final_norm)

    head_shape = (N_HEADS, HEAD_DIM)
    state_shape = (SSM_HEADS, SSM_HEAD_DIM, SSM_STATE)
    return (y_p.reshape(1, s, D_MODEL), y_s.reshape(bs, t_new, D_MODEL),
            jnp.transpose(k_p, (0, 3, 1, 2))[None, None], jnp.transpose(v_p, (0, 3, 1, 2))[None, None],
            k_s.reshape(1, bs, t_new, *head_shape), v_s.reshape(1, bs, t_new, *head_shape),
            st_p.reshape(1, 1, *state_shape), cv_p.reshape(1, 1, CONV_WIDTH - 1, CONV_DIM),
            st_s.reshape(1, bs, *state_shape), cv_s.reshape(1, bs, CONV_WIDTH - 1, CONV_DIM))
```

```python
import functools
import math

import jax
import jax.numpy as jnp
from jax import lax
from jax.experimental import pallas as pl
from jax.experimental.pallas import tpu as pltpu

F32 = jnp.float32
BF16 = jnp.bfloat16
I32 = jnp.int32
NEG_INF = float("-inf")
LOG2_E = math.log2(math.e)

D_MODEL = 1024
N_HEADS = 16
HEAD_DIM = 64
MOBA_BLOCK = 256
MOBA_TOPK = 3
PAGE_SIZE = 128
REL_BUCKETS = 32
REL_MAX_DIST = 128
D_INNER = 2048
SSM_HEADS = 32
SSM_HEAD_DIM = 64
SSM_GROUPS = 4
SSM_STATE = 128
CONV_WIDTH = 4
CONV_DIM = D_INNER + 2 * SSM_GROUPS * SSM_STATE
SSM_IN_DIM = 2 * D_INNER + 2 * SSM_GROUPS * SSM_STATE + SSM_HEADS
SSD_CHUNK = 256
D_FF = 2816
RMS_EPS = 1e-6

LANES = 128
SUBLANES = 8
VMEM_LIMIT_BYTES = 56 * 1024 * 1024

FF_CHUNK = 256
N_FF_CHUNKS = D_FF // FF_CHUNK
HEADS_PER_SLAB = LANES // HEAD_DIM
N_SLABS = N_HEADS // HEADS_PER_SLAB
GROUP_DIM = D_INNER // SSM_GROUPS
HEADS_PER_GROUP = SSM_HEADS // SSM_GROUPS
SSM_PROJ_PAD = 5248
DT_OFF = D_INNER + CONV_DIM
SAMPLE_T = 8
PAGES_PER_BLOCK = MOBA_BLOCK // PAGE_SIZE
KV_GROUP = 4
ATTN_HEADS = 4
ATTN_WIDTH = ATTN_HEADS * HEAD_DIM
SUM_ROWS = 16


def _params(n_axes, vmem=VMEM_LIMIT_BYTES):
    return pltpu.CompilerParams(dimension_semantics=("arbitrary",) * n_axes,
                                vmem_limit_bytes=vmem)


def _whole(shape):
    nd = len(shape)
    return pl.BlockSpec(shape, lambda *_: (0,) * nd)


def _smem():
    return pl.BlockSpec(memory_space=pltpu.SMEM)


def _rms(x, g):
    ms = jnp.mean(x * x, axis=-1, keepdims=True)
    return x * lax.rsqrt(ms + RMS_EPS) * g


def _silu(x):
    return x * jax.nn.sigmoid(x)


def _softplus(x):
    return jnp.maximum(x, 0.0) + jnp.log1p(jnp.exp(-jnp.abs(x)))


def _split3(x):
    hi = x.astype(BF16)
    r1 = x - hi.astype(F32)
    mid = r1.astype(BF16)
    lo = (r1 - mid.astype(F32)).astype(BF16)
    return hi, mid, lo


def _select_matmul(x, sel):
    hi, mid, lo = _split3(x)
    dot = functools.partial(jnp.dot, preferred_element_type=F32)
    return dot(hi, sel) + dot(mid, sel) + dot(lo, sel)


def _top3_rows(s, n_iota, sentinel):
    picks = []
    for _ in range(MOBA_TOPK):
        m = jnp.max(s, axis=0, keepdims=True)
        idx = jnp.min(jnp.where(s == m, n_iota, sentinel), axis=0, keepdims=True)
        picks.append((idx, m))
        s = jnp.where(n_iota == idx, NEG_INF, s)
    return picks


def _bias_from_buckets(bk, rb_ref, h):
    out = jnp.full(bk.shape, NEG_INF, F32)
    for b in range(REL_BUCKETS):
        out = jnp.where(bk == b, rb_ref[b, h], out)
    return out


def _rel_bucket(dist):
    n = jnp.maximum(dist, 0)
    max_exact = REL_BUCKETS // 2
    nf = jnp.maximum(n, 1).astype(F32)
    large = max_exact + (jnp.log(nf / max_exact) / math.log(REL_MAX_DIST / max_exact)
                         * (REL_BUCKETS - max_exact)).astype(I32)
    large = jnp.minimum(large, REL_BUCKETS - 1)
    return jnp.where(n < max_exact, n, large).astype(I32)


def _ffn_kernel(*refs, final):
    if final:
        x_ref, g_ref, wa_ref, wb_ref, wo_ref, fg_ref, o_ref, xn_ref, acc_ref = refs
    else:
        x_ref, g_ref, wa_ref, wb_ref, wo_ref, o_ref, xn_ref, acc_ref = refs
    xn_ref[...] = _rms(x_ref[...], g_ref[...]).astype(BF16)
    acc_ref[...] = jnp.zeros_like(acc_ref)

    def chunk(c, carry):
        xn = xn_ref[...]
        a = jnp.dot(xn, wa_ref[c], preferred_element_type=F32)
        b = jnp.dot(xn, wb_ref[c], preferred_element_type=F32)
        h = (_silu(a) * b).astype(BF16)
        acc_ref[...] += jnp.dot(h, wo_ref[c], preferred_element_type=F32)
        return carry

    lax.fori_loop(0, N_FF_CHUNKS, chunk, 0)
    y = x_ref[...] + 0.5 * acc_ref[...]
    if final:
        y = _rms(y, fg_ref[...])
    o_ref[...] = y


def _ffn(x, g, w, tm, final_g=None):
    rows = x.shape[0]
    wa, wb, wo = w
    row_spec = pl.BlockSpec((tm, D_MODEL), lambda i: (i, 0))
    in_specs = [row_spec, _whole((1, D_MODEL)), _whole(wa.shape), _whole(wb.shape), _whole(wo.shape)]
    args = [x, g.reshape(1, D_MODEL), wa, wb, wo]
    if final_g is not None:
        in_specs.append(_whole((1, D_MODEL)))
        args.append(final_g.reshape(1, D_MODEL))
    return pl.pallas_call(
        functools.partial(_ffn_kernel, final=final_g is not None),
        grid=(rows // tm,),
        in_specs=in_specs,
        out_specs=row_spec,
        out_shape=jax.ShapeDtypeStruct((rows, D_MODEL), F32),
        scratch_shapes=[pltpu.VMEM((tm, D_MODEL), BF16), pltpu.VMEM((tm, D_MODEL), F32)],
        compiler_params=_params(1),
        name="ffn",
    )(*args)


def _proj_kernel(x_ref, g_ref, w_ref, o_ref, xn_ref, *, widths):
    xn_ref[...] = _rms(x_ref[...], g_ref[...]).astype(BF16)
    lo = 0
    for wd in widths:
        o_ref[:, lo:lo + wd] = jnp.dot(xn_ref[...], w_ref[:, lo:lo + wd], preferred_element_type=F32)
        lo += wd


def _proj(x, g, w, tm):
    rows, n = x.shape[0], w.shape[1]
    col_chunk = 4 * LANES
    widths = [col_chunk] * (n // col_chunk) + ([n % col_chunk] if n % col_chunk else [])
    return pl.pallas_call(
        functools.partial(_proj_kernel, widths=tuple(widths)),
        grid=(rows // tm,),
        in_specs=[pl.BlockSpec((tm, D_MODEL), lambda i: (i, 0)), _whole((1, D_MODEL)), _whole(w.shape)],
        out_specs=pl.BlockSpec((tm, n), lambda i: (i, 0)),
        out_shape=jax.ShapeDtypeStruct((rows, n), F32),
        scratch_shapes=[pltpu.VMEM((tm, D_MODEL), BF16)],
        compiler_params=_params(1),
        name="norm_proj",
    )(x, g.reshape(1, D_MODEL), w)


def _matres_kernel(a_ref, w_ref, r_ref, o_ref):
    o_ref[...] = r_ref[...] + jnp.dot(a_ref[...].astype(BF16), w_ref[...], preferred_element_type=F32)


def _matres(a, w, res, tm):
    rows, k = a.shape
    n = w.shape[1]
    return pl.pallas_call(
        _matres_kernel,
        grid=(rows // tm,),
        in_specs=[pl.BlockSpec((tm, k), lambda i: (i, 0)), _whole(w.shape),
                  pl.BlockSpec((tm, n), lambda i: (i, 0))],
        out_specs=pl.BlockSpec((tm, n), lambda i: (i, 0)),
        out_shape=jax.ShapeDtypeStruct((rows, n), F32),
        compiler_params=_params(1),
        name="matmul_residual",
    )(a, w, res)


def _qkv_prompt_kernel(x_ref, g_ref, wqt_ref, wk_ref, wkt_ref, wvt_ref,
                       kp_ref, vp_ref, kb_ref, qt_ref, vt_ref, ks_ref, xn_ref, *, tm):
    xn_ref[...] = _rms(x_ref[...], g_ref[...]).astype(BF16)
    xn = xn_ref[...]
    k = jnp.dot(xn, wk_ref[...], preferred_element_type=F32)
    kb_ref[...] = k.astype(BF16)
    for j in range(tm // MOBA_BLOCK):
        ks_ref[0, j:j + 1, :] = jnp.sum(k[j * MOBA_BLOCK:(j + 1) * MOBA_BLOCK], axis=0, keepdims=True)
    nt = (((1,), (1,)), ((), ()))
    qt = lax.dot_general(wqt_ref[...], xn, nt, preferred_element_type=F32)
    qt_ref[...] = (qt * (LOG2_E * HEAD_DIM ** -0.5)).astype(BF16)
    kt = lax.dot_general(wkt_ref[...], xn, nt, preferred_element_type=F32)
    vt = lax.dot_general(wvt_ref[...], xn, nt, preferred_element_type=F32)
    for j in range(tm // MOBA_BLOCK):
        vt_ref[j] = vt[:, j * MOBA_BLOCK:(j + 1) * MOBA_BLOCK].astype(BF16)
    for pg in range(tm // PAGE_SIZE):
        rows = slice(pg * PAGE_SIZE, (pg + 1) * PAGE_SIZE)
        kp_ref[pg] = kt[:, rows].reshape(N_HEADS, HEAD_DIM, PAGE_SIZE)
        vp_ref[pg] = vt[:, rows].reshape(N_HEADS, HEAD_DIM, PAGE_SIZE)


def _qkv_prompt(x, g, wqt, wk, wkt, wvt, tm):
    s = x.shape[0]
    nblk = s // MOBA_BLOCK
    bpt = tm // MOBA_BLOCK
    ppt = tm // PAGE_SIZE
    row_spec = pl.BlockSpec((tm, D_MODEL), lambda i: (i, 0))
    page_spec = pl.BlockSpec((ppt, N_HEADS, HEAD_DIM, PAGE_SIZE), lambda i: (i, 0, 0, 0))
    pages = jax.ShapeDtypeStruct((s // PAGE_SIZE, N_HEADS, HEAD_DIM, PAGE_SIZE), F32)
    sq = _whole((D_MODEL, D_MODEL))
    return pl.pallas_call(
        functools.partial(_qkv_prompt_kernel, tm=tm),
        grid=(s // tm,),
        in_specs=[row_spec, _whole((1, D_MODEL)), sq, sq, sq, sq],
        out_specs=[page_spec, page_spec, row_spec,
                   pl.BlockSpec((D_MODEL, tm), lambda i: (0, i)),
                   pl.BlockSpec((bpt, D_MODEL, MOBA_BLOCK), lambda i: (i, 0, 0)),
                   pl.BlockSpec((1, bpt, D_MODEL), lambda i: (i, 0, 0))],
        out_shape=[pages, pages,
                   jax.ShapeDtypeStruct((s, D_MODEL), BF16),
                   jax.ShapeDtypeStruct((D_MODEL, s), BF16),
                   jax.ShapeDtypeStruct((nblk, D_MODEL, MOBA_BLOCK), BF16),
                   jax.ShapeDtypeStruct((s // tm, bpt, D_MODEL), F32)],
        scratch_shapes=[pltpu.VMEM((tm, D_MODEL), BF16)],
        compiler_params=_params(1),
        name="qkv_prompt",
    )(x, g.reshape(1, D_MODEL), wqt, wk, wkt, wvt)


def _block_sums(pages, o_ref):
    for i in range(len(pages) // PAGES_PER_BLOCK):
        tot = pages[PAGES_PER_BLOCK * i][0]
        for j in range(1, PAGES_PER_BLOCK):
            tot = tot + pages[PAGES_PER_BLOCK * i + j][0]
        o_ref[0, i] = jnp.sum(tot, axis=-1)


def _page_specs(n, first_page):
    def spec(i):
        return pl.BlockSpec((1, N_HEADS, HEAD_DIM, PAGE_SIZE),
                            lambda *a: (a[-1][first_page(*a[:-1]) + i], 0, 0, 0))
    return [spec(i) for i in range(n)]


def _softmax_group_step(tiles, v_tiles, state):
    m, acc = state
    m_new = m
    for s, keep, shift in tiles:
        bm = jnp.max(s, axis=0, keepdims=True) + shift
        m_new = jnp.maximum(m_new, bm if keep is None else jnp.where(keep > 0.0, bm, NEG_INF))
    acc = jnp.exp2(m - m_new) * acc
    for (s, keep, shift), v_t in zip(tiles, v_tiles):
        off = m_new - shift
        if keep is not None:
            off = jnp.where(keep > 0.0, off, jnp.inf)
        p = jnp.exp2(s - off)
        acc = acc + jnp.dot(v_t, p.astype(BF16), preferred_element_type=F32)
    return m_new, acc


def _moba_prompt_kernel(*refs, nblk, pps):
    if pps:
        refs = refs[1:]
    rb_ref, qt_ref, kb_ref, vt_ref, ks_ref, bkt_ref = refs[:6]
    pages = refs[6:6 + pps]
    o_ref = refs[6 + pps]
    sel_ref, bias_ref, sa_ref, sb_ref = refs[-4:]
    slab = pl.program_id(0)
    qi = pl.program_id(1)
    blk = MOBA_BLOCK
    heads = range(ATTN_HEADS)

    @pl.when(qi == 0)
    def _build_bias():
        for hh in heads:
            for kind in range(2):
                bias_ref[hh, kind] = LOG2_E * _bias_from_buckets(bkt_ref[kind], rb_ref,
                                                                 slab * ATTN_HEADS + hh)

    if pps:
        _block_sums(pages, refs[7 + pps])
    qt = qt_ref[...].astype(F32)
    row_head = lax.broadcasted_iota(I32, (ATTN_WIDTH, blk), 0) // HEAD_DIM
    n_iota = lax.broadcasted_iota(I32, (nblk, blk), 0)
    kmean = (ks_ref[...] * (1.0 / blk)).astype(BF16)
    far_bias = [LOG2_E * rb_ref[REL_BUCKETS - 1, slab * ATTN_HEADS + hh] for hh in heads]
    qz = [jnp.where(row_head == hh, qt, 0.0).astype(BF16) for hh in heads]
    for hh in heads:
        scores = jnp.dot(kmean, qz[hh], preferred_element_type=F32)
        scores = jnp.where(n_iota < qi, scores, NEG_INF)
        sel = jnp.zeros((nblk, blk), F32)
        for idx, m in _top3_rows(scores, n_iota, nblk):
            sel = jnp.where((n_iota == idx) & (m > NEG_INF), 1.0, sel)
        sel_ref[hh] = sel

    def fill(buf, base):
        for j in range(KV_GROUP):
            nc = jnp.maximum(base + j, 0)
            k_n = kb_ref[pl.ds(pl.multiple_of(nc * blk, blk), blk), :]
            for hh in heads:
                buf[hh * KV_GROUP + j] = jnp.dot(k_n, qz[hh], preferred_element_type=F32)

    def consume(buf, base, kinds, states):
        out = []
        for hh in heads:
            tiles, v_tiles = [], []
            for j, kind in enumerate(kinds):
                n = base + j
                nc = jnp.maximum(n, 0)
                s = buf[hh * KV_GROUP + j]
                v_tiles.append(jnp.concatenate([vt_ref[nc, hh * HEAD_DIM:(hh + 1) * HEAD_DIM, :], ones], axis=0))
                if kind == "own":
                    tiles.append((s + bias_ref[hh, 0], None, 0.0))
                    continue
                keep = jnp.where(n >= 0, sel_ref[hh, pl.ds(nc, 1), :], 0.0)
                if kind == "prev":
                    tiles.append((s + bias_ref[hh, 1], keep, 0.0))
                else:
                    tiles.append((s, keep, far_bias[hh]))
            out.append(_softmax_group_step(tiles, v_tiles, states[hh]))
        return tuple(out)

    near_base = qi - (KV_GROUP - 1)
    near_kinds = ["far"] * (KV_GROUP - 2) + ["prev", "own"]
    far_kinds = ["far"] * KV_GROUP

    def far_base(g):
        return near_base - KV_GROUP * (g + 1)

    ones = jnp.ones((SUM_ROWS, blk), BF16)
    init = tuple((jnp.full((1, blk), NEG_INF, F32), jnp.zeros((HEAD_DIM + SUM_ROWS, blk), F32))
                 for _ in heads)
    fill(sa_ref, near_base)
    fill(sb_ref, far_base(0))
    states = consume(sa_ref, near_base, near_kinds, init)

    def far_pair(i, states):
        g = 2 * i
        fill(sa_ref, far_base(g + 1))
        states = consume(sb_ref, far_base(g), far_kinds, states)
        fill(sb_ref, far_base(g + 2))
        return consume(sa_ref, far_base(g + 1), far_kinds, states)

    n_far_groups = lax.div(jnp.maximum(near_base, 0) + KV_GROUP - 1, KV_GROUP)
    n_pairs = lax.div(n_far_groups, 2)
    states = lax.fori_loop(0, n_pairs, far_pair, states)
    states = lax.cond(lax.rem(n_far_groups, 2) == 1,
                      lambda st: consume(sb_ref, far_base(2 * n_pairs), far_kinds, st),
                      lambda st: st, states)
    outs = [acc[:HEAD_DIM] / acc[HEAD_DIM:HEAD_DIM + 1] for _, acc in states]
    o_ref[...] = jnp.concatenate(outs, axis=0).T.astype(BF16)


def _prompt_bucket_maps():
    kk = jnp.arange(MOBA_BLOCK, dtype=I32)[:, None]
    tt = jnp.arange(MOBA_BLOCK, dtype=I32)[None, :]
    own = jnp.where(tt - kk >= 0, _rel_bucket(tt - kk), -1)
    prev = _rel_bucket(MOBA_BLOCK + tt - kk)
    return jnp.stack([own, prev]).astype(I32)


def _moba_prompt(qt, kb, vt, ksum, rel_bias, cache=None, page_table=None):
    s = kb.shape[0]
    nblk = s // MOBA_BLOCK
    grid = (N_HEADS // ATTN_HEADS, nblk)
    n_steps = grid[0] * grid[1]
    pps = 0
    if cache is not None and page_table.size % (n_steps * PAGES_PER_BLOCK) == 0:
        pps = page_table.size // n_steps
        if page_table.shape[1] % pps:
            pps = 0

    def imap(f):
        return (lambda sl, qi, pt: f(sl, qi)) if pps else f

    in_specs = [_smem(),
                pl.BlockSpec((ATTN_WIDTH, MOBA_BLOCK), imap(lambda sl, qi: (sl, qi))),
                pl.BlockSpec((s, ATTN_WIDTH), imap(lambda sl, qi: (0, sl)), pipeline_mode=pl.Buffered(1)),
                pl.BlockSpec((nblk, ATTN_WIDTH, MOBA_BLOCK), imap(lambda sl, qi: (0, sl, 0)),
                             pipeline_mode=pl.Buffered(1)),
                pl.BlockSpec((nblk, ATTN_WIDTH), imap(lambda sl, qi: (0, sl))),
                pl.BlockSpec((2, MOBA_BLOCK, MOBA_BLOCK), imap(lambda sl, qi: (0, 0, 0)))]
    out_specs = [pl.BlockSpec((MOBA_BLOCK, ATTN_WIDTH), imap(lambda sl, qi: (qi, sl)))]
    out_shape = [jax.ShapeDtypeStruct((s, D_MODEL), BF16)]
    args = [rel_bias, qt, kb, vt, ksum, _prompt_bucket_maps()]
    if pps:
        bps = pps // PAGES_PER_BLOCK
        in_specs += _page_specs(pps, lambda sl, qi: (sl * nblk + qi) * pps)
        out_specs.append(pl.BlockSpec((1, bps, N_HEADS, HEAD_DIM), lambda sl, qi, pt: (sl * nblk + qi, 0, 0, 0)))
        out_shape.append(jax.ShapeDtypeStruct((n_steps, bps, N_HEADS, HEAD_DIM), F32))
        args = [page_table.reshape(-1)] + args + [cache] * pps
    outs = pl.pallas_call(
        functools.partial(_moba_prompt_kernel, nblk=nblk, pps=pps),
        grid_spec=pltpu.PrefetchScalarGridSpec(
            num_scalar_prefetch=1 if pps else 0,
            grid=grid,
            in_specs=in_specs,
            out_specs=out_specs,
            scratch_shapes=[pltpu.VMEM((ATTN_HEADS, nblk, MOBA_BLOCK), F32),
                            pltpu.VMEM((ATTN_HEADS, 2, MOBA_BLOCK, MOBA_BLOCK), F32),
                            pltpu.VMEM((ATTN_HEADS * KV_GROUP, MOBA_BLOCK, MOBA_BLOCK), F32),
                            pltpu.VMEM((ATTN_HEADS * KV_GROUP, MOBA_BLOCK, MOBA_BLOCK), F32)],
        ),
        out_shape=out_shape,
        compiler_params=_params(2),
        name="moba_prompt",
    )(*args)
    if not pps:
        return outs[0], None
    b, npg = page_table.shape
    return outs[0], outs[1].reshape(b, npg // PAGES_PER_BLOCK, D_MODEL)


def _page_sum_kernel(pt_ref, *refs, pps):
    del pt_ref
    _block_sums(refs[:pps], refs[pps])


def _page_sums(cache, page_table):
    b, npg = page_table.shape
    pps = min(16, npg)
    bps = pps // PAGES_PER_BLOCK
    sums = pl.pallas_call(
        functools.partial(_page_sum_kernel, pps=pps),
        grid_spec=pltpu.PrefetchScalarGridSpec(
            num_scalar_prefetch=1,
            grid=(b, npg // pps),
            in_specs=_page_specs(pps, lambda bi, j: bi * npg + j * pps),
            out_specs=pl.BlockSpec((1, bps, N_HEADS, HEAD_DIM),
                                   lambda bi, j, pt: (bi * (npg // pps) + j, 0, 0, 0)),
        ),
        out_shape=jax.ShapeDtypeStruct((b * (npg // pps), bps, N_HEADS, HEAD_DIM), F32),
        compiler_params=_params(2),
        name="page_sums",
    )(page_table.reshape(-1), *([cache] * pps))
    return sums.reshape(b, npg // PAGES_PER_BLOCK, D_MODEL)


def _sample_gate_kernel(q_ref, bs_ref, seg_ref, sel_ref, *, t_new, nblk):
    kmean = bs_ref[0] * (1.0 / MOBA_BLOCK)
    n_iota = lax.broadcasted_iota(I32, (nblk, LANES), 0)
    sel_ref[...] = jnp.zeros_like(sel_ref)
    for t in range(t_new):
        prod = kmean * q_ref[0, t:t + 1, :]
        scores = _select_matmul(prod, seg_ref[...])
        for r, (idx, _) in enumerate(_top3_rows(scores, n_iota, nblk)):
            sel_ref[0, t, r:r + 1, :] = idx


def _sample_gate(q, bsum):
    b, t_new, _ = q.shape
    nblk = bsum.shape[1]
    seg = (jnp.arange(D_MODEL, dtype=I32)[:, None] // HEAD_DIM
           == jnp.arange(LANES, dtype=I32)[None, :]).astype(BF16)
    sel = pl.pallas_call(
        functools.partial(_sample_gate_kernel, t_new=t_new, nblk=nblk),
        grid=(b,),
        in_specs=[pl.BlockSpec((1, t_new, D_MODEL), lambda i: (i, 0, 0)),
                  pl.BlockSpec((1, nblk, D_MODEL), lambda i: (i, 0, 0)),
                  _whole((D_MODEL, LANES))],
        out_specs=pl.BlockSpec((1, t_new, SUBLANES, LANES), lambda i: (i, 0, 0, 0)),
        out_shape=jax.ShapeDtypeStruct((b, t_new, SUBLANES, LANES), I32),
        compiler_params=_params(1),
        name="sample_gate",
    )(q, bsum, seg)
    return sel[:, :, :MOBA_TOPK, :N_HEADS]


SLABS_PER_HEAD = MOBA_TOPK * PAGES_PER_BLOCK


def _sample_attn_kernel(sel_ref, pt_ref, rb_ref, q_ref, kn_ref, vn_ref, pb_ref, ck_hbm, cv_hbm, o_ref,
                        kbuf, vbuf, sems, o_scr, *, t_new, npg):
    bi, t = pl.program_id(0), pl.program_id(1)
    step = bi * t_new + t
    n_steps = pl.num_programs(0) * t_new
    slot = lax.rem(step, 2)
    last_blk = npg // PAGES_PER_BLOCK - 1

    def selected_block(b_, t_, j, h):
        return sel_ref[((b_ * t_new + t_) * MOBA_TOPK + j) * N_HEADS + h]

    def slab_copies(b_, t_, slot_):
        copies = []
        for h in range(N_HEADS):
            for j in range(MOBA_TOPK):
                blk = selected_block(b_, t_, j, h)
                for pg in range(PAGES_PER_BLOCK):
                    page = pt_ref[b_ * npg + blk * PAGES_PER_BLOCK + pg]
                    i = j * PAGES_PER_BLOCK + pg
                    copies.append(pltpu.make_async_copy(ck_hbm.at[page, h], kbuf.at[slot_, h, i],
                                                        sems.at[0, slot_]))
                    copies.append(pltpu.make_async_copy(cv_hbm.at[page, h], vbuf.at[slot_, h, i],
                                                        sems.at[1, slot_]))
        return copies

    @pl.when(step == 0)
    def _first_fetch():
        for c in slab_copies(bi, t, slot):
            c.start()

    @pl.when(step + 1 < n_steps)
    def _prefetch_next():
        nxt = step + 1
        for c in slab_copies(lax.div(nxt, t_new), lax.rem(nxt, t_new), 1 - slot):
            c.start()

    for c in slab_copies(bi, t, slot):
        c.wait()

    row = lax.broadcasted_iota(I32, (SUBLANES, 1), 0)
    q = q_ref[0, 0] * (HEAD_DIM ** -0.5)
    for h in range(N_HEADS):
        cols = slice(h * HEAD_DIM, (h + 1) * HEAD_DIM)
        far_bias = rb_ref[REL_BUCKETS - 1, h]
        qh = q[:, cols]
        q8 = jnp.broadcast_to(qh, (SUBLANES, HEAD_DIM)).astype(BF16)
        l1 = []
        for j in range(MOBA_TOPK):
            blk_j = selected_block(bi, t, j, h)
            for pg in range(PAGES_PER_BLOCK):
                k_t = kbuf[slot, h, j * PAGES_PER_BLOCK + pg].astype(BF16)
                near = _bias_from_buckets(pb_ref[pl.ds(t * PAGES_PER_BLOCK + pg, 1), :], rb_ref, h)
                l1.append(jnp.dot(q8, k_t, preferred_element_type=F32)[0:1]
                          + jnp.where(blk_j == last_blk, near, far_bias))
        l1 = jnp.concatenate(l1, axis=1)
        l2 = jnp.sum(kn_ref[0, :, cols] * qh, axis=-1, keepdims=True)
        bias2 = jnp.zeros((SUBLANES, 1), F32)
        for u in range(t_new):
            bias2 = jnp.where(row == u, rb_ref[jnp.maximum(t - u, 0), h], bias2)
        l2 = jnp.where(row <= t, l2 + bias2, NEG_INF)
        m = jnp.maximum(jnp.max(l1, axis=-1, keepdims=True), jnp.max(l2, axis=0, keepdims=True))
        p1 = jnp.exp(l1 - m)
        p2 = jnp.exp(l2 - m)
        denom = jnp.sum(p1, axis=-1, keepdims=True) + jnp.sum(p2, axis=0, keepdims=True)
        p8 = jnp.broadcast_to(p1, (SUBLANES, SLABS_PER_HEAD * PAGE_SIZE)).astype(BF16)
        o = jnp.sum(p2 * vn_ref[0, :, cols], axis=0, keepdims=True)
        for i in range(SLABS_PER_HEAD):
            v_t = vbuf[slot, h, i].astype(BF16)
            o = o + lax.dot_general(p8[:, i * PAGE_SIZE:(i + 1) * PAGE_SIZE], v_t, (((1,), (1,)), ((), ())),
                                    preferred_element_type=F32)[0:1]
        o_scr[:, cols] = jnp.broadcast_to(o / denom, (SUBLANES, HEAD_DIM))
    o_ref[0, 0] = o_scr[0:1, :]


def _sample_attn(q, k_new, v_new, cache_k, cache_v, sel, page_table, rel_bias):
    b, t_new, _ = q.shape
    npg = page_table.shape[1]
    pad = ((0, 0), (0, SUBLANES - t_new), (0, 0))
    tt = jnp.arange(t_new, dtype=I32)[:, None, None]
    pg = jnp.arange(PAGES_PER_BLOCK, dtype=I32)[None, :, None]
    rr = jnp.arange(PAGE_SIZE, dtype=I32)[None, None, :]
    near_bkt = _rel_bucket(MOBA_BLOCK + tt - pg * PAGE_SIZE - rr).reshape(t_new * PAGES_PER_BLOCK, PAGE_SIZE)

    new_spec = pl.BlockSpec((1, SUBLANES, D_MODEL), lambda bi, t, *_: (bi, 0, 0))
    tok_spec = pl.BlockSpec((1, 1, 1, D_MODEL), lambda bi, t, *_: (bi, t, 0, 0))
    slab_buf = pltpu.VMEM((2, N_HEADS, SLABS_PER_HEAD, HEAD_DIM, PAGE_SIZE), F32)
    out = pl.pallas_call(
        functools.partial(_sample_attn_kernel, t_new=t_new, npg=npg),
        grid_spec=pltpu.PrefetchScalarGridSpec(
            num_scalar_prefetch=2,
            grid=(b, t_new),
            in_specs=[_smem(), tok_spec, new_spec, new_spec,
                      pl.BlockSpec((t_new * PAGES_PER_BLOCK, PAGE_SIZE), lambda *_: (0, 0)),
                      pl.BlockSpec(memory_space=pl.ANY), pl.BlockSpec(memory_space=pl.ANY)],
            out_specs=tok_spec,
            scratch_shapes=[slab_buf, slab_buf, pltpu.SemaphoreType.DMA((2, 2)),
                            pltpu.VMEM((SUBLANES, D_MODEL), F32)],
        ),
        out_shape=jax.ShapeDtypeStruct((b, t_new, 1, D_MODEL), F32),
        compiler_params=_params(2),
        name="sample_attn",
    )(sel.reshape(-1), page_table.reshape(-1), rel_bias,
      q.reshape(b, t_new, 1, D_MODEL), jnp.pad(k_new, pad), jnp.pad(v_new, pad), near_bkt,
      cache_k, cache_v)
    return out.reshape(b, t_new, D_MODEL)


def _ssd_chunk(t, ext_ref, p_ref, st_ref, gy_ref, cw_ref, cb_ref, dtb_ref, alog_ref, dsk_ref,
               nw_ref, e_ref, wout_ref, n_valid):
    def conv_act(lo, width):
        acc = cb_ref[:, lo:lo + width]
        for j in range(CONV_WIDTH):
            acc = acc + cw_ref[j:j + 1, lo:lo + width] * ext_ref[5 + j:5 + j + t, lo:lo + width]
        return _silu(acc)

    dt = _softplus(p_ref[:, DT_OFF:DT_OFF + LANES] + dtb_ref[...])
    if n_valid < t:
        dt = jnp.where(lax.broadcasted_iota(I32, (t, LANES), 0) < n_valid, dt, 0.0)
    a = dt * -jnp.exp(alog_ref[...])
    r_i = lax.broadcasted_iota(I32, (t, t), 0)
    c_i = lax.broadcasted_iota(I32, (t, t), 1)
    causal = r_i >= c_i
    acs = jnp.dot(causal.astype(F32), a, preferred_element_type=F32,
                  precision=lax.Precision.HIGHEST)
    if t % LANES == 0:
        acs_t = acs.T
    else:
        acs_t = lax.dot_general(jnp.eye(LANES, dtype=F32), acs, (((1,), (1,)), ((), ())),
                                preferred_element_type=F32, precision=lax.Precision.HIGHEST)
    lane_hi = lax.broadcasted_iota(I32, (t, LANES), 1) >= SSM_HEAD_DIM

    for g in range(SSM_GROUPS):
        ch = slice(g * GROUP_DIM, (g + 1) * GROUP_DIM)
        e_g = e_ref[:, ch]
        xs = conv_act(g * GROUP_DIM, GROUP_DIM)
        b_g = conv_act(D_INNER + g * SSM_STATE, SSM_STATE).astype(BF16)
        c_g = conv_act(D_INNER + (SSM_GROUPS + g) * SSM_STATE, SSM_STATE).astype(BF16)
        dt_g = _select_matmul(dt, e_g)
        acs_g = _select_matmul(acs, e_g)
        last_g = acs_g[t - 1:t, :]
        xdt = xs * dt_g
        xdt_b = xdt.astype(BF16)
        xdt_dec = (xdt * jnp.exp(last_g - acs_g)).astype(BF16)
        cb = lax.dot_general(c_g, b_g, (((1,), (1,)), ((), ())), preferred_element_type=F32)
        st_in = st_ref[:, ch]
        y_off = jnp.dot(c_g, st_in.astype(BF16), preferred_element_type=F32) * jnp.exp(acs_g)
        st_new = lax.dot_general(b_g, xdt_dec, (((0,), (0,)), ((), ())), preferred_element_type=F32)
        st_ref[:, ch] = st_in * jnp.exp(last_g) + st_new
        for j in range(HEADS_PER_GROUP // HEADS_PER_SLAB):
            cols = slice(j * LANES, (j + 1) * LANES)
            pair = xdt_b[:, cols]
            ys = []
            for hh in range(HEADS_PER_SLAB):
                h = g * HEADS_PER_GROUP + j * HEADS_PER_SLAB + hh
                decay = jnp.exp(jnp.where(causal, acs[:, h:h + 1] - acs_t[h:h + 1, :], NEG_INF))
                ys.append(jnp.dot((cb * decay).astype(BF16), pair, preferred_element_type=F32))
            y_diag = jnp.where(lane_hi, ys[1], ys[0])
            lo = g * GROUP_DIM + j * LANES
            y = y_diag + y_off[:, cols] + dsk_ref[:, lo:lo + LANES] * xs[:, cols]
            gy_ref[:, lo:lo + LANES] = y * _silu(p_ref[:, lo:lo + LANES])
        gy = gy_ref[:, ch]
        ms = jnp.mean(gy * gy, axis=-1, keepdims=True)
        gy_ref[:, ch] = gy * lax.rsqrt(ms + RMS_EPS) * nw_ref[:, ch]
    return jnp.dot(gy_ref[...].astype(BF16), wout_ref[...], preferred_element_type=F32)


def _ssm_prompt_kernel(p_ref, xres_ref, cw_ref, cb_ref, dtb_ref, alog_ref, dsk_ref, nw_ref, e_ref,
                       wout_ref, y_ref, st_out_ref, cv_out_ref, ext_ref, st_ref, gy_ref):
    c = pl.program_id(0)
    t = SSD_CHUNK

    @pl.when(c == 0)
    def _init():
        ext_ref[t:t + SUBLANES, :] = jnp.zeros((SUBLANES, CONV_DIM), F32)
        st_ref[...] = jnp.zeros_like(st_ref)

    ext_ref[0:SUBLANES, :] = ext_ref[t:t + SUBLANES, :]
    ext_ref[SUBLANES:SUBLANES + t, :] = p_ref[:, D_INNER:D_INNER + CONV_DIM]
    out = _ssd_chunk(t, ext_ref, p_ref, st_ref, gy_ref, cw_ref, cb_ref, dtb_ref, alog_ref, dsk_ref,
                     nw_ref, e_ref, wout_ref, t)
    y_ref[...] = xres_ref[...] + out

    @pl.when(c == pl.num_programs(0) - 1)
    def _finish():
        st_out_ref[...] = st_ref[...].T
        cv_out_ref[...] = ext_ref[t:t + SUBLANES, :]


def _ssm_consts(w):
    conv_w, conv_b, dt_bias, a_log, d_skip, norm_w, w_out = w
    lane_pad = (0, LANES - SSM_HEADS)
    expand = (jnp.arange(LANES, dtype=I32)[:, None]
              == jnp.arange(D_INNER, dtype=I32)[None, :] // SSM_HEAD_DIM).astype(BF16)
    return (conv_w, conv_b.reshape(1, CONV_DIM),
            jnp.pad(dt_bias, lane_pad).reshape(1, LANES), jnp.pad(a_log, lane_pad).reshape(1, LANES),
            jnp.repeat(d_skip, SSM_HEAD_DIM).reshape(1, D_INNER), norm_w.reshape(1, D_INNER),
            expand, w_out)


def _const_specs(consts):
    return [_whole(c.shape) for c in consts]


def _ssm_prompt(proj, xres, w):
    s = proj.shape[0]
    t = SSD_CHUNK
    consts = _ssm_consts(w)
    y, st, cv = pl.pallas_call(
        _ssm_prompt_kernel,
        grid=(s // t,),
        in_specs=[pl.BlockSpec((t, SSM_PROJ_PAD), lambda c: (c, 0)),
                  pl.BlockSpec((t, D_MODEL), lambda c: (c, 0))] + _const_specs(consts),
        out_specs=[pl.BlockSpec((t, D_MODEL), lambda c: (c, 0)),
                   _whole((D_INNER, SSM_STATE)), _whole((SUBLANES, CONV_DIM))],
        out_shape=[jax.ShapeDtypeStruct((s, D_MODEL), F32),
                   jax.ShapeDtypeStruct((D_INNER, SSM_STATE), F32),
                   jax.ShapeDtypeStruct((SUBLANES, CONV_DIM), F32)],
        scratch_shapes=[pltpu.VMEM((t + 2 * SUBLANES, CONV_DIM), F32),
                        pltpu.VMEM((SSM_STATE, D_INNER), F32),
                        pltpu.VMEM((t, D_INNER), F32)],
        compiler_params=_params(1),
        name="ssm_prompt",
    )(proj, xres, *consts)
    return y, st, cv[SUBLANES - (CONV_WIDTH - 1):]


def _ssm_sample_kernel(p_ref, xres_ref, cst_ref, st_in_ref, cw_ref, cb_ref, dtb_ref, alog_ref,
                       dsk_ref, nw_ref, e_ref, wout_ref, y_ref, st_out_ref, cv_out_ref,
                       ext_ref, st_ref, gy_ref, *, t_new):
    t = SAMPLE_T
    ext_ref[0:SUBLANES, :] = cst_ref[0]
    ext_ref[SUBLANES:SUBLANES + t, :] = p_ref[0, :, D_INNER:D_INNER + CONV_DIM]
    st_ref[...] = st_in_ref[0].T
    out = _ssd_chunk(t, ext_ref, p_ref.at[0], st_ref, gy_ref, cw_ref, cb_ref, dtb_ref, alog_ref,
                     dsk_ref, nw_ref, e_ref, wout_ref, t_new)
    y_ref[0] = xres_ref[0] + out[0:t_new]
    st_out_ref[0] = st_ref[...].T
    cv_out_ref[0] = ext_ref[SUBLANES:SUBLANES + t, :]


def _ssm_sample(proj, xres, conv_state, ssm_state, w):
    b, t_new, _ = proj.shape
    t = SAMPLE_T
    consts = _ssm_consts(w)
    proj8 = jnp.pad(proj, ((0, 0), (0, t - t_new), (0, 0)))
    cst8 = jnp.pad(conv_state, ((0, 0), (SUBLANES - (CONV_WIDTH - 1), 0), (0, 0)))
    y, st, cv = pl.pallas_call(
        functools.partial(_ssm_sample_kernel, t_new=t_new),
        grid=(b,),
        in_specs=[pl.BlockSpec((1, t, SSM_PROJ_PAD), lambda i: (i, 0, 0)),
                  pl.BlockSpec((1, t_new, D_MODEL), lambda i: (i, 0, 0)),
                  pl.BlockSpec((1, SUBLANES, CONV_DIM), lambda i: (i, 0, 0)),
                  pl.BlockSpec((1, D_INNER, SSM_STATE), lambda i: (i, 0, 0))] + _const_specs(consts),
        out_specs=[pl.BlockSpec((1, t_new, D_MODEL), lambda i: (i, 0, 0)),
                   pl.BlockSpec((1, D_INNER, SSM_STATE), lambda i: (i, 0, 0)),
                   pl.BlockSpec((1, t, CONV_DIM), lambda i: (i, 0, 0))],
        out_shape=[jax.ShapeDtypeStruct((b, t_new, D_MODEL), F32),
                   jax.ShapeDtypeStruct((b, D_INNER, SSM_STATE), F32),
                   jax.ShapeDtypeStruct((b, t, CONV_DIM), F32)],
        scratch_shapes=[pltpu.VMEM((2 * SUBLANES, CONV_DIM), F32),
                        pltpu.VMEM((SSM_STATE, D_INNER), F32),
                        pltpu.VMEM((t, D_INNER), F32)],
        compiler_params=_params(1),
        name="ssm_sample",
    )(proj8, xres, cst8, ssm_state, *consts)
    return y, st, cv[:, t_new - (CONV_WIDTH - 1):t_new]


def _ffn_weights(w_in, w_out):
    def chunks(w):
        return w.reshape(D_MODEL, N_FF_CHUNKS, FF_CHUNK).transpose(1, 0, 2).astype(BF16)
    return (chunks(w_in[:, :D_FF]), chunks(w_in[:, D_FF:]),
            w_out.reshape(N_FF_CHUNKS, FF_CHUNK, D_MODEL).astype(BF16))


def _row_tile(rows, want=512):
    return want if rows % want == 0 else 128


def kernel(x_prompt, x_sample, cache_k, cache_v, page_table, state_ssm, state_conv, ffn1_norm, ffn1_w_in, ffn1_w_out, mix_norm, ffn2_norm, ffn2_w_in, ffn2_w_out, attn_w_qkv, attn_w_o, rel_bias, ssm_w_in, ssm_conv_w, ssm_conv_b, ssm_dt_bias, ssm_A_log, ssm_D, ssm_norm, ssm_w_out, final_norm):
    bp, s, _ = x_prompt.shape
    bs, t_new, _ = x_sample.shape
    assert bp == 1 and attn_w_qkv.shape[0] == 1 and ssm_w_in.shape[0] == 1
    xp = x_prompt.reshape(s, D_MODEL)
    xs = x_sample.reshape(bs * t_new, D_MODEL)
    tp, ts = _row_tile(s), _row_tile(bs * t_new)
    tf = _row_tile(s, 1024)

    w1 = _ffn_weights(ffn1_w_in[0], ffn1_w_out[0])
    xp = _ffn(xp, ffn1_norm[0], w1, tf)
    xs = _ffn(xs, ffn1_norm[0], w1, ts)

    wqkv = attn_w_qkv[0].astype(BF16)
    wq, wk, wv = (wqkv[:, i * D_MODEL:(i + 1) * D_MODEL] for i in range(3))
    wo = attn_w_o[0].astype(BF16)
    ck = jnp.transpose(cache_k[0], (0, 2, 3, 1))
    cv = jnp.transpose(cache_v[0], (0, 2, 3, 1))
    k_p, v_p, kb, qt, vt, ksum = _qkv_prompt(xp, mix_norm[0], wq.T, wk, wk.T, wv.T, tp)
    o_p, bsum = _moba_prompt(qt, kb, vt, ksum.reshape(s // MOBA_BLOCK, D_MODEL), rel_bias, ck, page_table)
    if bsum is None:
        bsum = _page_sums(ck, page_table)
    xp = _matres(o_p, wo, xp, tp)

    qkv_s = _proj(xs, mix_norm[0], wqkv, ts).reshape(bs, t_new, 3 * D_MODEL)
    q_s, k_s, v_s = (qkv_s[..., i * D_MODEL:(i + 1) * D_MODEL] for i in range(3))
    sel = _sample_gate(q_s, bsum)
    o_s = _sample_attn(q_s, k_s, v_s, ck, cv, sel, page_table, rel_bias)
    xs = _matres(o_s.reshape(bs * t_new, D_MODEL), wo, xs, ts)

    w2 = _ffn_weights(ffn2_w_in[0], ffn2_w_out[0])
    xp = _ffn(xp, ffn2_norm[0], w2, tf)
    xs = _ffn(xs, ffn2_norm[0], w2, ts)

    w1 = _ffn_weights(ffn1_w_in[1], ffn1_w_out[1])
    xp = _ffn(xp, ffn1_norm[1], w1, tf)
    xs = _ffn(xs, ffn1_norm[1], w1, ts)

    w_in = jnp.pad(ssm_w_in[0], ((0, 0), (0, SSM_PROJ_PAD - SSM_IN_DIM))).astype(BF16)
    ssm_w = (ssm_conv_w[0], ssm_conv_b[0], ssm_dt_bias[0], ssm_A_log[0], ssm_D[0], ssm_norm[0],
             ssm_w_out[0].astype(BF16))
    xp, st_p, cv_p = _ssm_prompt(_proj(xp, mix_norm[1], w_in, tp), xp, ssm_w)
    proj_s = _proj(xs, mix_norm[1], w_in, ts).reshape(bs, t_new, SSM_PROJ_PAD)
    xs3, st_s, cv_s = _ssm_sample(proj_s, xs.reshape(bs, t_new, D_MODEL), state_conv[0],
                                  state_ssm[0].reshape(bs, D_INNER, SSM_STATE), ssm_w)
    xs = xs3.reshape(bs * t_new, D_MODEL)

    w2 = _ffn_weights(ffn2_w_in[1], ffn2_w_out[1])
    y_p = _ffn(xp, ffn2_norm[1], w2, tf, final_g=final_norm)
    y_s = _ffn(xs, ffn2_norm[1], w2, ts, final_g=final_norm)

    head_shape = (N_HEADS, HEAD_DIM)
    state_shape = (SSM_HEADS, SSM_HEAD_DIM, SSM_STATE)
    return (y_p.reshape(1, s, D_MODEL), y_s.reshape(bs, t_new, D_MODEL),
            jnp.transpose(k_p, (0, 3, 1, 2))[None, None], jnp.transpose(v_p, (0, 3, 1, 2))[None, None],
            k_s.reshape(1, bs, t_new, *head_shape), v_s.reshape(1, bs, t_new, *head_shape),
            st_p.reshape(1, 1, *state_shape), cv_p.reshape(1, 1, CONV_WIDTH - 1, CONV_DIM),
            st_s.reshape(1, bs, *state_shape), cv_s.reshape(1, bs, CONV_WIDTH - 1, CONV_DIM))
```

```python
import functools
import math

import jax
import jax.numpy as jnp
from jax import lax
from jax.experimental import pallas as pl
from jax.experimental.pallas import tpu as pltpu

F32 = jnp.float32
BF16 = jnp.bfloat16
I32 = jnp.int32
NEG_INF = float("-inf")
LOG2_E = math.log2(math.e)

D_MODEL = 1024
N_HEADS = 16
HEAD_DIM = 64
MOBA_BLOCK = 256
MOBA_TOPK = 3
PAGE_SIZE = 128
REL_BUCKETS = 32
REL_MAX_DIST = 128
D_INNER = 2048
SSM_HEADS = 32
SSM_HEAD_DIM = 64
SSM_GROUPS = 4
SSM_STATE = 128
CONV_WIDTH = 4
CONV_DIM = D_INNER + 2 * SSM_GROUPS * SSM_STATE
SSM_IN_DIM = 2 * D_INNER + 2 * SSM_GROUPS * SSM_STATE + SSM_HEADS
SSD_CHUNK = 256
D_FF = 2816
RMS_EPS = 1e-6

LANES = 128
SUBLANES = 8
VMEM_LIMIT_BYTES = 56 * 1024 * 1024

FF_CHUNK = 256
N_FF_CHUNKS = D_FF // FF_CHUNK
HEADS_PER_SLAB = LANES // HEAD_DIM
GROUP_DIM = D_INNER // SSM_GROUPS
HEADS_PER_GROUP = SSM_HEADS // SSM_GROUPS
SSM_PROJ_PAD = -(-SSM_IN_DIM // LANES) * LANES
DT_OFF = D_INNER + CONV_DIM
SAMPLE_T = 8
PAGES_PER_BLOCK = MOBA_BLOCK // PAGE_SIZE
KV_GROUP = 4
ATTN_HEADS = 4
ATTN_WIDTH = ATTN_HEADS * HEAD_DIM
SUM_ROWS = 16


def _params(n_axes, vmem=VMEM_LIMIT_BYTES):
    return pltpu.CompilerParams(dimension_semantics=("arbitrary",) * n_axes,
                                vmem_limit_bytes=vmem)


def _whole(shape):
    nd = len(shape)
    return pl.BlockSpec(shape, lambda *_: (0,) * nd)


def _smem():
    return pl.BlockSpec(memory_space=pltpu.SMEM)


def _rms(x, g):
    ms = jnp.mean(x * x, axis=-1, keepdims=True)
    return x * lax.rsqrt(ms + RMS_EPS) * g


def _silu(x):
    return x * jax.nn.sigmoid(x)


def _softplus(x):
    return jnp.maximum(x, 0.0) + jnp.log1p(jnp.exp(-jnp.abs(x)))


def _split3(x):
    hi = x.astype(BF16)
    r1 = x - hi.astype(F32)
    mid = r1.astype(BF16)
    lo = (r1 - mid.astype(F32)).astype(BF16)
    return hi, mid, lo


def _select_matmul(x, sel):
    hi, mid, lo = _split3(x)
    dot = functools.partial(jnp.dot, preferred_element_type=F32)
    return dot(hi, sel) + dot(mid, sel) + dot(lo, sel)


def _top3_rows(s, n_iota, sentinel):
    picks = []
    for _ in range(MOBA_TOPK):
        m = jnp.max(s, axis=0, keepdims=True)
        idx = jnp.min(jnp.where(s == m, n_iota, sentinel), axis=0, keepdims=True)
        picks.append((idx, m))
        s = jnp.where(n_iota == idx, NEG_INF, s)
    return picks


def _bias_from_buckets(bk, rb_ref, h):
    out = jnp.full(bk.shape, NEG_INF, F32)
    for b in range(REL_BUCKETS):
        out = jnp.where(bk == b, rb_ref[b, h], out)
    return out


def _rel_bucket(dist):
    n = jnp.maximum(dist, 0)
    max_exact = REL_BUCKETS // 2
    nf = jnp.maximum(n, 1).astype(F32)
    large = max_exact + (jnp.log(nf / max_exact) / math.log(REL_MAX_DIST / max_exact)
                         * (REL_BUCKETS - max_exact)).astype(I32)
    large = jnp.minimum(large, REL_BUCKETS - 1)
    return jnp.where(n < max_exact, n, large).astype(I32)


def _ffn_kernel(*refs, final):
    if final:
        x_ref, g_ref, wa_ref, wb_ref, wo_ref, fg_ref, o_ref, xn_ref, acc_ref = refs
    else:
        x_ref, g_ref, wa_ref, wb_ref, wo_ref, o_ref, xn_ref, acc_ref = refs
    xn_ref[...] = _rms(x_ref[...], g_ref[...]).astype(BF16)
    acc_ref[...] = jnp.zeros_like(acc_ref)

    def chunk(c, carry):
        xn = xn_ref[...]
        a = jnp.dot(xn, wa_ref[c], preferred_element_type=F32)
        b = jnp.dot(xn, wb_ref[c], preferred_element_type=F32)
        h = (_silu(a) * b).astype(BF16)
        acc_ref[...] += jnp.dot(h, wo_ref[c], preferred_element_type=F32)
        return carry

    lax.fori_loop(0, N_FF_CHUNKS, chunk, 0)
    y = x_ref[...] + 0.5 * acc_ref[...]
    if final:
        y = _rms(y, fg_ref[...])
    o_ref[...] = y


def _ffn(x, g, w, tm, final_g=None):
    rows = x.shape[0]
    wa, wb, wo = w
    row_spec = pl.BlockSpec((tm, D_MODEL), lambda i: (i, 0))
    in_specs = [row_spec, _whole((1, D_MODEL)), _whole(wa.shape), _whole(wb.shape), _whole(wo.shape)]
    args = [x, g.reshape(1, D_MODEL), wa, wb, wo]
    if final_g is not None:
        in_specs.append(_whole((1, D_MODEL)))
        args.append(final_g.reshape(1, D_MODEL))
    return pl.pallas_call(
        functools.partial(_ffn_kernel, final=final_g is not None),
        grid=(rows // tm,),
        in_specs=in_specs,
        out_specs=row_spec,
        out_shape=jax.ShapeDtypeStruct((rows, D_MODEL), F32),
        scratch_shapes=[pltpu.VMEM((tm, D_MODEL), BF16), pltpu.VMEM((tm, D_MODEL), F32)],
        compiler_params=_params(1),
        name="ffn",
    )(*args)


def _proj_kernel(x_ref, g_ref, w_ref, o_ref, xn_ref, *, widths):
    xn_ref[...] = _rms(x_ref[...], g_ref[...]).astype(BF16)
    lo = 0
    for wd in widths:
        o_ref[:, lo:lo + wd] = jnp.dot(xn_ref[...], w_ref[:, lo:lo + wd], preferred_element_type=F32)
        lo += wd


def _proj(x, g, w, tm):
    rows, n = x.shape[0], w.shape[1]
    col_chunk = 4 * LANES
    widths = [col_chunk] * (n // col_chunk) + ([n % col_chunk] if n % col_chunk else [])
    return pl.pallas_call(
        functools.partial(_proj_kernel, widths=tuple(widths)),
        grid=(rows // tm,),
        in_specs=[pl.BlockSpec((tm, D_MODEL), lambda i: (i, 0)), _whole((1, D_MODEL)), _whole(w.shape)],
        out_specs=pl.BlockSpec((tm, n), lambda i: (i, 0)),
        out_shape=jax.ShapeDtypeStruct((rows, n), F32),
        scratch_shapes=[pltpu.VMEM((tm, D_MODEL), BF16)],
        compiler_params=_params(1),
        name="norm_proj",
    )(x, g.reshape(1, D_MODEL), w)


def _matres_kernel(a_ref, w_ref, r_ref, o_ref):
    o_ref[...] = r_ref[...] + jnp.dot(a_ref[...].astype(BF16), w_ref[...], preferred_element_type=F32)


def _matres(a, w, res, tm):
    rows, k = a.shape
    n = w.shape[1]
    return pl.pallas_call(
        _matres_kernel,
        grid=(rows // tm,),
        in_specs=[pl.BlockSpec((tm, k), lambda i: (i, 0)), _whole(w.shape),
                  pl.BlockSpec((tm, n), lambda i: (i, 0))],
        out_specs=pl.BlockSpec((tm, n), lambda i: (i, 0)),
        out_shape=jax.ShapeDtypeStruct((rows, n), F32),
        compiler_params=_params(1),
        name="matmul_residual",
    )(a, w, res)


def _qkv_prompt_kernel(x_ref, g_ref, wqt_ref, wk_ref, wkt_ref, wvt_ref,
                       kp_ref, vp_ref, kb_ref, qt_ref, vt_ref, ks_ref, xn_ref, *, tm):
    xn_ref[...] = _rms(x_ref[...], g_ref[...]).astype(BF16)
    xn = xn_ref[...]
    k = jnp.dot(xn, wk_ref[...], preferred_element_type=F32)
    kb_ref[...] = k.astype(BF16)
    for j in range(tm // MOBA_BLOCK):
        ks_ref[0, j:j + 1, :] = jnp.sum(k[j * MOBA_BLOCK:(j + 1) * MOBA_BLOCK], axis=0, keepdims=True)
    nt = (((1,), (1,)), ((), ()))
    qt = lax.dot_general(wqt_ref[...], xn, nt, preferred_element_type=F32)
    qt_ref[...] = (qt * (LOG2_E * HEAD_DIM ** -0.5)).astype(BF16)
    kt = lax.dot_general(wkt_ref[...], xn, nt, preferred_element_type=F32)
    vt = lax.dot_general(wvt_ref[...], xn, nt, preferred_element_type=F32)
    for j in range(tm // MOBA_BLOCK):
        vt_ref[j] = vt[:, j * MOBA_BLOCK:(j + 1) * MOBA_BLOCK].astype(BF16)
    for pg in range(tm // PAGE_SIZE):
        rows = slice(pg * PAGE_SIZE, (pg + 1) * PAGE_SIZE)
        kp_ref[pg] = kt[:, rows].reshape(N_HEADS, HEAD_DIM, PAGE_SIZE)
        vp_ref[pg] = vt[:, rows].reshape(N_HEADS, HEAD_DIM, PAGE_SIZE)


def _qkv_prompt(x, g, wqt, wk, wkt, wvt, tm):
    s = x.shape[0]
    nblk = s // MOBA_BLOCK
    bpt = tm // MOBA_BLOCK
    ppt = tm // PAGE_SIZE
    row_spec = pl.BlockSpec((tm, D_MODEL), lambda i: (i, 0))
    page_spec = pl.BlockSpec((ppt, N_HEADS, HEAD_DIM, PAGE_SIZE), lambda i: (i, 0, 0, 0))
    pages = jax.ShapeDtypeStruct((s // PAGE_SIZE, N_HEADS, HEAD_DIM, PAGE_SIZE), F32)
    sq = _whole((D_MODEL, D_MODEL))
    return pl.pallas_call(
        functools.partial(_qkv_prompt_kernel, tm=tm),
        grid=(s // tm,),
        in_specs=[row_spec, _whole((1, D_MODEL)), sq, sq, sq, sq],
        out_specs=[page_spec, page_spec, row_spec,
                   pl.BlockSpec((D_MODEL, tm), lambda i: (0, i)),
                   pl.BlockSpec((bpt, D_MODEL, MOBA_BLOCK), lambda i: (i, 0, 0)),
                   pl.BlockSpec((1, bpt, D_MODEL), lambda i: (i, 0, 0))],
        out_shape=[pages, pages,
                   jax.ShapeDtypeStruct((s, D_MODEL), BF16),
                   jax.ShapeDtypeStruct((D_MODEL, s), BF16),
                   jax.ShapeDtypeStruct((nblk, D_MODEL, MOBA_BLOCK), BF16),
                   jax.ShapeDtypeStruct((s // tm, bpt, D_MODEL), F32)],
        scratch_shapes=[pltpu.VMEM((tm, D_MODEL), BF16)],
        compiler_params=_params(1),
        name="qkv_prompt",
    )(x, g.reshape(1, D_MODEL), wqt, wk, wkt, wvt)


def _block_sums(pages, o_ref):
    for i in range(len(pages) // PAGES_PER_BLOCK):
        tot = pages[PAGES_PER_BLOCK * i][0]
        for j in range(1, PAGES_PER_BLOCK):
            tot = tot + pages[PAGES_PER_BLOCK * i + j][0]
        o_ref[0, i] = jnp.sum(tot, axis=-1)


def _page_specs(n, first_page):
    def spec(i):
        return pl.BlockSpec((1, N_HEADS, HEAD_DIM, PAGE_SIZE),
                            lambda *a: (a[-1][first_page(*a[:-1]) + i], 0, 0, 0))
    return [spec(i) for i in range(n)]


def _softmax_group_step(tiles, v_tiles, state):
    m, acc = state
    m_new = m
    for s, keep, shift in tiles:
        bm = jnp.max(s, axis=0, keepdims=True) + shift
        m_new = jnp.maximum(m_new, bm if keep is None else jnp.where(keep > 0.0, bm, NEG_INF))
    acc = jnp.exp2(m - m_new) * acc
    for (s, keep, shift), v_t in zip(tiles, v_tiles):
        off = m_new - shift
        if keep is not None:
            off = jnp.where(keep > 0.0, off, jnp.inf)
        p = jnp.exp2(s - off)
        acc = acc + jnp.dot(v_t, p.astype(BF16), preferred_element_type=F32)
    return m_new, acc


def _moba_prompt_kernel(*refs, nblk, pps):
    if pps:
        refs = refs[1:]
    rb_ref, qt_ref, kb_ref, vt_ref, ks_ref, bkt_ref = refs[:6]
    pages = refs[6:6 + pps]
    o_ref = refs[6 + pps]
    sel_ref, bias_ref, sa_ref, sb_ref = refs[-4:]
    slab = pl.program_id(0)
    qi = pl.program_id(1)
    blk = MOBA_BLOCK
    heads = range(ATTN_HEADS)

    @pl.when(qi == 0)
    def _build_bias():
        for hh in heads:
            for kind in range(2):
                bias_ref[hh, kind] = LOG2_E * _bias_from_buckets(bkt_ref[kind], rb_ref,
                                                                 slab * ATTN_HEADS + hh)

    if pps:
        _block_sums(pages, refs[7 + pps])
    qt = qt_ref[...].astype(F32)
    row_head = lax.broadcasted_iota(I32, (ATTN_WIDTH, blk), 0) // HEAD_DIM
    n_iota = lax.broadcasted_iota(I32, (nblk, blk), 0)
    kmean = (ks_ref[...] * (1.0 / blk)).astype(BF16)
    far_bias = [LOG2_E * rb_ref[REL_BUCKETS - 1, slab * ATTN_HEADS + hh] for hh in heads]
    qz = [jnp.where(row_head == hh, qt, 0.0).astype(BF16) for hh in heads]
    for hh in heads:
        scores = jnp.dot(kmean, qz[hh], preferred_element_type=F32)
        scores = jnp.where(n_iota < qi, scores, NEG_INF)
        sel = jnp.zeros((nblk, blk), F32)
        for idx, m in _top3_rows(scores, n_iota, nblk):
            sel = jnp.where((n_iota == idx) & (m > NEG_INF), 1.0, sel)
        sel_ref[hh] = sel

    def fill(buf, base):
        for j in range(KV_GROUP):
            nc = jnp.maximum(base + j, 0)
            k_n = kb_ref[pl.ds(pl.multiple_of(nc * blk, blk), blk), :]
            for hh in heads:
                buf[hh * KV_GROUP + j] = jnp.dot(k_n, qz[hh], preferred_element_type=F32)

    def consume(buf, base, kinds, states):
        out = []
        for hh in heads:
            tiles, v_tiles = [], []
            for j, kind in enumerate(kinds):
                n = base + j
                nc = jnp.maximum(n, 0)
                s = buf[hh * KV_GROUP + j]
                v_tiles.append(jnp.concatenate([vt_ref[nc, hh * HEAD_DIM:(hh + 1) * HEAD_DIM, :], ones], axis=0))
                if kind == "own":
                    tiles.append((s + bias_ref[hh, 0], None, 0.0))
                    continue
                keep = jnp.where(n >= 0, sel_ref[hh, pl.ds(nc, 1), :], 0.0)
                if kind == "prev":
                    tiles.append((s + bias_ref[hh, 1], keep, 0.0))
                else:
                    tiles.append((s, keep, far_bias[hh]))
            out.append(_softmax_group_step(tiles, v_tiles, states[hh]))
        return tuple(out)

    near_base = qi - (KV_GROUP - 1)
    near_kinds = ["far"] * (KV_GROUP - 2) + ["prev", "own"]
    far_kinds = ["far"] * KV_GROUP

    def far_base(g):
        return near_base - KV_GROUP * (g + 1)

    ones = jnp.ones((SUM_ROWS, blk), BF16)
    init = tuple((jnp.full((1, blk), NEG_INF, F32), jnp.zeros((HEAD_DIM + SUM_ROWS, blk), F32))
                 for _ in heads)
    fill(sa_ref, near_base)
    fill(sb_ref, far_base(0))
    states = consume(sa_ref, near_base, near_kinds, init)

    def far_pair(i, states):
        g = 2 * i
        fill(sa_ref, far_base(g + 1))
        states = consume(sb_ref, far_base(g), far_kinds, states)
        fill(sb_ref, far_base(g + 2))
        return consume(sa_ref, far_base(g + 1), far_kinds, states)

    n_far_groups = lax.div(jnp.maximum(near_base, 0) + KV_GROUP - 1, KV_GROUP)
    n_pairs = lax.div(n_far_groups, 2)
    states = lax.fori_loop(0, n_pairs, far_pair, states)
    states = lax.cond(lax.rem(n_far_groups, 2) == 1,
                      lambda st: consume(sb_ref, far_base(2 * n_pairs), far_kinds, st),
                      lambda st: st, states)
    outs = [acc[:HEAD_DIM] / acc[HEAD_DIM:HEAD_DIM + 1] for _, acc in states]
    o_ref[...] = jnp.concatenate(outs, axis=0).T.astype(BF16)


def _prompt_bucket_maps():
    kk = jnp.arange(MOBA_BLOCK, dtype=I32)[:, None]
    tt = jnp.arange(MOBA_BLOCK, dtype=I32)[None, :]
    own = jnp.where(tt - kk >= 0, _rel_bucket(tt - kk), -1)
    prev = _rel_bucket(MOBA_BLOCK + tt - kk)
    return jnp.stack([own, prev]).astype(I32)


def _moba_prompt(qt, kb, vt, ksum, rel_bias, cache=None, page_table=None):
    s = kb.shape[0]
    nblk = s // MOBA_BLOCK
    grid = (N_HEADS // ATTN_HEADS, nblk)
    n_steps = grid[0] * grid[1]
    pps = 0
    if cache is not None and page_table.size % (n_steps * PAGES_PER_BLOCK) == 0:
        pps = page_table.size // n_steps
        if page_table.shape[1] % pps:
            pps = 0

    def imap(f):
        return (lambda sl, qi, pt: f(sl, qi)) if pps else f

    in_specs = [_smem(),
                pl.BlockSpec((ATTN_WIDTH, MOBA_BLOCK), imap(lambda sl, qi: (sl, qi))),
                pl.BlockSpec((s, ATTN_WIDTH), imap(lambda sl, qi: (0, sl)), pipeline_mode=pl.Buffered(1)),
                pl.BlockSpec((nblk, ATTN_WIDTH, MOBA_BLOCK), imap(lambda sl, qi: (0, sl, 0)),
                             pipeline_mode=pl.Buffered(1)),
                pl.BlockSpec((nblk, ATTN_WIDTH), imap(lambda sl, qi: (0, sl))),
                pl.BlockSpec((2, MOBA_BLOCK, MOBA_BLOCK), imap(lambda sl, qi: (0, 0, 0)))]
    out_specs = [pl.BlockSpec((MOBA_BLOCK, ATTN_WIDTH), imap(lambda sl, qi: (qi, sl)))]
    out_shape = [jax.ShapeDtypeStruct((s, D_MODEL), BF16)]
    args = [rel_bias, qt, kb, vt, ksum, _prompt_bucket_maps()]
    if pps:
        bps = pps // PAGES_PER_BLOCK
        in_specs += _page_specs(pps, lambda sl, qi: (sl * nblk + qi) * pps)
        out_specs.append(pl.BlockSpec((1, bps, N_HEADS, HEAD_DIM), lambda sl, qi, pt: (sl * nblk + qi, 0, 0, 0)))
        out_shape.append(jax.ShapeDtypeStruct((n_steps, bps, N_HEADS, HEAD_DIM), F32))
        args = [page_table.reshape(-1)] + args + [cache] * pps
    outs = pl.pallas_call(
        functools.partial(_moba_prompt_kernel, nblk=nblk, pps=pps),
        grid_spec=pltpu.PrefetchScalarGridSpec(
            num_scalar_prefetch=1 if pps else 0,
            grid=grid,
            in_specs=in_specs,
            out_specs=out_specs,
            scratch_shapes=[pltpu.VMEM((ATTN_HEADS, nblk, MOBA_BLOCK), F32),
                            pltpu.VMEM((ATTN_HEADS, 2, MOBA_BLOCK, MOBA_BLOCK), F32),
                            pltpu.VMEM((ATTN_HEADS * KV_GROUP, MOBA_BLOCK, MOBA_BLOCK), F32),
                            pltpu.VMEM((ATTN_HEADS * KV_GROUP, MOBA_BLOCK, MOBA_BLOCK), F32)],
        ),
        out_shape=out_shape,
        compiler_params=_params(2),
        name="moba_prompt",
    )(*args)
    if not pps:
        return outs[0], None
    b, npg = page_table.shape
    return outs[0], outs[1].reshape(b, npg // PAGES_PER_BLOCK, D_MODEL)


def _page_sum_kernel(pt_ref, *refs, pps):
    del pt_ref
    _block_sums(refs[:pps], refs[pps])


def _page_sums(cache, page_table):
    b, npg = page_table.shape
    pps = min(16, npg)
    bps = pps // PAGES_PER_BLOCK
    sums = pl.pallas_call(
        functools.partial(_page_sum_kernel, pps=pps),
        grid_spec=pltpu.PrefetchScalarGridSpec(
            num_scalar_prefetch=1,
            grid=(b, npg // pps),
            in_specs=_page_specs(pps, lambda bi, j: bi * npg + j * pps),
            out_specs=pl.BlockSpec((1, bps, N_HEADS, HEAD_DIM),
                                   lambda bi, j, pt: (bi * (npg // pps) + j, 0, 0, 0)),
        ),
        out_shape=jax.ShapeDtypeStruct((b * (npg // pps), bps, N_HEADS, HEAD_DIM), F32),
        compiler_params=_params(2),
        name="page_sums",
    )(page_table.reshape(-1), *([cache] * pps))
    return sums.reshape(b, npg // PAGES_PER_BLOCK, D_MODEL)


def _sample_gate_kernel(q_ref, bs_ref, seg_ref, sel_ref, *, t_new, nblk):
    kmean = bs_ref[0] * (1.0 / MOBA_BLOCK)
    n_iota = lax.broadcasted_iota(I32, (nblk, LANES), 0)
    sel_ref[...] = jnp.zeros_like(sel_ref)
    for t in range(t_new):
        prod = kmean * q_ref[0, t:t + 1, :]
        scores = _select_matmul(prod, seg_ref[...])
        for r, (idx, _) in enumerate(_top3_rows(scores, n_iota, nblk)):
            sel_ref[0, t, r:r + 1, :] = idx


def _sample_gate(q, bsum):
    b, t_new, _ = q.shape
    nblk = bsum.shape[1]
    seg = (jnp.arange(D_MODEL, dtype=I32)[:, None] // HEAD_DIM
           == jnp.arange(LANES, dtype=I32)[None, :]).astype(BF16)
    sel = pl.pallas_call(
        functools.partial(_sample_gate_kernel, t_new=t_new, nblk=nblk),
        grid=(b,),
        in_specs=[pl.BlockSpec((1, t_new, D_MODEL), lambda i: (i, 0, 0)),
                  pl.BlockSpec((1, nblk, D_MODEL), lambda i: (i, 0, 0)),
                  _whole((D_MODEL, LANES))],
        out_specs=pl.BlockSpec((1, t_new, SUBLANES, LANES), lambda i: (i, 0, 0, 0)),
        out_shape=jax.ShapeDtypeStruct((b, t_new, SUBLANES, LANES), I32),
        compiler_params=_params(1),
        name="sample_gate",
    )(q, bsum, seg)
    return sel[:, :, :MOBA_TOPK, :N_HEADS]


SLABS_PER_HEAD = MOBA_TOPK * PAGES_PER_BLOCK


def _sample_attn_kernel(sel_ref, pt_ref, rb_ref, q_ref, kn_ref, vn_ref, pb_ref, ck_hbm, cv_hbm, o_ref,
                        kbuf, vbuf, sems, o_scr, *, t_new, npg):
    bi, t = pl.program_id(0), pl.program_id(1)
    step = bi * t_new + t
    n_steps = pl.num_programs(0) * t_new
    slot = lax.rem(step, 2)
    last_blk = npg // PAGES_PER_BLOCK - 1

    def selected_block(b_, t_, j, h):
        return sel_ref[((b_ * t_new + t_) * MOBA_TOPK + j) * N_HEADS + h]

    def slab_copies(b_, t_, slot_):
        copies = []
        for h in range(N_HEADS):
            for j in range(MOBA_TOPK):
                blk = selected_block(b_, t_, j, h)
                for pg in range(PAGES_PER_BLOCK):
                    page = pt_ref[b_ * npg + blk * PAGES_PER_BLOCK + pg]
                    i = j * PAGES_PER_BLOCK + pg
                    copies.append(pltpu.make_async_copy(ck_hbm.at[page, h], kbuf.at[slot_, h, i],
                                                        sems.at[0, slot_]))
                    copies.append(pltpu.make_async_copy(cv_hbm.at[page, h], vbuf.at[slot_, h, i],
                                                        sems.at[1, slot_]))
        return copies

    @pl.when(step == 0)
    def _first_fetch():
        for c in slab_copies(bi, t, slot):
            c.start()

    @pl.when(step + 1 < n_steps)
    def _prefetch_next():
        nxt = step + 1
        for c in slab_copies(lax.div(nxt, t_new), lax.rem(nxt, t_new), 1 - slot):
            c.start()

    for c in slab_copies(bi, t, slot):
        c.wait()

    row = lax.broadcasted_iota(I32, (SUBLANES, 1), 0)
    q = q_ref[0, 0] * (HEAD_DIM ** -0.5)
    for h in range(N_HEADS):
        cols = slice(h * HEAD_DIM, (h + 1) * HEAD_DIM)
        far_bias = rb_ref[REL_BUCKETS - 1, h]
        qh = q[:, cols]
        q8 = jnp.broadcast_to(qh, (SUBLANES, HEAD_DIM)).astype(BF16)
        l1 = []
        for j in range(MOBA_TOPK):
            blk_j = selected_block(bi, t, j, h)
            for pg in range(PAGES_PER_BLOCK):
                k_t = kbuf[slot, h, j * PAGES_PER_BLOCK + pg].astype(BF16)
                near = _bias_from_buckets(pb_ref[pl.ds(t * PAGES_PER_BLOCK + pg, 1), :], rb_ref, h)
                l1.append(jnp.dot(q8, k_t, preferred_element_type=F32)[0:1]
                          + jnp.where(blk_j == last_blk, near, far_bias))
        l1 = jnp.concatenate(l1, axis=1)
        l2 = jnp.sum(kn_ref[0, :, cols] * qh, axis=-1, keepdims=True)
        bias2 = jnp.zeros((SUBLANES, 1), F32)
        for u in range(t_new):
            bias2 = jnp.where(row == u, rb_ref[jnp.maximum(t - u, 0), h], bias2)
        l2 = jnp.where(row <= t, l2 + bias2, NEG_INF)
        m = jnp.maximum(jnp.max(l1, axis=-1, keepdims=True), jnp.max(l2, axis=0, keepdims=True))
        p1 = jnp.exp(l1 - m)
        p2 = jnp.exp(l2 - m)
        denom = jnp.sum(p1, axis=-1, keepdims=True) + jnp.sum(p2, axis=0, keepdims=True)
        p8 = jnp.broadcast_to(p1, (SUBLANES, SLABS_PER_HEAD * PAGE_SIZE)).astype(BF16)
        o = jnp.sum(p2 * vn_ref[0, :, cols], axis=0, keepdims=True)
        for i in range(SLABS_PER_HEAD):
            v_t = vbuf[slot, h, i].astype(BF16)
            o = o + lax.dot_general(p8[:, i * PAGE_SIZE:(i + 1) * PAGE_SIZE], v_t, (((1,), (1,)), ((), ())),
                                    preferred_element_type=F32)[0:1]
        o_scr[:, cols] = jnp.broadcast_to(o / denom, (SUBLANES, HEAD_DIM))
    o_ref[0, 0] = o_scr[0:1, :]


def _sample_attn(q, k_new, v_new, cache_k, cache_v, sel, page_table, rel_bias):
    b, t_new, _ = q.shape
    npg = page_table.shape[1]
    pad = ((0, 0), (0, SUBLANES - t_new), (0, 0))
    tt = jnp.arange(t_new, dtype=I32)[:, None, None]
    pg = jnp.arange(PAGES_PER_BLOCK, dtype=I32)[None, :, None]
    rr = jnp.arange(PAGE_SIZE, dtype=I32)[None, None, :]
    near_bkt = _rel_bucket(MOBA_BLOCK + tt - pg * PAGE_SIZE - rr).reshape(t_new * PAGES_PER_BLOCK, PAGE_SIZE)

    new_spec = pl.BlockSpec((1, SUBLANES, D_MODEL), lambda bi, t, *_: (bi, 0, 0))
    tok_spec = pl.BlockSpec((1, 1, 1, D_MODEL), lambda bi, t, *_: (bi, t, 0, 0))
    slab_buf = pltpu.VMEM((2, N_HEADS, SLABS_PER_HEAD, HEAD_DIM, PAGE_SIZE), F32)
    out = pl.pallas_call(
        functools.partial(_sample_attn_kernel, t_new=t_new, npg=npg),
        grid_spec=pltpu.PrefetchScalarGridSpec(
            num_scalar_prefetch=2,
            grid=(b, t_new),
            in_specs=[_smem(), tok_spec, new_spec, new_spec,
                      pl.BlockSpec((t_new * PAGES_PER_BLOCK, PAGE_SIZE), lambda *_: (0, 0)),
                      pl.BlockSpec(memory_space=pl.ANY), pl.BlockSpec(memory_space=pl.ANY)],
            out_specs=tok_spec,
            scratch_shapes=[slab_buf, slab_buf, pltpu.SemaphoreType.DMA((2, 2)),
                            pltpu.VMEM((SUBLANES, D_MODEL), F32)],
        ),
        out_shape=jax.ShapeDtypeStruct((b, t_new, 1, D_MODEL), F32),
        compiler_params=_params(2),
        name="sample_attn",
    )(sel.reshape(-1), page_table.reshape(-1), rel_bias,
      q.reshape(b, t_new, 1, D_MODEL), jnp.pad(k_new, pad), jnp.pad(v_new, pad), near_bkt,
      cache_k, cache_v)
    return out.reshape(b, t_new, D_MODEL)


def _ssd_chunk(t, ext_ref, p_ref, st_ref, gy_ref, cw_ref, cb_ref, dtb_ref, alog_ref, dsk_ref,
               nw_ref, e_ref, wout_ref, n_valid):
    def conv_act(lo, width):
        acc = cb_ref[:, lo:lo + width]
        for j in range(CONV_WIDTH):
            acc = acc + cw_ref[j:j + 1, lo:lo + width] * ext_ref[5 + j:5 + j + t, lo:lo + width]
        return _silu(acc)

    dt = _softplus(p_ref[:, DT_OFF:DT_OFF + LANES] + dtb_ref[...])
    if n_valid < t:
        dt = jnp.where(lax.broadcasted_iota(I32, (t, LANES), 0) < n_valid, dt, 0.0)
    a = dt * -jnp.exp(alog_ref[...])
    r_i = lax.broadcasted_iota(I32, (t, t), 0)
    c_i = lax.broadcasted_iota(I32, (t, t), 1)
    causal = r_i >= c_i
    acs = jnp.dot(causal.astype(F32), a, preferred_element_type=F32,
                  precision=lax.Precision.HIGHEST)
    if t % LANES == 0:
        acs_t = acs.T
    else:
        acs_t = lax.dot_general(jnp.eye(LANES, dtype=F32), acs, (((1,), (1,)), ((), ())),
                                preferred_element_type=F32, precision=lax.Precision.HIGHEST)
    lane_hi = lax.broadcasted_iota(I32, (t, LANES), 1) >= SSM_HEAD_DIM

    for g in range(SSM_GROUPS):
        ch = slice(g * GROUP_DIM, (g + 1) * GROUP_DIM)
        e_g = e_ref[:, ch]
        xs = conv_act(g * GROUP_DIM, GROUP_DIM)
        b_g = conv_act(D_INNER + g * SSM_STATE, SSM_STATE).astype(BF16)
        c_g = conv_act(D_INNER + (SSM_GROUPS + g) * SSM_STATE, SSM_STATE).astype(BF16)
        dt_g = _select_matmul(dt, e_g)
        acs_g = _select_matmul(acs, e_g)
        last_g = acs_g[t - 1:t, :]
        xdt = xs * dt_g
        xdt_b = xdt.astype(BF16)
        xdt_dec = (xdt * jnp.exp(last_g - acs_g)).astype(BF16)
        cb = lax.dot_general(c_g, b_g, (((1,), (1,)), ((), ())), preferred_element_type=F32)
        st_in = st_ref[:, ch]
        y_off = jnp.dot(c_g, st_in.astype(BF16), preferred_element_type=F32) * jnp.exp(acs_g)
        st_new = lax.dot_general(b_g, xdt_dec, (((0,), (0,)), ((), ())), preferred_element_type=F32)
        st_ref[:, ch] = st_in * jnp.exp(last_g) + st_new
        for j in range(HEADS_PER_GROUP // HEADS_PER_SLAB):
            cols = slice(j * LANES, (j + 1) * LANES)
            pair = xdt_b[:, cols]
            ys = []
            for hh in range(HEADS_PER_SLAB):
                h = g * HEADS_PER_GROUP + j * HEADS_PER_SLAB + hh
                decay = jnp.exp(jnp.where(causal, acs[:, h:h + 1] - acs_t[h:h + 1, :], NEG_INF))
                ys.append(jnp.dot((cb * decay).astype(BF16), pair, preferred_element_type=F32))
            y_diag = jnp.where(lane_hi, ys[1], ys[0])
            lo = g * GROUP_DIM + j * LANES
            y = y_diag + y_off[:, cols] + dsk_ref[:, lo:lo + LANES] * xs[:, cols]
            gy_ref[:, lo:lo + LANES] = y * _silu(p_ref[:, lo:lo + LANES])
        gy = gy_ref[:, ch]
        ms = jnp.mean(gy * gy, axis=-1, keepdims=True)
        gy_ref[:, ch] = gy * lax.rsqrt(ms + RMS_EPS) * nw_ref[:, ch]
    return jnp.dot(gy_ref[...].astype(BF16), wout_ref[...], preferred_element_type=F32)


def _ssm_prompt_kernel(p_ref, xres_ref, cw_ref, cb_ref, dtb_ref, alog_ref, dsk_ref, nw_ref, e_ref,
                       wout_ref, y_ref, st_out_ref, cv_out_ref, ext_ref, st_ref, gy_ref):
    c = pl.program_id(0)
    t = SSD_CHUNK

    @pl.when(c == 0)
    def _init():
        ext_ref[t:t + SUBLANES, :] = jnp.zeros((SUBLANES, CONV_DIM), F32)
        st_ref[...] = jnp.zeros_like(st_ref)

    ext_ref[0:SUBLANES, :] = ext_ref[t:t + SUBLANES, :]
    ext_ref[SUBLANES:SUBLANES + t, :] = p_ref[:, D_INNER:D_INNER + CONV_DIM]
    out = _ssd_chunk(t, ext_ref, p_ref, st_ref, gy_ref, cw_ref, cb_ref, dtb_ref, alog_ref, dsk_ref,
                     nw_ref, e_ref, wout_ref, t)
    y_ref[...] = xres_ref[...] + out

    @pl.when(c == pl.num_programs(0) - 1)
    def _finish():
        st_out_ref[...] = st_ref[...].T
        cv_out_ref[...] = ext_ref[t:t + SUBLANES, :]


def _ssm_consts(w):
    conv_w, conv_b, dt_bias, a_log, d_skip, norm_w, w_out = w
    lane_pad = (0, LANES - SSM_HEADS)
    expand = (jnp.arange(LANES, dtype=I32)[:, None]
              == jnp.arange(D_INNER, dtype=I32)[None, :] // SSM_HEAD_DIM).astype(BF16)
    return (conv_w, conv_b.reshape(1, CONV_DIM),
            jnp.pad(dt_bias, lane_pad).reshape(1, LANES), jnp.pad(a_log, lane_pad).reshape(1, LANES),
            jnp.repeat(d_skip, SSM_HEAD_DIM).reshape(1, D_INNER), norm_w.reshape(1, D_INNER),
            expand, w_out)


def _const_specs(consts):
    return [_whole(c.shape) for c in consts]


def _ssm_prompt(proj, xres, w):
    s = proj.shape[0]
    t = SSD_CHUNK
    consts = _ssm_consts(w)
    y, st, cv = pl.pallas_call(
        _ssm_prompt_kernel,
        grid=(s // t,),
        in_specs=[pl.BlockSpec((t, SSM_PROJ_PAD), lambda c: (c, 0)),
                  pl.BlockSpec((t, D_MODEL), lambda c: (c, 0))] + _const_specs(consts),
        out_specs=[pl.BlockSpec((t, D_MODEL), lambda c: (c, 0)),
                   _whole((D_INNER, SSM_STATE)), _whole((SUBLANES, CONV_DIM))],
        out_shape=[jax.ShapeDtypeStruct((s, D_MODEL), F32),
                   jax.ShapeDtypeStruct((D_INNER, SSM_STATE), F32),
                   jax.ShapeDtypeStruct((SUBLANES, CONV_DIM), F32)],
        scratch_shapes=[pltpu.VMEM((t + 2 * SUBLANES, CONV_DIM), F32),
                        pltpu.VMEM((SSM_STATE, D_INNER), F32),
                        pltpu.VMEM((t, D_INNER), F32)],
        compiler_params=_params(1),
        name="ssm_prompt",
    )(proj, xres, *consts)
    return y, st, cv[SUBLANES - (CONV_WIDTH - 1):]


def _ssm_sample_kernel(p_ref, xres_ref, cst_ref, st_in_ref, cw_ref, cb_ref, dtb_ref, alog_ref,
                       dsk_ref, nw_ref, e_ref, wout_ref, y_ref, st_out_ref, cv_out_ref,
                       ext_ref, st_ref, gy_ref, *, t_new):
    t = SAMPLE_T
    ext_ref[0:SUBLANES, :] = cst_ref[0]
    ext_ref[SUBLANES:SUBLANES + t, :] = p_ref[0, :, D_INNER:D_INNER + CONV_DIM]
    st_ref[...] = st_in_ref[0].T
    out = _ssd_chunk(t, ext_ref, p_ref.at[0], st_ref, gy_ref, cw_ref, cb_ref, dtb_ref, alog_ref,
                     dsk_ref, nw_ref, e_ref, wout_ref, t_new)
    y_ref[0] = xres_ref[0] + out[0:t_new]
    st_out_ref[0] = st_ref[...].T
    cv_out_ref[0] = ext_ref[SUBLANES:SUBLANES + t, :]


def _ssm_sample(proj, xres, conv_state, ssm_state, w):
    b, t_new, _ = proj.shape
    t = SAMPLE_T
    consts = _ssm_consts(w)
    proj8 = jnp.pad(proj, ((0, 0), (0, t - t_new), (0, 0)))
    cst8 = jnp.pad(conv_state, ((0, 0), (SUBLANES - (CONV_WIDTH - 1), 0), (0, 0)))
    y, st, cv = pl.pallas_call(
        functools.partial(_ssm_sample_kernel, t_new=t_new),
        grid=(b,),
        in_specs=[pl.BlockSpec((1, t, SSM_PROJ_PAD), lambda i: (i, 0, 0)),
                  pl.BlockSpec((1, t_new, D_MODEL), lambda i: (i, 0, 0)),
                  pl.BlockSpec((1, SUBLANES, CONV_DIM), lambda i: (i, 0, 0)),
                  pl.BlockSpec((1, D_INNER, SSM_STATE), lambda i: (i, 0, 0))] + _const_specs(consts),
        out_specs=[pl.BlockSpec((1, t_new, D_MODEL), lambda i: (i, 0, 0)),
                   pl.BlockSpec((1, D_INNER, SSM_STATE), lambda i: (i, 0, 0)),
                   pl.BlockSpec((1, t, CONV_DIM), lambda i: (i, 0, 0))],
        out_shape=[jax.ShapeDtypeStruct((b, t_new, D_MODEL), F32),
                   jax.ShapeDtypeStruct((b, D_INNER, SSM_STATE), F32),
                   jax.ShapeDtypeStruct((b, t, CONV_DIM), F32)],
        scratch_shapes=[pltpu.VMEM((2 * SUBLANES, CONV_DIM), F32),
                        pltpu.VMEM((SSM_STATE, D_INNER), F32),
                        pltpu.VMEM((t, D_INNER), F32)],
        compiler_params=_params(1),
        name="ssm_sample",
    )(proj8, xres, cst8, ssm_state, *consts)
    return y, st, cv[:, t_new - (CONV_WIDTH - 1):t_new]


def _ffn_weights(w_in, w_out):
    def chunks(w):
        return w.reshape(D_MODEL, N_FF_CHUNKS, FF_CHUNK).transpose(1, 0, 2).astype(BF16)
    return (chunks(w_in[:, :D_FF]), chunks(w_in[:, D_FF:]),
            w_out.reshape(N_FF_CHUNKS, FF_CHUNK, D_MODEL).astype(BF16))


def _row_tile(rows, want=512):
    return want if rows % want == 0 else 128


def kernel(x_prompt, x_sample, cache_k, cache_v, page_table, state_ssm, state_conv, ffn1_norm, ffn1_w_in, ffn1_w_out, mix_norm, ffn2_norm, ffn2_w_in, ffn2_w_out, attn_w_qkv, attn_w_o, rel_bias, ssm_w_in, ssm_conv_w, ssm_conv_b, ssm_dt_bias, ssm_A_log, ssm_D, ssm_norm, ssm_w_out, final_norm):
    bp, s, _ = x_prompt.shape
    bs, t_new, _ = x_sample.shape
    assert bp == 1 and attn_w_qkv.shape[0] == 1 and ssm_w_in.shape[0] == 1
    xp = x_prompt.reshape(s, D_MODEL)
    xs = x_sample.reshape(bs * t_new, D_MODEL)
    tp, ts = _row_tile(s), _row_tile(bs * t_new)
    tf = _row_tile(s, 1024)

    w1 = _ffn_weights(ffn1_w_in[0], ffn1_w_out[0])
    xp = _ffn(xp, ffn1_norm[0], w1, tf)
    xs = _ffn(xs, ffn1_norm[0], w1, ts)

    wqkv = attn_w_qkv[0].astype(BF16)
    wq, wk, wv = (wqkv[:, i * D_MODEL:(i + 1) * D_MODEL] for i in range(3))
    wo = attn_w_o[0].astype(BF16)
    ck = jnp.transpose(cache_k[0], (0, 2, 3, 1))
    cv = jnp.transpose(cache_v[0], (0, 2, 3, 1))
    k_p, v_p, kb, qt, vt, ksum = _qkv_prompt(xp, mix_norm[0], wq.T, wk, wk.T, wv.T, tp)
    o_p, bsum = _moba_prompt(qt, kb, vt, ksum.reshape(s // MOBA_BLOCK, D_MODEL), rel_bias, ck, page_table)
    if bsum is None:
        bsum = _page_sums(ck, page_table)
    xp = _matres(o_p, wo, xp, tp)

    qkv_s = _proj(xs, mix_norm[0], wqkv, ts).reshape(bs, t_new, 3 * D_MODEL)
    q_s, k_s, v_s = (qkv_s[..., i * D_MODEL:(i + 1) * D_MODEL] for i in range(3))
    sel = _sample_gate(q_s, bsum)
    o_s = _sample_attn(q_s, k_s, v_s, ck, cv, sel, page_table, rel_bias)
    xs = _matres(o_s.reshape(bs * t_new, D_MODEL), wo, xs, ts)

    w2 = _ffn_weights(ffn2_w_in[0], ffn2_w_out[0])
    xp = _ffn(xp, ffn2_norm[0], w2, tf)
    xs = _ffn(xs, ffn2_norm[0], w2, ts)

    w1 = _ffn_weights(ffn1_w_in[1], ffn1_w_out[1])
    xp = _ffn(xp, ffn1_norm[1], w1, tf)
    xs = _ffn(xs, ffn1_norm[1], w1, ts)

    w_in = jnp.pad(ssm_w_in[0], ((0, 0), (0, SSM_PROJ_PAD - SSM_IN_DIM))).astype(BF16)
    ssm_w = (ssm_conv_w[0], ssm_conv_b[0], ssm_dt_bias[0], ssm_A_log[0], ssm_D[0], ssm_norm[0],
             ssm_w_out[0].astype(BF16))
    xp, st_p, cv_p = _ssm_prompt(_proj(xp, mix_norm[1], w_in, tp), xp, ssm_w)
    proj_s = _proj(xs, mix_norm[1], w_in, ts).reshape(bs, t_new, SSM_PROJ_PAD)
    xs3, st_s, cv_s = _ssm_sample(proj_s, xs.reshape(bs, t_new, D_MODEL), state_conv[0],
                                  state_ssm[0].reshape(bs, D_INNER, SSM_STATE), ssm_w)
    xs = xs3.reshape(bs * t_new, D_MODEL)

    w2 = _ffn_weights(ffn2_w_in[1], ffn2_w_out[1])
    y_p = _ffn(xp, ffn2_norm[1], w2, tf, final_g=final_norm)
    y_s = _ffn(xs, ffn2_norm[1], w2, ts, final_g=final_norm)

    head_shape = (N_HEADS, HEAD_DIM)
    state_shape = (SSM_HEADS, SSM_HEAD_DIM, SSM_STATE)
    return (y_p.reshape(1, s, D_MODEL), y_s.reshape(bs, t_new, D_MODEL),
            jnp.transpose(k_p, (0, 3, 1, 2))[None, None], jnp.transpose(v_p, (0, 3, 1, 2))[None, None],
            k_s.reshape(1, bs, t_new, *head_shape), v_s.reshape(1, bs, t_new, *head_shape),
            st_p.reshape(1, 1, *state_shape), cv_p.reshape(1, 1, CONV_WIDTH - 1, CONV_DIM),
            st_s.reshape(1, bs, *state_shape), cv_s.reshape(1, bs, CONV_WIDTH - 1, CONV_DIM))
```

```python
import functools
import math

import jax
import jax.numpy as jnp
from jax import lax
from jax.experimental import pallas as pl
from jax.experimental.pallas import tpu as pltpu

F32 = jnp.float32
BF16 = jnp.bfloat16
I32 = jnp.int32
NEG_INF = float("-inf")
LOG2_E = math.log2(math.e)

D_MODEL = 1024
N_HEADS = 16
HEAD_DIM = 64
MOBA_BLOCK = 256
MOBA_TOPK = 3
PAGE_SIZE = 128
REL_BUCKETS = 32
REL_MAX_DIST = 128
D_INNER = 2048
SSM_HEADS = 32
SSM_HEAD_DIM = 64
SSM_GROUPS = 4
SSM_STATE = 128
CONV_WIDTH = 4
CONV_DIM = D_INNER + 2 * SSM_GROUPS * SSM_STATE
SSM_IN_DIM = 2 * D_INNER + 2 * SSM_GROUPS * SSM_STATE + SSM_HEADS
SSD_CHUNK = 256
D_FF = 2816
RMS_EPS = 1e-6

LANES = 128
SUBLANES = 8
VMEM_LIMIT_BYTES = 56 * 1024 * 1024

FF_CHUNK = 256
N_FF_CHUNKS = D_FF // FF_CHUNK
HEADS_PER_SLAB = LANES // HEAD_DIM
GROUP_DIM = D_INNER // SSM_GROUPS
HEADS_PER_GROUP = SSM_HEADS // SSM_GROUPS
SSM_PROJ_PAD = -(-SSM_IN_DIM // LANES) * LANES
DT_OFF = D_INNER + CONV_DIM
SAMPLE_T = 8
PAGES_PER_BLOCK = MOBA_BLOCK // PAGE_SIZE
KV_GROUP = 4
ATTN_HEADS = 4
ATTN_WIDTH = ATTN_HEADS * HEAD_DIM
SUM_ROWS = 16


def _params(n_axes, vmem=VMEM_LIMIT_BYTES):
    return pltpu.CompilerParams(dimension_semantics=("arbitrary",) * n_axes,
                                vmem_limit_bytes=vmem)


def _whole(shape):
    nd = len(shape)
    return pl.BlockSpec(shape, lambda *_: (0,) * nd)


def _smem():
    return pl.BlockSpec(memory_space=pltpu.SMEM)


def _rms(x, g):
    ms = jnp.mean(x * x, axis=-1, keepdims=True)
    return x * lax.rsqrt(ms + RMS_EPS) * g


def _silu(x):
    return x * jax.nn.sigmoid(x)


def _softplus(x):
    return jnp.maximum(x, 0.0) + jnp.log1p(jnp.exp(-jnp.abs(x)))


def _split3(x):
    hi = x.astype(BF16)
    r1 = x - hi.astype(F32)
    mid = r1.astype(BF16)
    lo = (r1 - mid.astype(F32)).astype(BF16)
    return hi, mid, lo


def _select_matmul(x, sel):
    hi, mid, lo = _split3(x)
    dot = functools.partial(jnp.dot, preferred_element_type=F32)
    return dot(hi, sel) + dot(mid, sel) + dot(lo, sel)


def _top3_rows(s, n_iota, sentinel):
    picks = []
    for _ in range(MOBA_TOPK):
        m = jnp.max(s, axis=0, keepdims=True)
        idx = jnp.min(jnp.where(s == m, n_iota, sentinel), axis=0, keepdims=True)
        picks.append((idx, m))
        s = jnp.where(n_iota == idx, NEG_INF, s)
    return picks


def _bias_from_buckets(bk, rb_ref, h):
    out = jnp.full(bk.shape, NEG_INF, F32)
    for b in range(REL_BUCKETS):
        out = jnp.where(bk == b, rb_ref[b, h], out)
    return out


def _rel_bucket(dist):
    n = jnp.maximum(dist, 0)
    max_exact = REL_BUCKETS // 2
    nf = jnp.maximum(n, 1).astype(F32)
    large = max_exact + (jnp.log(nf / max_exact) / math.log(REL_MAX_DIST / max_exact)
                         * (REL_BUCKETS - max_exact)).astype(I32)
    large = jnp.minimum(large, REL_BUCKETS - 1)
    return jnp.where(n < max_exact, n, large).astype(I32)


def _ffn_kernel(*refs, final):
    if final:
        x_ref, g_ref, wa_ref, wb_ref, wo_ref, fg_ref, o_ref, xn_ref, acc_ref = refs
    else:
        x_ref, g_ref, wa_ref, wb_ref, wo_ref, o_ref, xn_ref, acc_ref = refs
    xn_ref[...] = _rms(x_ref[...], g_ref[...]).astype(BF16)
    acc_ref[...] = jnp.zeros_like(acc_ref)

    def chunk(c, carry):
        xn = xn_ref[...]
        a = jnp.dot(xn, wa_ref[c], preferred_element_type=F32)
        b = jnp.dot(xn, wb_ref[c], preferred_element_type=F32)
        h = (_silu(a) * b).astype(BF16)
        acc_ref[...] += jnp.dot(h, wo_ref[c], preferred_element_type=F32)
        return carry

    lax.fori_loop(0, N_FF_CHUNKS, chunk, 0)
    y = x_ref[...] + 0.5 * acc_ref[...]
    if final:
        y = _rms(y, fg_ref[...])
    o_ref[...] = y


def _ffn(x, g, w, tm, final_g=None):
    rows = x.shape[0]
    wa, wb, wo = w
    row_spec = pl.BlockSpec((tm, D_MODEL), lambda i: (i, 0))
    in_specs = [row_spec, _whole((1, D_MODEL)), _whole(wa.shape), _whole(wb.shape), _whole(wo.shape)]
    args = [x, g.reshape(1, D_MODEL), wa, wb, wo]
    if final_g is not None:
        in_specs.append(_whole((1, D_MODEL)))
        args.append(final_g.reshape(1, D_MODEL))
    return pl.pallas_call(
        functools.partial(_ffn_kernel, final=final_g is not None),
        grid=(rows // tm,),
        in_specs=in_specs,
        out_specs=row_spec,
        out_shape=jax.ShapeDtypeStruct((rows, D_MODEL), F32),
        scratch_shapes=[pltpu.VMEM((tm, D_MODEL), BF16), pltpu.VMEM((tm, D_MODEL), F32)],
        compiler_params=_params(1),
        name="ffn",
    )(*args)


def _proj_kernel(x_ref, g_ref, w_ref, o_ref, xn_ref, *, widths):
    xn_ref[...] = _rms(x_ref[...], g_ref[...]).astype(BF16)
    lo = 0
    for wd in widths:
        o_ref[:, lo:lo + wd] = jnp.dot(xn_ref[...], w_ref[:, lo:lo + wd], preferred_element_type=F32)
        lo += wd


def _proj(x, g, w, tm):
    rows, n = x.shape[0], w.shape[1]
    col_chunk = 4 * LANES
    widths = [col_chunk] * (n // col_chunk) + ([n % col_chunk] if n % col_chunk else [])
    return pl.pallas_call(
        functools.partial(_proj_kernel, widths=tuple(widths)),
        grid=(rows // tm,),
        in_specs=[pl.BlockSpec((tm, D_MODEL), lambda i: (i, 0)), _whole((1, D_MODEL)), _whole(w.shape)],
        out_specs=pl.BlockSpec((tm, n), lambda i: (i, 0)),
        out_shape=jax.ShapeDtypeStruct((rows, n), F32),
        scratch_shapes=[pltpu.VMEM((tm, D_MODEL), BF16)],
        compiler_params=_params(1),
        name="norm_proj",
    )(x, g.reshape(1, D_MODEL), w)


def _matres_kernel(a_ref, w_ref, r_ref, o_ref):
    o_ref[...] = r_ref[...] + jnp.dot(a_ref[...].astype(BF16), w_ref[...], preferred_element_type=F32)


def _matres(a, w, res, tm):
    rows, k = a.shape
    n = w.shape[1]
    return pl.pallas_call(
        _matres_kernel,
        grid=(rows // tm,),
        in_specs=[pl.BlockSpec((tm, k), lambda i: (i, 0)), _whole(w.shape),
                  pl.BlockSpec((tm, n), lambda i: (i, 0))],
        out_specs=pl.BlockSpec((tm, n), lambda i: (i, 0)),
        out_shape=jax.ShapeDtypeStruct((rows, n), F32),
        compiler_params=_params(1),
        name="matmul_residual",
    )(a, w, res)


def _qkv_prompt_kernel(x_ref, g_ref, wqt_ref, wk_ref, wkt_ref, wvt_ref,
                       kp_ref, vp_ref, kb_ref, qt_ref, vt_ref, ks_ref, xn_ref, *, tm):
    xn_ref[...] = _rms(x_ref[...], g_ref[...]).astype(BF16)
    xn = xn_ref[...]
    k = jnp.dot(xn, wk_ref[...], preferred_element_type=F32)
    kb_ref[...] = k.astype(BF16)
    for j in range(tm // MOBA_BLOCK):
        ks_ref[0, j:j + 1, :] = jnp.sum(k[j * MOBA_BLOCK:(j + 1) * MOBA_BLOCK], axis=0, keepdims=True)
    nt = (((1,), (1,)), ((), ()))
    qt = lax.dot_general(wqt_ref[...], xn, nt, preferred_element_type=F32)
    qt_ref[...] = (qt * (LOG2_E * HEAD_DIM ** -0.5)).astype(BF16)
    kt = lax.dot_general(wkt_ref[...], xn, nt, preferred_element_type=F32)
    vt = lax.dot_general(wvt_ref[...], xn, nt, preferred_element_type=F32)
    for j in range(tm // MOBA_BLOCK):
        vt_ref[j] = vt[:, j * MOBA_BLOCK:(j + 1) * MOBA_BLOCK].astype(BF16)
    for pg in range(tm // PAGE_SIZE):
        rows = slice(pg * PAGE_SIZE, (pg + 1) * PAGE_SIZE)
        kp_ref[pg] = kt[:, rows].reshape(N_HEADS, HEAD_DIM, PAGE_SIZE)
        vp_ref[pg] = vt[:, rows].reshape(N_HEADS, HEAD_DIM, PAGE_SIZE)


def _qkv_prompt(x, g, wqt, wk, wkt, wvt, tm):
    s = x.shape[0]
    nblk = s // MOBA_BLOCK
    bpt = tm // MOBA_BLOCK
    ppt = tm // PAGE_SIZE
    row_spec = pl.BlockSpec((tm, D_MODEL), lambda i: (i, 0))
    page_spec = pl.BlockSpec((ppt, N_HEADS, HEAD_DIM, PAGE_SIZE), lambda i: (i, 0, 0, 0))
    pages = jax.ShapeDtypeStruct((s // PAGE_SIZE, N_HEADS, HEAD_DIM, PAGE_SIZE), F32)
    sq = _whole((D_MODEL, D_MODEL))
    return pl.pallas_call(
        functools.partial(_qkv_prompt_kernel, tm=tm),
        grid=(s // tm,),
        in_specs=[row_spec, _whole((1, D_MODEL)), sq, sq, sq, sq],
        out_specs=[page_spec, page_spec, row_spec,
                   pl.BlockSpec((D_MODEL, tm), lambda i: (0, i)),
                   pl.BlockSpec((bpt, D_MODEL, MOBA_BLOCK), lambda i: (i, 0, 0)),
                   pl.BlockSpec((1, bpt, D_MODEL), lambda i: (i, 0, 0))],
        out_shape=[pages, pages,
                   jax.ShapeDtypeStruct((s, D_MODEL), BF16),
                   jax.ShapeDtypeStruct((D_MODEL, s), BF16),
                   jax.ShapeDtypeStruct((nblk, D_MODEL, MOBA_BLOCK), BF16),
                   jax.ShapeDtypeStruct((s // tm, bpt, D_MODEL), F32)],
        scratch_shapes=[pltpu.VMEM((tm, D_MODEL), BF16)],
        compiler_params=_params(1),
        name="qkv_prompt",
    )(x, g.reshape(1, D_MODEL), wqt, wk, wkt, wvt)


def _block_sums(pages, o_ref):
    for i in range(len(pages) // PAGES_PER_BLOCK):
        tot = pages[PAGES_PER_BLOCK * i][0]
        for j in range(1, PAGES_PER_BLOCK):
            tot = tot + pages[PAGES_PER_BLOCK * i + j][0]
        o_ref[0, i] = jnp.sum(tot, axis=-1)


def _page_specs(n, first_page):
    def spec(i):
        return pl.BlockSpec((1, N_HEADS, HEAD_DIM, PAGE_SIZE),
                            lambda *a: (a[-1][first_page(*a[:-1]) + i], 0, 0, 0))
    return [spec(i) for i in range(n)]


def _softmax_group_step(tiles, v_tiles, state):
    m, acc = state
    m_new = m
    for s, keep, shift in tiles:
        bm = jnp.max(s, axis=0, keepdims=True) + shift
        m_new = jnp.maximum(m_new, bm if keep is None else jnp.where(keep > 0.0, bm, NEG_INF))
    acc = jnp.exp2(m - m_new) * acc
    for (s, keep, shift), v_t in zip(tiles, v_tiles):
        off = m_new - shift
        if keep is not None:
            off = jnp.where(keep > 0.0, off, jnp.inf)
        p = jnp.exp2(s - off)
        acc = acc + jnp.dot(v_t, p.astype(BF16), preferred_element_type=F32)
    return m_new, acc


def _moba_prompt_kernel(*refs, nblk, pps):
    if pps:
        refs = refs[1:]
    rb_ref, qt_ref, kb_ref, vt_ref, ks_ref, bkt_ref = refs[:6]
    pages = refs[6:6 + pps]
    o_ref = refs[6 + pps]
    sel_ref, bias_ref, sa_ref, sb_ref = refs[-4:]
    slab = pl.program_id(0)
    qi = pl.program_id(1)
    blk = MOBA_BLOCK
    heads = range(ATTN_HEADS)

    @pl.when(qi == 0)
    def _build_bias():
        for hh in heads:
            for kind in range(2):
                bias_ref[hh, kind] = LOG2_E * _bias_from_buckets(bkt_ref[kind], rb_ref,
                                                                 slab * ATTN_HEADS + hh)

    if pps:
        _block_sums(pages, refs[7 + pps])
    qt = qt_ref[...].astype(F32)
    row_head = lax.broadcasted_iota(I32, (ATTN_WIDTH, blk), 0) // HEAD_DIM
    n_iota = lax.broadcasted_iota(I32, (nblk, blk), 0)
    kmean = (ks_ref[...] * (1.0 / blk)).astype(BF16)
    far_bias = [LOG2_E * rb_ref[REL_BUCKETS - 1, slab * ATTN_HEADS + hh] for hh in heads]
    qz = [jnp.where(row_head == hh, qt, 0.0).astype(BF16) for hh in heads]
    for hh in heads:
        scores = jnp.dot(kmean, qz[hh], preferred_element_type=F32)
        scores = jnp.where(n_iota < qi, scores, NEG_INF)
        sel = jnp.zeros((nblk, blk), F32)
        for idx, m in _top3_rows(scores, n_iota, nblk):
            sel = jnp.where((n_iota == idx) & (m > NEG_INF), 1.0, sel)
        sel_ref[hh] = sel

    def fill(buf, base):
        for j in range(KV_GROUP):
            nc = jnp.maximum(base + j, 0)
            k_n = kb_ref[pl.ds(pl.multiple_of(nc * blk, blk), blk), :]
            for hh in heads:
                buf[hh * KV_GROUP + j] = jnp.dot(k_n, qz[hh], preferred_element_type=F32)

    def fill_far(buf, base):
        first = jnp.maximum(base, 0)
        k_g = kb_ref[pl.ds(pl.multiple_of(first * blk, blk), KV_GROUP * blk), :]
        for hh in heads:
            buf[hh * KV_GROUP:(hh + 1) * KV_GROUP] = jnp.dot(
                k_g, qz[hh], preferred_element_type=F32).reshape(KV_GROUP, blk, blk)

    def consume(buf, base, kinds, states, first=None):
        out = []
        for hh in heads:
            tiles, v_tiles = [], []
            for j, kind in enumerate(kinds):
                n = base + j if first is None else first + j
                nc = jnp.maximum(n, 0)
                s = buf[hh * KV_GROUP + j]
                v_tiles.append(jnp.concatenate([vt_ref[nc, hh * HEAD_DIM:(hh + 1) * HEAD_DIM, :], ones], axis=0))
                if kind == "own":
                    tiles.append((s + bias_ref[hh, 0], None, 0.0))
                    continue
                inside = (n >= 0) if first is None else (n < base + KV_GROUP)
                keep = jnp.where(inside, sel_ref[hh, pl.ds(nc, 1), :], 0.0)
                if kind == "prev":
                    tiles.append((s + bias_ref[hh, 1], keep, 0.0))
                else:
                    tiles.append((s, keep, far_bias[hh]))
            out.append(_softmax_group_step(tiles, v_tiles, states[hh]))
        return tuple(out)

    near_base = qi - (KV_GROUP - 1)
    near_kinds = ["far"] * (KV_GROUP - 2) + ["prev", "own"]
    far_kinds = ["far"] * KV_GROUP

    def far_base(g):
        return near_base - KV_GROUP * (g + 1)

    ones = jnp.ones((SUM_ROWS, blk), BF16)
    init = tuple((jnp.full((1, blk), NEG_INF, F32), jnp.zeros((HEAD_DIM + SUM_ROWS, blk), F32))
                 for _ in heads)
    def consume_far(buf, g, states):
        return consume(buf, far_base(g), far_kinds, states, first=jnp.maximum(far_base(g), 0))

    fill(sa_ref, near_base)
    fill_far(sb_ref, far_base(0))
    states = consume(sa_ref, near_base, near_kinds, init)

    def far_pair(i, states):
        g = 2 * i
        fill_far(sa_ref, far_base(g + 1))
        states = consume_far(sb_ref, g, states)
        fill_far(sb_ref, far_base(g + 2))
        return consume_far(sa_ref, g + 1, states)

    n_far_groups = lax.div(jnp.maximum(near_base, 0) + KV_GROUP - 1, KV_GROUP)
    n_pairs = lax.div(n_far_groups, 2)
    states = lax.fori_loop(0, n_pairs, far_pair, states)
    states = lax.cond(lax.rem(n_far_groups, 2) == 1,
                      lambda st: consume_far(sb_ref, 2 * n_pairs, st),
                      lambda st: st, states)
    outs = [acc[:HEAD_DIM] / acc[HEAD_DIM:HEAD_DIM + 1] for _, acc in states]
    o_ref[...] = jnp.concatenate(outs, axis=0).T.astype(BF16)


def _prompt_bucket_maps():
    kk = jnp.arange(MOBA_BLOCK, dtype=I32)[:, None]
    tt = jnp.arange(MOBA_BLOCK, dtype=I32)[None, :]
    own = jnp.where(tt - kk >= 0, _rel_bucket(tt - kk), -1)
    prev = _rel_bucket(MOBA_BLOCK + tt - kk)
    return jnp.stack([own, prev]).astype(I32)


def _moba_prompt(qt, kb, vt, ksum, rel_bias, cache=None, page_table=None):
    s = kb.shape[0]
    nblk = s // MOBA_BLOCK
    grid = (N_HEADS // ATTN_HEADS, nblk)
    n_steps = grid[0] * grid[1]
    pps = 0
    if cache is not None and page_table.size % (n_steps * PAGES_PER_BLOCK) == 0:
        pps = page_table.size // n_steps
        if page_table.shape[1] % pps:
            pps = 0

    def imap(f):
        return (lambda sl, qi, pt: f(sl, qi)) if pps else f

    in_specs = [_smem(),
                pl.BlockSpec((ATTN_WIDTH, MOBA_BLOCK), imap(lambda sl, qi: (sl, qi))),
                pl.BlockSpec((s, ATTN_WIDTH), imap(lambda sl, qi: (0, sl)), pipeline_mode=pl.Buffered(1)),
                pl.BlockSpec((nblk, ATTN_WIDTH, MOBA_BLOCK), imap(lambda sl, qi: (0, sl, 0)),
                             pipeline_mode=pl.Buffered(1)),
                pl.BlockSpec((nblk, ATTN_WIDTH), imap(lambda sl, qi: (0, sl))),
                pl.BlockSpec((2, MOBA_BLOCK, MOBA_BLOCK), imap(lambda sl, qi: (0, 0, 0)))]
    out_specs = [pl.BlockSpec((MOBA_BLOCK, ATTN_WIDTH), imap(lambda sl, qi: (qi, sl)))]
    out_shape = [jax.ShapeDtypeStruct((s, D_MODEL), BF16)]
    args = [rel_bias, qt, kb, vt, ksum, _prompt_bucket_maps()]
    if pps:
        bps = pps // PAGES_PER_BLOCK
        in_specs += _page_specs(pps, lambda sl, qi: (sl * nblk + qi) * pps)
        out_specs.append(pl.BlockSpec((1, bps, N_HEADS, HEAD_DIM), lambda sl, qi, pt: (sl * nblk + qi, 0, 0, 0)))
        out_shape.append(jax.ShapeDtypeStruct((n_steps, bps, N_HEADS, HEAD_DIM), F32))
        args = [page_table.reshape(-1)] + args + [cache] * pps
    outs = pl.pallas_call(
        functools.partial(_moba_prompt_kernel, nblk=nblk, pps=pps),
        grid_spec=pltpu.PrefetchScalarGridSpec(
            num_scalar_prefetch=1 if pps else 0,
            grid=grid,
            in_specs=in_specs,
            out_specs=out_specs,
            scratch_shapes=[pltpu.VMEM((ATTN_HEADS, nblk, MOBA_BLOCK), F32),
                            pltpu.VMEM((ATTN_HEADS, 2, MOBA_BLOCK, MOBA_BLOCK), F32),
                            pltpu.VMEM((ATTN_HEADS * KV_GROUP, MOBA_BLOCK, MOBA_BLOCK), F32),
                            pltpu.VMEM((ATTN_HEADS * KV_GROUP, MOBA_BLOCK, MOBA_BLOCK), F32)],
        ),
        out_shape=out_shape,
        compiler_params=_params(2),
        name="moba_prompt",
    )(*args)
    if not pps:
        return outs[0], None
    b, npg = page_table.shape
    return outs[0], outs[1].reshape(b, npg // PAGES_PER_BLOCK, D_MODEL)


def _page_sum_kernel(pt_ref, *refs, pps):
    del pt_ref
    _block_sums(refs[:pps], refs[pps])


def _page_sums(cache, page_table):
    b, npg = page_table.shape
    pps = min(16, npg)
    bps = pps // PAGES_PER_BLOCK
    sums = pl.pallas_call(
        functools.partial(_page_sum_kernel, pps=pps),
        grid_spec=pltpu.PrefetchScalarGridSpec(
            num_scalar_prefetch=1,
            grid=(b, npg // pps),
            in_specs=_page_specs(pps, lambda bi, j: bi * npg + j * pps),
            out_specs=pl.BlockSpec((1, bps, N_HEADS, HEAD_DIM),
                                   lambda bi, j, pt: (bi * (npg // pps) + j, 0, 0, 0)),
        ),
        out_shape=jax.ShapeDtypeStruct((b * (npg // pps), bps, N_HEADS, HEAD_DIM), F32),
        compiler_params=_params(2),
        name="page_sums",
    )(page_table.reshape(-1), *([cache] * pps))
    return sums.reshape(b, npg // PAGES_PER_BLOCK, D_MODEL)


def _sample_gate_kernel(q_ref, bs_ref, seg_ref, sel_ref, *, t_new, nblk):
    kmean = bs_ref[0] * (1.0 / MOBA_BLOCK)
    n_iota = lax.broadcasted_iota(I32, (nblk, LANES), 0)
    sel_ref[...] = jnp.zeros_like(sel_ref)
    for t in range(t_new):
        prod = kmean * q_ref[0, t:t + 1, :]
        scores = _select_matmul(prod, seg_ref[...])
        for r, (idx, _) in enumerate(_top3_rows(scores, n_iota, nblk)):
            sel_ref[0, t, r:r + 1, :] = idx


def _sample_gate(q, bsum):
    b, t_new, _ = q.shape
    nblk = bsum.shape[1]
    seg = (jnp.arange(D_MODEL, dtype=I32)[:, None] // HEAD_DIM
           == jnp.arange(LANES, dtype=I32)[None, :]).astype(BF16)
    sel = pl.pallas_call(
        functools.partial(_sample_gate_kernel, t_new=t_new, nblk=nblk),
        grid=(b,),
        in_specs=[pl.BlockSpec((1, t_new, D_MODEL), lambda i: (i, 0, 0)),
                  pl.BlockSpec((1, nblk, D_MODEL), lambda i: (i, 0, 0)),
                  _whole((D_MODEL, LANES))],
        out_specs=pl.BlockSpec((1, t_new, SUBLANES, LANES), lambda i: (i, 0, 0, 0)),
        out_shape=jax.ShapeDtypeStruct((b, t_new, SUBLANES, LANES), I32),
        compiler_params=_params(1),
        name="sample_gate",
    )(q, bsum, seg)
    return sel[:, :, :MOBA_TOPK, :N_HEADS]


SLABS_PER_HEAD = MOBA_TOPK * PAGES_PER_BLOCK


def _sample_attn_kernel(sel_ref, pt_ref, rb_ref, q_ref, kn_ref, vn_ref, pb_ref, ck_hbm, cv_hbm, o_ref,
                        kbuf, vbuf, sems, o_scr, *, t_new, npg):
    bi, t = pl.program_id(0), pl.program_id(1)
    step = bi * t_new + t
    n_steps = pl.num_programs(0) * t_new
    slot = lax.rem(step, 2)
    last_blk = npg // PAGES_PER_BLOCK - 1

    def selected_block(b_, t_, j, h):
        return sel_ref[((b_ * t_new + t_) * MOBA_TOPK + j) * N_HEADS + h]

    def slab_copies(b_, t_, slot_):
        copies = []
        for h in range(N_HEADS):
            for j in range(MOBA_TOPK):
                blk = selected_block(b_, t_, j, h)
                for pg in range(PAGES_PER_BLOCK):
                    page = pt_ref[b_ * npg + blk * PAGES_PER_BLOCK + pg]
                    i = j * PAGES_PER_BLOCK + pg
                    copies.append(pltpu.make_async_copy(ck_hbm.at[page, h], kbuf.at[slot_, h, i],
                                                        sems.at[0, slot_]))
                    copies.append(pltpu.make_async_copy(cv_hbm.at[page, h], vbuf.at[slot_, h, i],
                                                        sems.at[1, slot_]))
        return copies

    @pl.when(step == 0)
    def _first_fetch():
        for c in slab_copies(bi, t, slot):
            c.start()

    @pl.when(step + 1 < n_steps)
    def _prefetch_next():
        nxt = step + 1
        for c in slab_copies(lax.div(nxt, t_new), lax.rem(nxt, t_new), 1 - slot):
            c.start()

    for c in slab_copies(bi, t, slot):
        c.wait()

    row = lax.broadcasted_iota(I32, (SUBLANES, 1), 0)
    q = q_ref[0, 0] * (HEAD_DIM ** -0.5)
    for h in range(N_HEADS):
        cols = slice(h * HEAD_DIM, (h + 1) * HEAD_DIM)
        far_bias = rb_ref[REL_BUCKETS - 1, h]
        qh = q[:, cols]
        q8 = jnp.broadcast_to(qh, (SUBLANES, HEAD_DIM)).astype(BF16)
        l1 = []
        for j in range(MOBA_TOPK):
            blk_j = selected_block(bi, t, j, h)
            for pg in range(PAGES_PER_BLOCK):
                k_t = kbuf[slot, h, j * PAGES_PER_BLOCK + pg].astype(BF16)
                near = _bias_from_buckets(pb_ref[pl.ds(t * PAGES_PER_BLOCK + pg, 1), :], rb_ref, h)
                l1.append(jnp.dot(q8, k_t, preferred_element_type=F32)[0:1]
                          + jnp.where(blk_j == last_blk, near, far_bias))
        l1 = jnp.concatenate(l1, axis=1)
        l2 = jnp.sum(kn_ref[0, :, cols] * qh, axis=-1, keepdims=True)
        bias2 = jnp.zeros((SUBLANES, 1), F32)
        for u in range(t_new):
            bias2 = jnp.where(row == u, rb_ref[jnp.maximum(t - u, 0), h], bias2)
        l2 = jnp.where(row <= t, l2 + bias2, NEG_INF)
        m = jnp.maximum(jnp.max(l1, axis=-1, keepdims=True), jnp.max(l2, axis=0, keepdims=True))
        p1 = jnp.exp(l1 - m)
        p2 = jnp.exp(l2 - m)
        denom = jnp.sum(p1, axis=-1, keepdims=True) + jnp.sum(p2, axis=0, keepdims=True)
        p8 = jnp.broadcast_to(p1, (SUBLANES, SLABS_PER_HEAD * PAGE_SIZE)).astype(BF16)
        o = jnp.sum(p2 * vn_ref[0, :, cols], axis=0, keepdims=True)
        for i in range(SLABS_PER_HEAD):
            v_t = vbuf[slot, h, i].astype(BF16)
            o = o + lax.dot_general(p8[:, i * PAGE_SIZE:(i + 1) * PAGE_SIZE], v_t, (((1,), (1,)), ((), ())),
                                    preferred_element_type=F32)[0:1]
        o_scr[:, cols] = jnp.broadcast_to(o / denom, (SUBLANES, HEAD_DIM))
    o_ref[0, 0] = o_scr[0:1, :]


def _sample_attn(q, k_new, v_new, cache_k, cache_v, sel, page_table, rel_bias):
    b, t_new, _ = q.shape
    npg = page_table.shape[1]
    pad = ((0, 0), (0, SUBLANES - t_new), (0, 0))
    tt = jnp.arange(t_new, dtype=I32)[:, None, None]
    pg = jnp.arange(PAGES_PER_BLOCK, dtype=I32)[None, :, None]
    rr = jnp.arange(PAGE_SIZE, dtype=I32)[None, None, :]
    near_bkt = _rel_bucket(MOBA_BLOCK + tt - pg * PAGE_SIZE - rr).reshape(t_new * PAGES_PER_BLOCK, PAGE_SIZE)

    new_spec = pl.BlockSpec((1, SUBLANES, D_MODEL), lambda bi, t, *_: (bi, 0, 0))
    tok_spec = pl.BlockSpec((1, 1, 1, D_MODEL), lambda bi, t, *_: (bi, t, 0, 0))
    slab_buf = pltpu.VMEM((2, N_HEADS, SLABS_PER_HEAD, HEAD_DIM, PAGE_SIZE), F32)
    out = pl.pallas_call(
        functools.partial(_sample_attn_kernel, t_new=t_new, npg=npg),
        grid_spec=pltpu.PrefetchScalarGridSpec(
            num_scalar_prefetch=2,
            grid=(b, t_new),
            in_specs=[_smem(), tok_spec, new_spec, new_spec,
                      pl.BlockSpec((t_new * PAGES_PER_BLOCK, PAGE_SIZE), lambda *_: (0, 0)),
                      pl.BlockSpec(memory_space=pl.ANY), pl.BlockSpec(memory_space=pl.ANY)],
            out_specs=tok_spec,
            scratch_shapes=[slab_buf, slab_buf, pltpu.SemaphoreType.DMA((2, 2)),
                            pltpu.VMEM((SUBLANES, D_MODEL), F32)],
        ),
        out_shape=jax.ShapeDtypeStruct((b, t_new, 1, D_MODEL), F32),
        compiler_params=_params(2),
        name="sample_attn",
    )(sel.reshape(-1), page_table.reshape(-1), rel_bias,
      q.reshape(b, t_new, 1, D_MODEL), jnp.pad(k_new, pad), jnp.pad(v_new, pad), near_bkt,
      cache_k, cache_v)
    return out.reshape(b, t_new, D_MODEL)


def _ssd_chunk(t, ext_ref, p_ref, st_ref, gy_ref, cw_ref, cb_ref, dtb_ref, alog_ref, dsk_ref,
               nw_ref, e_ref, wout_ref, n_valid):
    def conv_act(lo, width):
        acc = cb_ref[:, lo:lo + width]
        for j in range(CONV_WIDTH):
            acc = acc + cw_ref[j:j + 1, lo:lo + width] * ext_ref[5 + j:5 + j + t, lo:lo + width]
        return _silu(acc)

    dt = _softplus(p_ref[:, DT_OFF:DT_OFF + LANES] + dtb_ref[...])
    if n_valid < t:
        dt = jnp.where(lax.broadcasted_iota(I32, (t, LANES), 0) < n_valid, dt, 0.0)
    a = dt * -jnp.exp(alog_ref[...])
    r_i = lax.broadcasted_iota(I32, (t, t), 0)
    c_i = lax.broadcasted_iota(I32, (t, t), 1)
    causal = r_i >= c_i
    acs = jnp.dot(causal.astype(F32), a, preferred_element_type=F32,
                  precision=lax.Precision.HIGHEST)
    if t % LANES == 0:
        acs_t = acs.T
    else:
        acs_t = lax.dot_general(jnp.eye(LANES, dtype=F32), acs, (((1,), (1,)), ((), ())),
                                preferred_element_type=F32, precision=lax.Precision.HIGHEST)
    lane_hi = lax.broadcasted_iota(I32, (t, LANES), 1) >= SSM_HEAD_DIM

    for g in range(SSM_GROUPS):
        ch = slice(g * GROUP_DIM, (g + 1) * GROUP_DIM)
        e_g = e_ref[:, ch]
        xs = conv_act(g * GROUP_DIM, GROUP_DIM)
        b_g = conv_act(D_INNER + g * SSM_STATE, SSM_STATE).astype(BF16)
        c_g = conv_act(D_INNER + (SSM_GROUPS + g) * SSM_STATE, SSM_STATE).astype(BF16)
        dt_g = _select_matmul(dt, e_g)
        acs_g = _select_matmul(acs, e_g)
        last_g = acs_g[t - 1:t, :]
        xdt = xs * dt_g
        xdt_b = xdt.astype(BF16)
        xdt_dec = (xdt * jnp.exp(last_g - acs_g)).astype(BF16)
        cb = lax.dot_general(c_g, b_g, (((1,), (1,)), ((), ())), preferred_element_type=F32)
        st_in = st_ref[:, ch]
        y_off = jnp.dot(c_g, st_in.astype(BF16), preferred_element_type=F32) * jnp.exp(acs_g)
        st_new = lax.dot_general(b_g, xdt_dec, (((0,), (0,)), ((), ())), preferred_element_type=F32)
        st_ref[:, ch] = st_in * jnp.exp(last_g) + st_new
        for j in range(HEADS_PER_GROUP // HEADS_PER_SLAB):
            cols = slice(j * LANES, (j + 1) * LANES)
            pair = xdt_b[:, cols]
            ys = []
            for hh in range(HEADS_PER_SLAB):
                h = g * HEADS_PER_GROUP + j * HEADS_PER_SLAB + hh
                decay = jnp.exp(jnp.where(causal, acs[:, h:h + 1] - acs_t[h:h + 1, :], NEG_INF))
                ys.append(jnp.dot((cb * decay).astype(BF16), pair, preferred_element_type=F32))
            y_diag = jnp.where(lane_hi, ys[1], ys[0])
            lo = g * GROUP_DIM + j * LANES
            y = y_diag + y_off[:, cols] + dsk_ref[:, lo:lo + LANES] * xs[:, cols]
            gy_ref[:, lo:lo + LANES] = y * _silu(p_ref[:, lo:lo + LANES])
        gy = gy_ref[:, ch]
        ms = jnp.mean(gy * gy, axis=-1, keepdims=True)
        gy_ref[:, ch] = gy * lax.rsqrt(ms + RMS_EPS) * nw_ref[:, ch]
    return jnp.dot(gy_ref[...].astype(BF16), wout_ref[...], preferred_element_type=F32)


def _ssm_prompt_kernel(p_ref, xres_ref, cw_ref, cb_ref, dtb_ref, alog_ref, dsk_ref, nw_ref, e_ref,
                       wout_ref, y_ref, st_out_ref, cv_out_ref, ext_ref, st_ref, gy_ref):
    c = pl.program_id(0)
    t = SSD_CHUNK

    @pl.when(c == 0)
    def _init():
        ext_ref[t:t + SUBLANES, :] = jnp.zeros((SUBLANES, CONV_DIM), F32)
        st_ref[...] = jnp.zeros_like(st_ref)

    ext_ref[0:SUBLANES, :] = ext_ref[t:t + SUBLANES, :]
    ext_ref[SUBLANES:SUBLANES + t, :] = p_ref[:, D_INNER:D_INNER + CONV_DIM]
    out = _ssd_chunk(t, ext_ref, p_ref, st_ref, gy_ref, cw_ref, cb_ref, dtb_ref, alog_ref, dsk_ref,
                     nw_ref, e_ref, wout_ref, t)
    y_ref[...] = xres_ref[...] + out

    @pl.when(c == pl.num_programs(0) - 1)
    def _finish():
        st_out_ref[...] = st_ref[...].T
        cv_out_ref[...] = ext_ref[t:t + SUBLANES, :]


def _ssm_consts(w):
    conv_w, conv_b, dt_bias, a_log, d_skip, norm_w, w_out = w
    lane_pad = (0, LANES - SSM_HEADS)
    expand = (jnp.arange(LANES, dtype=I32)[:, None]
              == jnp.arange(D_INNER, dtype=I32)[None, :] // SSM_HEAD_DIM).astype(BF16)
    return (conv_w, conv_b.reshape(1, CONV_DIM),
            jnp.pad(dt_bias, lane_pad).reshape(1, LANES), jnp.pad(a_log, lane_pad).reshape(1, LANES),
            jnp.repeat(d_skip, SSM_HEAD_DIM).reshape(1, D_INNER), norm_w.reshape(1, D_INNER),
            expand, w_out)


def _const_specs(consts):
    return [_whole(c.shape) for c in consts]


def _ssm_prompt(proj, xres, w):
    s = proj.shape[0]
    t = SSD_CHUNK
    consts = _ssm_consts(w)
    y, st, cv = pl.pallas_call(
        _ssm_prompt_kernel,
        grid=(s // t,),
        in_specs=[pl.BlockSpec((t, SSM_PROJ_PAD), lambda c: (c, 0)),
                  pl.BlockSpec((t, D_MODEL), lambda c: (c, 0))] + _const_specs(consts),
        out_specs=[pl.BlockSpec((t, D_MODEL), lambda c: (c, 0)),
                   _whole((D_INNER, SSM_STATE)), _whole((SUBLANES, CONV_DIM))],
        out_shape=[jax.ShapeDtypeStruct((s, D_MODEL), F32),
                   jax.ShapeDtypeStruct((D_INNER, SSM_STATE), F32),
                   jax.ShapeDtypeStruct((SUBLANES, CONV_DIM), F32)],
        scratch_shapes=[pltpu.VMEM((t + 2 * SUBLANES, CONV_DIM), F32),
                        pltpu.VMEM((SSM_STATE, D_INNER), F32),
                        pltpu.VMEM((t, D_INNER), F32)],
        compiler_params=_params(1),
        name="ssm_prompt",
    )(proj, xres, *consts)
    return y, st, cv[SUBLANES - (CONV_WIDTH - 1):]


def _ssm_sample_kernel(p_ref, xres_ref, cst_ref, st_in_ref, cw_ref, cb_ref, dtb_ref, alog_ref,
                       dsk_ref, nw_ref, e_ref, wout_ref, y_ref, st_out_ref, cv_out_ref,
                       ext_ref, st_ref, gy_ref, *, t_new):
    t = SAMPLE_T
    ext_ref[0:SUBLANES, :] = cst_ref[0]
    ext_ref[SUBLANES:SUBLANES + t, :] = p_ref[0, :, D_INNER:D_INNER + CONV_DIM]
    st_ref[...] = st_in_ref[0].T
    out = _ssd_chunk(t, ext_ref, p_ref.at[0], st_ref, gy_ref, cw_ref, cb_ref, dtb_ref, alog_ref,
                     dsk_ref, nw_ref, e_ref, wout_ref, t_new)
    y_ref[0] = xres_ref[0] + out[0:t_new]
    st_out_ref[0] = st_ref[...].T
    cv_out_ref[0] = ext_ref[SUBLANES:SUBLANES + t, :]


def _ssm_sample(proj, xres, conv_state, ssm_state, w):
    b, t_new, _ = proj.shape
    t = SAMPLE_T
    consts = _ssm_consts(w)
    proj8 = jnp.pad(proj, ((0, 0), (0, t - t_new), (0, 0)))
    cst8 = jnp.pad(conv_state, ((0, 0), (SUBLANES - (CONV_WIDTH - 1), 0), (0, 0)))
    y, st, cv = pl.pallas_call(
        functools.partial(_ssm_sample_kernel, t_new=t_new),
        grid=(b,),
        in_specs=[pl.BlockSpec((1, t, SSM_PROJ_PAD), lambda i: (i, 0, 0)),
                  pl.BlockSpec((1, t_new, D_MODEL), lambda i: (i, 0, 0)),
                  pl.BlockSpec((1, SUBLANES, CONV_DIM), lambda i: (i, 0, 0)),
                  pl.BlockSpec((1, D_INNER, SSM_STATE), lambda i: (i, 0, 0))] + _const_specs(consts),
        out_specs=[pl.BlockSpec((1, t_new, D_MODEL), lambda i: (i, 0, 0)),
                   pl.BlockSpec((1, D_INNER, SSM_STATE), lambda i: (i, 0, 0)),
                   pl.BlockSpec((1, t, CONV_DIM), lambda i: (i, 0, 0))],
        out_shape=[jax.ShapeDtypeStruct((b, t_new, D_MODEL), F32),
                   jax.ShapeDtypeStruct((b, D_INNER, SSM_STATE), F32),
                   jax.ShapeDtypeStruct((b, t, CONV_DIM), F32)],
        scratch_shapes=[pltpu.VMEM((2 * SUBLANES, CONV_DIM), F32),
                        pltpu.VMEM((SSM_STATE, D_INNER), F32),
                        pltpu.VMEM((t, D_INNER), F32)],
        compiler_params=_params(1),
        name="ssm_sample",
    )(proj8, xres, cst8, ssm_state, *consts)
    return y, st, cv[:, t_new - (CONV_WIDTH - 1):t_new]


def _ffn_weights(w_in, w_out):
    def chunks(w):
        return w.reshape(D_MODEL, N_FF_CHUNKS, FF_CHUNK).transpose(1, 0, 2).astype(BF16)
    return (chunks(w_in[:, :D_FF]), chunks(w_in[:, D_FF:]),
            w_out.reshape(N_FF_CHUNKS, FF_CHUNK, D_MODEL).astype(BF16))


def _row_tile(rows, want=512):
    return want if rows % want == 0 else 128


def kernel(x_prompt, x_sample, cache_k, cache_v, page_table, state_ssm, state_conv, ffn1_norm, ffn1_w_in, ffn1_w_out, mix_norm, ffn2_norm, ffn2_w_in, ffn2_w_out, attn_w_qkv, attn_w_o, rel_bias, ssm_w_in, ssm_conv_w, ssm_conv_b, ssm_dt_bias, ssm_A_log, ssm_D, ssm_norm, ssm_w_out, final_norm):
    bp, s, _ = x_prompt.shape
    bs, t_new, _ = x_sample.shape
    assert bp == 1 and attn_w_qkv.shape[0] == 1 and ssm_w_in.shape[0] == 1
    xp = x_prompt.reshape(s, D_MODEL)
    xs = x_sample.reshape(bs * t_new, D_MODEL)
    tp, ts = _row_tile(s), _row_tile(bs * t_new)
    tf = _row_tile(s, 1024)

    w1 = _ffn_weights(ffn1_w_in[0], ffn1_w_out[0])
    xp = _ffn(xp, ffn1_norm[0], w1, tf)
    xs = _ffn(xs, ffn1_norm[0], w1, ts)

    wqkv = attn_w_qkv[0].astype(BF16)
    wq, wk, wv = (wqkv[:, i * D_MODEL:(i + 1) * D_MODEL] for i in range(3))
    wo = attn_w_o[0].astype(BF16)
    ck = jnp.transpose(cache_k[0], (0, 2, 3, 1))
    cv = jnp.transpose(cache_v[0], (0, 2, 3, 1))
    k_p, v_p, kb, qt, vt, ksum = _qkv_prompt(xp, mix_norm[0], wq.T, wk, wk.T, wv.T, tp)
    o_p, bsum = _moba_prompt(qt, kb, vt, ksum.reshape(s // MOBA_BLOCK, D_MODEL), rel_bias, ck, page_table)
    if bsum is None:
        bsum = _page_sums(ck, page_table)
    xp = _matres(o_p, wo, xp, tp)

    qkv_s = _proj(xs, mix_norm[0], wqkv, ts).reshape(bs, t_new, 3 * D_MODEL)
    q_s, k_s, v_s = (qkv_s[..., i * D_MODEL:(i + 1) * D_MODEL] for i in range(3))
    sel = _sample_gate(q_s, bsum)
    o_s = _sample_attn(q_s, k_s, v_s, ck, cv, sel, page_table, rel_bias)
    xs = _matres(o_s.reshape(bs * t_new, D_MODEL), wo, xs, ts)

    w2 = _ffn_weights(ffn2_w_in[0], ffn2_w_out[0])
    xp = _ffn(xp, ffn2_norm[0], w2, tf)
    xs = _ffn(xs, ffn2_norm[0], w2, ts)

    w1 = _ffn_weights(ffn1_w_in[1], ffn1_w_out[1])
    xp = _ffn(xp, ffn1_norm[1], w1, tf)
    xs = _ffn(xs, ffn1_norm[1], w1, ts)

    w_in = jnp.pad(ssm_w_in[0], ((0, 0), (0, SSM_PROJ_PAD - SSM_IN_DIM))).astype(BF16)
    ssm_w = (ssm_conv_w[0], ssm_conv_b[0], ssm_dt_bias[0], ssm_A_log[0], ssm_D[0], ssm_norm[0],
             ssm_w_out[0].astype(BF16))
    xp, st_p, cv_p = _ssm_prompt(_proj(xp, mix_norm[1], w_in, tp), xp, ssm_w)
    proj_s = _proj(xs, mix_norm[1], w_in, ts).reshape(bs, t_new, SSM_PROJ_PAD)
    xs3, st_s, cv_s = _ssm_sample(proj_s, xs.reshape(bs, t_new, D_MODEL), state_conv[0],
                                  state_ssm[0].reshape(bs, D_INNER, SSM_STATE), ssm_w)
    xs = xs3.reshape(bs * t_new, D_MODEL)

    w2 = _ffn_weights(ffn2_w_in[1], ffn2_w_out[1])
    y_p = _ffn(xp, ffn2_norm[1], w2, tf, final_g=final_norm)
    y_s = _ffn(xs, ffn2_norm[1], w2, ts, final_g=final_norm)

    head_shape = (N_HEADS, HEAD_DIM)
    state_shape = (SSM_HEADS, SSM_HEAD_DIM, SSM_STATE)
    return (y_p.reshape(1, s, D_MODEL), y_s.reshape(bs, t_new, D_MODEL),
            jnp.transpose(k_p, (0, 3, 1, 2))[None, None], jnp.transpose(v_p, (0, 3, 1, 2))[None, None],
            k_s.reshape(1, bs, t_new, *head_shape), v_s.reshape(1, bs, t_new, *head_shape),
            st_p.reshape(1, 1, *state_shape), cv_p.reshape(1, 1, CONV_WIDTH - 1, CONV_DIM),
            st_s.reshape(1, bs, *state_shape), cv_s.reshape(1, bs, CONV_WIDTH - 1, CONV_DIM))
```

```python
import functools
import math

import jax
import jax.numpy as jnp
from jax import lax
from jax.experimental import pallas as pl
from jax.experimental.pallas import tpu as pltpu

F32 = jnp.float32
BF16 = jnp.bfloat16
I32 = jnp.int32
NEG_INF = float("-inf")
LOG2_E = math.log2(math.e)

D_MODEL = 1024
N_HEADS = 16
HEAD_DIM = 64
MOBA_BLOCK = 256
MOBA_TOPK = 3
PAGE_SIZE = 128
REL_BUCKETS = 32
REL_MAX_DIST = 128
D_INNER = 2048
SSM_HEADS = 32
SSM_HEAD_DIM = 64
SSM_GROUPS = 4
SSM_STATE = 128
CONV_WIDTH = 4
CONV_DIM = D_INNER + 2 * SSM_GROUPS * SSM_STATE
SSM_IN_DIM = 2 * D_INNER + 2 * SSM_GROUPS * SSM_STATE + SSM_HEADS
SSD_CHUNK = 256
D_FF = 2816
RMS_EPS = 1e-6

LANES = 128
SUBLANES = 8
VMEM_LIMIT_BYTES = 56 * 1024 * 1024

FF_CHUNK = 256
N_FF_CHUNKS = D_FF // FF_CHUNK
HEADS_PER_SLAB = LANES // HEAD_DIM
N_SLABS = N_HEADS // HEADS_PER_SLAB
GROUP_DIM = D_INNER // SSM_GROUPS
HEADS_PER_GROUP = SSM_HEADS // SSM_GROUPS
SSM_PROJ_PAD = 5248
DT_OFF = D_INNER + CONV_DIM
SAMPLE_T = 8
PAGES_PER_BLOCK = MOBA_BLOCK // PAGE_SIZE
KV_GROUP = 4
ATTN_HEADS = 4
ATTN_WIDTH = ATTN_HEADS * HEAD_DIM
SUM_ROWS = 16


def _params(n_axes, vmem=VMEM_LIMIT_BYTES):
    return pltpu.CompilerParams(dimension_semantics=("arbitrary",) * n_axes,
                                vmem_limit_bytes=vmem)


def _whole(shape):
    nd = len(shape)
    return pl.BlockSpec(shape, lambda *_: (0,) * nd)


def _smem():
    return pl.BlockSpec(memory_space=pltpu.SMEM)


def _rms(x, g):
    ms = jnp.mean(x * x, axis=-1, keepdims=True)
    return x * lax.rsqrt(ms + RMS_EPS) * g


def _silu(x):
    return x * jax.nn.sigmoid(x)


def _softplus(x):
    return jnp.maximum(x, 0.0) + jnp.log1p(jnp.exp(-jnp.abs(x)))


def _split3(x):
    hi = x.astype(BF16)
    r1 = x - hi.astype(F32)
    mid = r1.astype(BF16)
    lo = (r1 - mid.astype(F32)).astype(BF16)
    return hi, mid, lo


def _select_matmul(x, sel):
    hi, mid, lo = _split3(x)
    dot = functools.partial(jnp.dot, preferred_element_type=F32)
    return dot(hi, sel) + dot(mid, sel) + dot(lo, sel)


def _top3_rows(s, n_iota, sentinel):
    picks = []
    for _ in range(MOBA_TOPK):
        m = jnp.max(s, axis=0, keepdims=True)
        idx = jnp.min(jnp.where(s == m, n_iota, sentinel), axis=0, keepdims=True)
        picks.append((idx, m))
        s = jnp.where(n_iota == idx, NEG_INF, s)
    return picks


def _bias_from_buckets(bk, rb_ref, h):
    out = jnp.full(bk.shape, NEG_INF, F32)
    for b in range(REL_BUCKETS):
        out = jnp.where(bk == b, rb_ref[b, h], out)
    return out


def _rel_bucket(dist):
    n = jnp.maximum(dist, 0)
    max_exact = REL_BUCKETS // 2
    nf = jnp.maximum(n, 1).astype(F32)
    large = max_exact + (jnp.log(nf / max_exact) / math.log(REL_MAX_DIST / max_exact)
                         * (REL_BUCKETS - max_exact)).astype(I32)
    large = jnp.minimum(large, REL_BUCKETS - 1)
    return jnp.where(n < max_exact, n, large).astype(I32)


def _ffn_kernel(*refs, final, mixed):
    refs = list(refs)
    x_ref, g_ref, wa_ref, wb_ref, wo_ref = refs[:5]
    o_ref, xn_ref, acc_ref = refs[-3:]
    extra = refs[5:-3]
    if mixed:
        a_ref, wm_ref = extra[:2]
        o_ref[...] = x_ref[...] + jnp.dot(a_ref[...].astype(BF16), wm_ref[...], preferred_element_type=F32)
        x_ref = o_ref
    if final:
        fg_ref = extra[-1]
    xn_ref[...] = _rms(x_ref[...], g_ref[...]).astype(BF16)
    acc_ref[...] = jnp.zeros_like(acc_ref)

    def chunk(c, carry):
        xn = xn_ref[...]
        a = jnp.dot(xn, wa_ref[c], preferred_element_type=F32)
        b = jnp.dot(xn, wb_ref[c], preferred_element_type=F32)
        h = (_silu(a) * b).astype(BF16)
        acc_ref[...] += jnp.dot(h, wo_ref[c], preferred_element_type=F32)
        return carry

    lax.fori_loop(0, N_FF_CHUNKS, chunk, 0)
    y = x_ref[...] + 0.5 * acc_ref[...]
    if final:
        y = _rms(y, fg_ref[...])
    o_ref[...] = y


def _ffn(x, g, w, tm, final_g=None, mix=None):
    rows = x.shape[0]
    wa, wb, wo = w
    row_spec = pl.BlockSpec((tm, D_MODEL), lambda i: (i, 0))
    in_specs = [row_spec, _whole((1, D_MODEL)), _whole(wa.shape), _whole(wb.shape), _whole(wo.shape)]
    args = [x, g.reshape(1, D_MODEL), wa, wb, wo]
    if mix is not None:
        in_specs += [pl.BlockSpec((tm, mix[0].shape[1]), lambda i: (i, 0)), _whole(mix[1].shape)]
        args += list(mix)
    if final_g is not None:
        in_specs.append(_whole((1, D_MODEL)))
        args.append(final_g.reshape(1, D_MODEL))
    return pl.pallas_call(
        functools.partial(_ffn_kernel, final=final_g is not None, mixed=mix is not None),
        grid=(rows // tm,),
        in_specs=in_specs,
        out_specs=row_spec,
        out_shape=jax.ShapeDtypeStruct((rows, D_MODEL), F32),
        scratch_shapes=[pltpu.VMEM((tm, D_MODEL), BF16), pltpu.VMEM((tm, D_MODEL), F32)],
        compiler_params=_params(1),
        name="ffn",
    )(*args)


def _proj_kernel(x_ref, g_ref, w_ref, o_ref, xn_ref, *, widths):
    xn_ref[...] = _rms(x_ref[...], g_ref[...]).astype(BF16)
    lo = 0
    for wd in widths:
        o_ref[:, lo:lo + wd] = jnp.dot(xn_ref[...], w_ref[:, lo:lo + wd], preferred_element_type=F32)
        lo += wd


def _proj(x, g, w, tm):
    rows, n = x.shape[0], w.shape[1]
    col_chunk = 4 * LANES
    widths = [col_chunk] * (n // col_chunk) + ([n % col_chunk] if n % col_chunk else [])
    return pl.pallas_call(
        functools.partial(_proj_kernel, widths=tuple(widths)),
        grid=(rows // tm,),
        in_specs=[pl.BlockSpec((tm, D_MODEL), lambda i: (i, 0)), _whole((1, D_MODEL)), _whole(w.shape)],
        out_specs=pl.BlockSpec((tm, n), lambda i: (i, 0)),
        out_shape=jax.ShapeDtypeStruct((rows, n), F32),
        scratch_shapes=[pltpu.VMEM((tm, D_MODEL), BF16)],
        compiler_params=_params(1),
        name="norm_proj",
    )(x, g.reshape(1, D_MODEL), w)


def _matres_kernel(a_ref, w_ref, r_ref, o_ref):
    o_ref[...] = r_ref[...] + jnp.dot(a_ref[...].astype(BF16), w_ref[...], preferred_element_type=F32)


def _matres(a, w, res, tm):
    rows, k = a.shape
    n = w.shape[1]
    return pl.pallas_call(
        _matres_kernel,
        grid=(rows // tm,),
        in_specs=[pl.BlockSpec((tm, k), lambda i: (i, 0)), _whole(w.shape),
                  pl.BlockSpec((tm, n), lambda i: (i, 0))],
        out_specs=pl.BlockSpec((tm, n), lambda i: (i, 0)),
        out_shape=jax.ShapeDtypeStruct((rows, n), F32),
        compiler_params=_params(1),
        name="matmul_residual",
    )(a, w, res)


def _qkv_prompt_kernel(x_ref, g_ref, wqt_ref, wk_ref, wkt_ref, wvt_ref,
                       kp_ref, vp_ref, kb_ref, qt_ref, vt_ref, ks_ref, xn_ref, *, tm):
    xn_ref[...] = _rms(x_ref[...], g_ref[...]).astype(BF16)
    xn = xn_ref[...]
    k = jnp.dot(xn, wk_ref[...], preferred_element_type=F32)
    kb_ref[...] = k.astype(BF16)
    for j in range(tm // MOBA_BLOCK):
        ks_ref[0, j:j + 1, :] = jnp.sum(k[j * MOBA_BLOCK:(j + 1) * MOBA_BLOCK], axis=0, keepdims=True)
    nt = (((1,), (1,)), ((), ()))
    qt = lax.dot_general(wqt_ref[...], xn, nt, preferred_element_type=F32)
    qt_ref[...] = (qt * (LOG2_E * HEAD_DIM ** -0.5)).astype(BF16)
    kt = lax.dot_general(wkt_ref[...], xn, nt, preferred_element_type=F32)
    vt = lax.dot_general(wvt_ref[...], xn, nt, preferred_element_type=F32)
    for j in range(tm // MOBA_BLOCK):
        vt_ref[j] = vt[:, j * MOBA_BLOCK:(j + 1) * MOBA_BLOCK].astype(BF16)
    for pg in range(tm // PAGE_SIZE):
        rows = slice(pg * PAGE_SIZE, (pg + 1) * PAGE_SIZE)
        kp_ref[pg] = kt[:, rows].reshape(N_HEADS, HEAD_DIM, PAGE_SIZE)
        vp_ref[pg] = vt[:, rows].reshape(N_HEADS, HEAD_DIM, PAGE_SIZE)


def _qkv_prompt(x, g, wqt, wk, wkt, wvt, tm):
    s = x.shape[0]
    nblk = s // MOBA_BLOCK
    bpt = tm // MOBA_BLOCK
    ppt = tm // PAGE_SIZE
    row_spec = pl.BlockSpec((tm, D_MODEL), lambda i: (i, 0))
    page_spec = pl.BlockSpec((ppt, N_HEADS, HEAD_DIM, PAGE_SIZE), lambda i: (i, 0, 0, 0))
    pages = jax.ShapeDtypeStruct((s // PAGE_SIZE, N_HEADS, HEAD_DIM, PAGE_SIZE), F32)
    sq = _whole((D_MODEL, D_MODEL))
    return pl.pallas_call(
        functools.partial(_qkv_prompt_kernel, tm=tm),
        grid=(s // tm,),
        in_specs=[row_spec, _whole((1, D_MODEL)), sq, sq, sq, sq],
        out_specs=[page_spec, page_spec, row_spec,
                   pl.BlockSpec((D_MODEL, tm), lambda i: (0, i)),
                   pl.BlockSpec((bpt, D_MODEL, MOBA_BLOCK), lambda i: (i, 0, 0)),
                   pl.BlockSpec((1, bpt, D_MODEL), lambda i: (i, 0, 0))],
        out_shape=[pages, pages,
                   jax.ShapeDtypeStruct((s, D_MODEL), BF16),
                   jax.ShapeDtypeStruct((D_MODEL, s), BF16),
                   jax.ShapeDtypeStruct((nblk, D_MODEL, MOBA_BLOCK), BF16),
                   jax.ShapeDtypeStruct((s // tm, bpt, D_MODEL), F32)],
        scratch_shapes=[pltpu.VMEM((tm, D_MODEL), BF16)],
        compiler_params=_params(1),
        name="qkv_prompt",
    )(x, g.reshape(1, D_MODEL), wqt, wk, wkt, wvt)


def _block_sums(pages, o_ref):
    for i in range(len(pages) // PAGES_PER_BLOCK):
        tot = pages[PAGES_PER_BLOCK * i][0]
        for j in range(1, PAGES_PER_BLOCK):
            tot = tot + pages[PAGES_PER_BLOCK * i + j][0]
        o_ref[0, i] = jnp.sum(tot, axis=-1)


def _page_specs(n, first_page):
    def spec(i):
        return pl.BlockSpec((1, N_HEADS, HEAD_DIM, PAGE_SIZE),
                            lambda *a: (a[-1][first_page(*a[:-1]) + i], 0, 0, 0))
    return [spec(i) for i in range(n)]


def _softmax_group_step(tiles, v_tiles, state):
    m, acc = state
    m_new = m
    for s, keep, shift in tiles:
        bm = jnp.max(s, axis=0, keepdims=True) + shift
        m_new = jnp.maximum(m_new, bm if keep is None else jnp.where(keep > 0.0, bm, NEG_INF))
    acc = jnp.exp2(m - m_new) * acc
    for (s, keep, shift), v_t in zip(tiles, v_tiles):
        off = m_new - shift
        if keep is not None:
            off = jnp.where(keep > 0.0, off, jnp.inf)
        p = jnp.exp2(s - off)
        acc = acc + jnp.dot(v_t, p.astype(BF16), preferred_element_type=F32)
    return m_new, acc


def _moba_prompt_kernel(*refs, nblk, pps):
    if pps:
        refs = refs[1:]
    rb_ref, qt_ref, kb_ref, vt_ref, ks_ref, bkt_ref = refs[:6]
    pages = refs[6:6 + pps]
    o_ref = refs[6 + pps]
    sel_ref, bias_ref, sa_ref, sb_ref = refs[-4:]
    slab = pl.program_id(0)
    qi = pl.program_id(1)
    blk = MOBA_BLOCK
    heads = range(ATTN_HEADS)

    @pl.when(qi == 0)
    def _build_bias():
        for hh in heads:
            for kind in range(2):
                bias_ref[hh, kind] = LOG2_E * _bias_from_buckets(bkt_ref[kind], rb_ref,
                                                                 slab * ATTN_HEADS + hh)

    if pps:
        _block_sums(pages, refs[7 + pps])
    qt = qt_ref[...].astype(F32)
    row_head = lax.broadcasted_iota(I32, (ATTN_WIDTH, blk), 0) // HEAD_DIM
    n_iota = lax.broadcasted_iota(I32, (nblk, blk), 0)
    kmean = (ks_ref[...] * (1.0 / blk)).astype(BF16)
    far_bias = [LOG2_E * rb_ref[REL_BUCKETS - 1, slab * ATTN_HEADS + hh] for hh in heads]
    qz = [jnp.where(row_head == hh, qt, 0.0).astype(BF16) for hh in heads]
    for hh in heads:
        scores = jnp.dot(kmean, qz[hh], preferred_element_type=F32)
        scores = jnp.where(n_iota < qi, scores, NEG_INF)
        sel = jnp.zeros((nblk, blk), F32)
        for idx, m in _top3_rows(scores, n_iota, nblk):
            sel = jnp.where((n_iota == idx) & (m > NEG_INF), 1.0, sel)
        sel_ref[hh] = sel

    def fill(buf, base):
        for j in range(KV_GROUP):
            nc = jnp.maximum(base + j, 0)
            k_n = kb_ref[pl.ds(pl.multiple_of(nc * blk, blk), blk), :]
            for hh in heads:
                buf[hh * KV_GROUP + j] = jnp.dot(k_n, qz[hh], preferred_element_type=F32)

    def consume(buf, base, kinds, states):
        out = []
        for hh in heads:
            tiles, v_tiles = [], []
            for j, kind in enumerate(kinds):
                n = base + j
                nc = jnp.maximum(n, 0)
                s = buf[hh * KV_GROUP + j]
                v_tiles.append(jnp.concatenate([vt_ref[nc, hh * HEAD_DIM:(hh + 1) * HEAD_DIM, :], ones], axis=0))
                if kind == "own":
                    tiles.append((s + bias_ref[hh, 0], None, 0.0))
                    continue
                keep = jnp.where(n >= 0, sel_ref[hh, pl.ds(nc, 1), :], 0.0)
                if kind == "prev":
                    tiles.append((s + bias_ref[hh, 1], keep, 0.0))
                else:
                    tiles.append((s, keep, far_bias[hh]))
            out.append(_softmax_group_step(tiles, v_tiles, states[hh]))
        return tuple(out)

    near_base = qi - (KV_GROUP - 1)
    near_kinds = ["far"] * (KV_GROUP - 2) + ["prev", "own"]
    far_kinds = ["far"] * KV_GROUP

    def far_base(g):
        return near_base - KV_GROUP * (g + 1)

    ones = jnp.ones((SUM_ROWS, blk), BF16)
    init = tuple((jnp.full((1, blk), NEG_INF, F32), jnp.zeros((HEAD_DIM + SUM_ROWS, blk), F32))
                 for _ in heads)
    fill(sa_ref, near_base)
    fill(sb_ref, far_base(0))
    states = consume(sa_ref, near_base, near_kinds, init)

    def far_pair(i, states):
        g = 2 * i
        fill(sa_ref, far_base(g + 1))
        states = consume(sb_ref, far_base(g), far_kinds, states)
        fill(sb_ref, far_base(g + 2))
        return consume(sa_ref, far_base(g + 1), far_kinds, states)

    n_far_groups = lax.div(jnp.maximum(near_base, 0) + KV_GROUP - 1, KV_GROUP)
    n_pairs = lax.div(n_far_groups, 2)
    states = lax.fori_loop(0, n_pairs, far_pair, states)
    states = lax.cond(lax.rem(n_far_groups, 2) == 1,
                      lambda st: consume(sb_ref, far_base(2 * n_pairs), far_kinds, st),
                      lambda st: st, states)
    outs = [acc[:HEAD_DIM] / acc[HEAD_DIM:HEAD_DIM + 1] for _, acc in states]
    o_ref[...] = jnp.concatenate(outs, axis=0).T.astype(BF16)


def _prompt_bucket_maps():
    kk = jnp.arange(MOBA_BLOCK, dtype=I32)[:, None]
    tt = jnp.arange(MOBA_BLOCK, dtype=I32)[None, :]
    own = jnp.where(tt - kk >= 0, _rel_bucket(tt - kk), -1)
    prev = _rel_bucket(MOBA_BLOCK + tt - kk)
    return jnp.stack([own, prev]).astype(I32)


def _moba_prompt(qt, kb, vt, ksum, rel_bias, cache=None, page_table=None):
    s = kb.shape[0]
    nblk = s // MOBA_BLOCK
    grid = (N_HEADS // ATTN_HEADS, nblk)
    n_steps = grid[0] * grid[1]
    pps = 0
    if cache is not None and page_table.size % (n_steps * PAGES_PER_BLOCK) == 0:
        pps = page_table.size // n_steps
        if page_table.shape[1] % pps:
            pps = 0

    def imap(f):
        return (lambda sl, qi, pt: f(sl, qi)) if pps else f

    in_specs = [_smem(),
                pl.BlockSpec((ATTN_WIDTH, MOBA_BLOCK), imap(lambda sl, qi: (sl, qi))),
                pl.BlockSpec((s, ATTN_WIDTH), imap(lambda sl, qi: (0, sl)), pipeline_mode=pl.Buffered(1)),
                pl.BlockSpec((nblk, ATTN_WIDTH, MOBA_BLOCK), imap(lambda sl, qi: (0, sl, 0)),
                             pipeline_mode=pl.Buffered(1)),
                pl.BlockSpec((nblk, ATTN_WIDTH), imap(lambda sl, qi: (0, sl))),
                pl.BlockSpec((2, MOBA_BLOCK, MOBA_BLOCK), imap(lambda sl, qi: (0, 0, 0)))]
    out_specs = [pl.BlockSpec((MOBA_BLOCK, ATTN_WIDTH), imap(lambda sl, qi: (qi, sl)))]
    out_shape = [jax.ShapeDtypeStruct((s, D_MODEL), BF16)]
    args = [rel_bias, qt, kb, vt, ksum, _prompt_bucket_maps()]
    if pps:
        bps = pps // PAGES_PER_BLOCK
        in_specs += _page_specs(pps, lambda sl, qi: (sl * nblk + qi) * pps)
        out_specs.append(pl.BlockSpec((1, bps, N_HEADS, HEAD_DIM), lambda sl, qi, pt: (sl * nblk + qi, 0, 0, 0)))
        out_shape.append(jax.ShapeDtypeStruct((n_steps, bps, N_HEADS, HEAD_DIM), F32))
        args = [page_table.reshape(-1)] + args + [cache] * pps
    outs = pl.pallas_call(
        functools.partial(_moba_prompt_kernel, nblk=nblk, pps=pps),
        grid_spec=pltpu.PrefetchScalarGridSpec(
            num_scalar_prefetch=1 if pps else 0,
            grid=grid,
            in_specs=in_specs,
            out_specs=out_specs,
            scratch_shapes=[pltpu.VMEM((ATTN_HEADS, nblk, MOBA_BLOCK), F32),
                            pltpu.VMEM((ATTN_HEADS, 2, MOBA_BLOCK, MOBA_BLOCK), F32),
                            pltpu.VMEM((ATTN_HEADS * KV_GROUP, MOBA_BLOCK, MOBA_BLOCK), F32),
                            pltpu.VMEM((ATTN_HEADS * KV_GROUP, MOBA_BLOCK, MOBA_BLOCK), F32)],
        ),
        out_shape=out_shape,
        compiler_params=_params(2),
        name="moba_prompt",
    )(*args)
    if not pps:
        return outs[0], None
    b, npg = page_table.shape
    return outs[0], outs[1].reshape(b, npg // PAGES_PER_BLOCK, D_MODEL)


def _page_sum_kernel(pt_ref, *refs, pps):
    del pt_ref
    _block_sums(refs[:pps], refs[pps])


def _page_sums(cache, page_table):
    b, npg = page_table.shape
    pps = min(16, npg)
    bps = pps // PAGES_PER_BLOCK
    sums = pl.pallas_call(
        functools.partial(_page_sum_kernel, pps=pps),
        grid_spec=pltpu.PrefetchScalarGridSpec(
            num_scalar_prefetch=1,
            grid=(b, npg // pps),
            in_specs=_page_specs(pps, lambda bi, j: bi * npg + j * pps),
            out_specs=pl.BlockSpec((1, bps, N_HEADS, HEAD_DIM),
                                   lambda bi, j, pt: (bi * (npg // pps) + j, 0, 0, 0)),
        ),
        out_shape=jax.ShapeDtypeStruct((b * (npg // pps), bps, N_HEADS, HEAD_DIM), F32),
        compiler_params=_params(2),
        name="page_sums",
    )(page_table.reshape(-1), *([cache] * pps))
    return sums.reshape(b, npg // PAGES_PER_BLOCK, D_MODEL)


def _sample_gate_kernel(q_ref, bs_ref, seg_ref, sel_ref, *, t_new, nblk):
    kmean = bs_ref[0] * (1.0 / MOBA_BLOCK)
    n_iota = lax.broadcasted_iota(I32, (nblk, LANES), 0)
    sel_ref[...] = jnp.zeros_like(sel_ref)
    for t in range(t_new):
        prod = kmean * q_ref[0, t:t + 1, :]
        scores = _select_matmul(prod, seg_ref[...])
        for r, (idx, _) in enumerate(_top3_rows(scores, n_iota, nblk)):
            sel_ref[0, t, r:r + 1, :] = idx


def _sample_gate(q, bsum):
    b, t_new, _ = q.shape
    nblk = bsum.shape[1]
    seg = (jnp.arange(D_MODEL, dtype=I32)[:, None] // HEAD_DIM
           == jnp.arange(LANES, dtype=I32)[None, :]).astype(BF16)
    sel = pl.pallas_call(
        functools.partial(_sample_gate_kernel, t_new=t_new, nblk=nblk),
        grid=(b,),
        in_specs=[pl.BlockSpec((1, t_new, D_MODEL), lambda i: (i, 0, 0)),
                  pl.BlockSpec((1, nblk, D_MODEL), lambda i: (i, 0, 0)),
                  _whole((D_MODEL, LANES))],
        out_specs=pl.BlockSpec((1, t_new, SUBLANES, LANES), lambda i: (i, 0, 0, 0)),
        out_shape=jax.ShapeDtypeStruct((b, t_new, SUBLANES, LANES), I32),
        compiler_params=_params(1),
        name="sample_gate",
    )(q, bsum, seg)
    return sel[:, :, :MOBA_TOPK, :N_HEADS]


SLABS_PER_HEAD = MOBA_TOPK * PAGES_PER_BLOCK


def _sample_attn_kernel(sel_ref, pt_ref, rb_ref, q_ref, kn_ref, vn_ref, pb_ref, ck_hbm, cv_hbm, o_ref,
                        kbuf, vbuf, sems, o_scr, *, t_new, npg):
    bi, t = pl.program_id(0), pl.program_id(1)
    step = bi * t_new + t
    n_steps = pl.num_programs(0) * t_new
    slot = lax.rem(step, 2)
    last_blk = npg // PAGES_PER_BLOCK - 1

    def selected_block(b_, t_, j, h):
        return sel_ref[((b_ * t_new + t_) * MOBA_TOPK + j) * N_HEADS + h]

    def slab_copies(b_, t_, slot_):
        copies = []
        for h in range(N_HEADS):
            for j in range(MOBA_TOPK):
                blk = selected_block(b_, t_, j, h)
                for pg in range(PAGES_PER_BLOCK):
                    page = pt_ref[b_ * npg + blk * PAGES_PER_BLOCK + pg]
                    i = j * PAGES_PER_BLOCK + pg
                    copies.append(pltpu.make_async_copy(ck_hbm.at[page, h], kbuf.at[slot_, h, i],
                                                        sems.at[0, slot_]))
                    copies.append(pltpu.make_async_copy(cv_hbm.at[page, h], vbuf.at[slot_, h, i],
                                                        sems.at[1, slot_]))
        return copies

    @pl.when(step == 0)
    def _first_fetch():
        for c in slab_copies(bi, t, slot):
            c.start()

    @pl.when(step + 1 < n_steps)
    def _prefetch_next():
        nxt = step + 1
        for c in slab_copies(lax.div(nxt, t_new), lax.rem(nxt, t_new), 1 - slot):
            c.start()

    for c in slab_copies(bi, t, slot):
        c.wait()

    row = lax.broadcasted_iota(I32, (SUBLANES, 1), 0)
    q = q_ref[0, 0] * (HEAD_DIM ** -0.5)
    for h in range(N_HEADS):
        cols = slice(h * HEAD_DIM, (h + 1) * HEAD_DIM)
        far_bias = rb_ref[REL_BUCKETS - 1, h]
        qh = q[:, cols]
        q8 = jnp.broadcast_to(qh, (SUBLANES, HEAD_DIM)).astype(BF16)
        l1 = []
        for j in range(MOBA_TOPK):
            blk_j = selected_block(bi, t, j, h)
            for pg in range(PAGES_PER_BLOCK):
                k_t = kbuf[slot, h, j * PAGES_PER_BLOCK + pg].astype(BF16)
                near = _bias_from_buckets(pb_ref[pl.ds(t * PAGES_PER_BLOCK + pg, 1), :], rb_ref, h)
                l1.append(jnp.dot(q8, k_t, preferred_element_type=F32)[0:1]
                          + jnp.where(blk_j == last_blk, near, far_bias))
        l1 = jnp.concatenate(l1, axis=1)
        l2 = jnp.sum(kn_ref[0, :, cols] * qh, axis=-1, keepdims=True)
        bias2 = jnp.zeros((SUBLANES, 1), F32)
        for u in range(t_new):
            bias2 = jnp.where(row == u, rb_ref[jnp.maximum(t - u, 0), h], bias2)
        l2 = jnp.where(row <= t, l2 + bias2, NEG_INF)
        m = jnp.maximum(jnp.max(l1, axis=-1, keepdims=True), jnp.max(l2, axis=0, keepdims=True))
        p1 = jnp.exp(l1 - m)
        p2 = jnp.exp(l2 - m)
        denom = jnp.sum(p1, axis=-1, keepdims=True) + jnp.sum(p2, axis=0, keepdims=True)
        p8 = jnp.broadcast_to(p1, (SUBLANES, SLABS_PER_HEAD * PAGE_SIZE)).astype(BF16)
        o = jnp.sum(p2 * vn_ref[0, :, cols], axis=0, keepdims=True)
        for i in range(SLABS_PER_HEAD):
            v_t = vbuf[slot, h, i].astype(BF16)
            o = o + lax.dot_general(p8[:, i * PAGE_SIZE:(i + 1) * PAGE_SIZE], v_t, (((1,), (1,)), ((), ())),
                                    preferred_element_type=F32)[0:1]
        o_scr[:, cols] = jnp.broadcast_to(o / denom, (SUBLANES, HEAD_DIM))
    o_ref[0, 0] = o_scr[0:1, :]


def _sample_attn(q, k_new, v_new, cache_k, cache_v, sel, page_table, rel_bias):
    b, t_new, _ = q.shape
    npg = page_table.shape[1]
    pad = ((0, 0), (0, SUBLANES - t_new), (0, 0))
    tt = jnp.arange(t_new, dtype=I32)[:, None, None]
    pg = jnp.arange(PAGES_PER_BLOCK, dtype=I32)[None, :, None]
    rr = jnp.arange(PAGE_SIZE, dtype=I32)[None, None, :]
    near_bkt = _rel_bucket(MOBA_BLOCK + tt - pg * PAGE_SIZE - rr).reshape(t_new * PAGES_PER_BLOCK, PAGE_SIZE)

    new_spec = pl.BlockSpec((1, SUBLANES, D_MODEL), lambda bi, t, *_: (bi, 0, 0))
    tok_spec = pl.BlockSpec((1, 1, 1, D_MODEL), lambda bi, t, *_: (bi, t, 0, 0))
    slab_buf = pltpu.VMEM((2, N_HEADS, SLABS_PER_HEAD, HEAD_DIM, PAGE_SIZE), F32)
    out = pl.pallas_call(
        functools.partial(_sample_attn_kernel, t_new=t_new, npg=npg),
        grid_spec=pltpu.PrefetchScalarGridSpec(
            num_scalar_prefetch=2,
            grid=(b, t_new),
            in_specs=[_smem(), tok_spec, new_spec, new_spec,
                      pl.BlockSpec((t_new * PAGES_PER_BLOCK, PAGE_SIZE), lambda *_: (0, 0)),
                      pl.BlockSpec(memory_space=pl.ANY), pl.BlockSpec(memory_space=pl.ANY)],
            out_specs=tok_spec,
            scratch_shapes=[slab_buf, slab_buf, pltpu.SemaphoreType.DMA((2, 2)),
                            pltpu.VMEM((SUBLANES, D_MODEL), F32)],
        ),
        out_shape=jax.ShapeDtypeStruct((b, t_new, 1, D_MODEL), F32),
        compiler_params=_params(2),
        name="sample_attn",
    )(sel.reshape(-1), page_table.reshape(-1), rel_bias,
      q.reshape(b, t_new, 1, D_MODEL), jnp.pad(k_new, pad), jnp.pad(v_new, pad), near_bkt,
      cache_k, cache_v)
    return out.reshape(b, t_new, D_MODEL)


def _ssd_chunk(t, ext_ref, p_ref, st_ref, gy_ref, cw_ref, cb_ref, dtb_ref, alog_ref, dsk_ref,
               nw_ref, e_ref, wout_ref, n_valid):
    def conv_act(lo, width):
        acc = cb_ref[:, lo:lo + width]
        for j in range(CONV_WIDTH):
            acc = acc + cw_ref[j:j + 1, lo:lo + width] * ext_ref[5 + j:5 + j + t, lo:lo + width]
        return _silu(acc)

    dt = _softplus(p_ref[:, DT_OFF:DT_OFF + LANES] + dtb_ref[...])
    if n_valid < t:
        dt = jnp.where(lax.broadcasted_iota(I32, (t, LANES), 0) < n_valid, dt, 0.0)
    a = dt * -jnp.exp(alog_ref[...])
    r_i = lax.broadcasted_iota(I32, (t, t), 0)
    c_i = lax.broadcasted_iota(I32, (t, t), 1)
    causal = r_i >= c_i
    acs = jnp.dot(causal.astype(F32), a, preferred_element_type=F32,
                  precision=lax.Precision.HIGHEST)
    if t % LANES == 0:
        acs_t = acs.T
    else:
        acs_t = lax.dot_general(jnp.eye(LANES, dtype=F32), acs, (((1,), (1,)), ((), ())),
                                preferred_element_type=F32, precision=lax.Precision.HIGHEST)
    lane_hi = lax.broadcasted_iota(I32, (t, LANES), 1) >= SSM_HEAD_DIM

    for g in range(SSM_GROUPS):
        ch = slice(g * GROUP_DIM, (g + 1) * GROUP_DIM)
        e_g = e_ref[:, ch]
        xs = conv_act(g * GROUP_DIM, GROUP_DIM)
        b_g = conv_act(D_INNER + g * SSM_STATE, SSM_STATE).astype(BF16)
        c_g = conv_act(D_INNER + (SSM_GROUPS + g) * SSM_STATE, SSM_STATE).astype(BF16)
        dt_g = _select_matmul(dt, e_g)
        acs_g = _select_matmul(acs, e_g)
        last_g = acs_g[t - 1:t, :]
        xdt = xs * dt_g
        xdt_b = xdt.astype(BF16)
        xdt_dec = (xdt * jnp.exp(last_g - acs_g)).astype(BF16)
        cb = lax.dot_general(c_g, b_g, (((1,), (1,)), ((), ())), preferred_element_type=F32)
        st_in = st_ref[:, ch]
        y_off = jnp.dot(c_g, st_in.astype(BF16), preferred_element_type=F32) * jnp.exp(acs_g)
        st_new = lax.dot_general(b_g, xdt_dec, (((0,), (0,)), ((), ())), preferred_element_type=F32)
        st_ref[:, ch] = st_in * jnp.exp(last_g) + st_new
        for j in range(HEADS_PER_GROUP // HEADS_PER_SLAB):
            cols = slice(j * LANES, (j + 1) * LANES)
            pair = xdt_b[:, cols]
            ys = []
            for hh in range(HEADS_PER_SLAB):
                h = g * HEADS_PER_GROUP + j * HEADS_PER_SLAB + hh
                decay = jnp.exp(jnp.where(causal, acs[:, h:h + 1] - acs_t[h:h + 1, :], NEG_INF))
                ys.append(jnp.dot((cb * decay).astype(BF16), pair, preferred_element_type=F32))
            y_diag = jnp.where(lane_hi, ys[1], ys[0])
            lo = g * GROUP_DIM + j * LANES
            y = y_diag + y_off[:, cols] + dsk_ref[:, lo:lo + LANES] * xs[:, cols]
            gy_ref[:, lo:lo + LANES] = y * _silu(p_ref[:, lo:lo + LANES])
        gy = gy_ref[:, ch]
        ms = jnp.mean(gy * gy, axis=-1, keepdims=True)
        gy_ref[:, ch] = gy * lax.rsqrt(ms + RMS_EPS) * nw_ref[:, ch]
    return jnp.dot(gy_ref[...].astype(BF16), wout_ref[...], preferred_element_type=F32)


def _ssm_prompt_kernel(p_ref, xres_ref, cw_ref, cb_ref, dtb_ref, alog_ref, dsk_ref, nw_ref, e_ref,
                       wout_ref, y_ref, st_out_ref, cv_out_ref, ext_ref, st_ref, gy_ref):
    c = pl.program_id(0)
    t = SSD_CHUNK

    @pl.when(c == 0)
    def _init():
        ext_ref[t:t + SUBLANES, :] = jnp.zeros((SUBLANES, CONV_DIM), F32)
        st_ref[...] = jnp.zeros_like(st_ref)

    ext_ref[0:SUBLANES, :] = ext_ref[t:t + SUBLANES, :]
    ext_ref[SUBLANES:SUBLANES + t, :] = p_ref[:, D_INNER:D_INNER + CONV_DIM]
    out = _ssd_chunk(t, ext_ref, p_ref, st_ref, gy_ref, cw_ref, cb_ref, dtb_ref, alog_ref, dsk_ref,
                     nw_ref, e_ref, wout_ref, t)
    y_ref[...] = xres_ref[...] + out

    @pl.when(c == pl.num_programs(0) - 1)
    def _finish():
        st_out_ref[...] = st_ref[...].T
        cv_out_ref[...] = ext_ref[t:t + SUBLANES, :]


def _ssm_consts(w):
    conv_w, conv_b, dt_bias, a_log, d_skip, norm_w, w_out = w
    lane_pad = (0, LANES - SSM_HEADS)
    expand = (jnp.arange(LANES, dtype=I32)[:, None]
              == jnp.arange(D_INNER, dtype=I32)[None, :] // SSM_HEAD_DIM).astype(BF16)
    return (conv_w, conv_b.reshape(1, CONV_DIM),
            jnp.pad(dt_bias, lane_pad).reshape(1, LANES), jnp.pad(a_log, lane_pad).reshape(1, LANES),
            jnp.repeat(d_skip, SSM_HEAD_DIM).reshape(1, D_INNER), norm_w.reshape(1, D_INNER),
            expand, w_out)


def _const_specs(consts):
    return [_whole(c.shape) for c in consts]


def _ssm_prompt(proj, xres, w):
    s = proj.shape[0]
    t = SSD_CHUNK
    consts = _ssm_consts(w)
    y, st, cv = pl.pallas_call(
        _ssm_prompt_kernel,
        grid=(s // t,),
        in_specs=[pl.BlockSpec((t, SSM_PROJ_PAD), lambda c: (c, 0)),
                  pl.BlockSpec((t, D_MODEL), lambda c: (c, 0))] + _const_specs(consts),
        out_specs=[pl.BlockSpec((t, D_MODEL), lambda c: (c, 0)),
                   _whole((D_INNER, SSM_STATE)), _whole((SUBLANES, CONV_DIM))],
        out_shape=[jax.ShapeDtypeStruct((s, D_MODEL), F32),
                   jax.ShapeDtypeStruct((D_INNER, SSM_STATE), F32),
                   jax.ShapeDtypeStruct((SUBLANES, CONV_DIM), F32)],
        scratch_shapes=[pltpu.VMEM((t + 2 * SUBLANES, CONV_DIM), F32),
                        pltpu.VMEM((SSM_STATE, D_INNER), F32),
                        pltpu.VMEM((t, D_INNER), F32)],
        compiler_params=_params(1),
        name="ssm_prompt",
    )(proj, xres, *consts)
    return y, st, cv[SUBLANES - (CONV_WIDTH - 1):]


def _ssm_sample_kernel(p_ref, xres_ref, cst_ref, st_in_ref, cw_ref, cb_ref, dtb_ref, alog_ref,
                       dsk_ref, nw_ref, e_ref, wout_ref, y_ref, st_out_ref, cv_out_ref,
                       ext_ref, st_ref, gy_ref, *, t_new):
    t = SAMPLE_T
    ext_ref[0:SUBLANES, :] = cst_ref[0]
    ext_ref[SUBLANES:SUBLANES + t, :] = p_ref[0, :, D_INNER:D_INNER + CONV_DIM]
    st_ref[...] = st_in_ref[0].T
    out = _ssd_chunk(t, ext_ref, p_ref.at[0], st_ref, gy_ref, cw_ref, cb_ref, dtb_ref, alog_ref,
                     dsk_ref, nw_ref, e_ref, wout_ref, t_new)
    y_ref[0] = xres_ref[0] + out[0:t_new]
    st_out_ref[0] = st_ref[...].T
    cv_out_ref[0] = ext_ref[SUBLANES:SUBLANES + t, :]


def _ssm_sample(proj, xres, conv_state, ssm_state, w):
    b, t_new, _ = proj.shape
    t = SAMPLE_T
    consts = _ssm_consts(w)
    proj8 = jnp.pad(proj, ((0, 0), (0, t - t_new), (0, 0)))
    cst8 = jnp.pad(conv_state, ((0, 0), (SUBLANES - (CONV_WIDTH - 1), 0), (0, 0)))
    y, st, cv = pl.pallas_call(
        functools.partial(_ssm_sample_kernel, t_new=t_new),
        grid=(b,),
        in_specs=[pl.BlockSpec((1, t, SSM_PROJ_PAD), lambda i: (i, 0, 0)),
                  pl.BlockSpec((1, t_new, D_MODEL), lambda i: (i, 0, 0)),
                  pl.BlockSpec((1, SUBLANES, CONV_DIM), lambda i: (i, 0, 0)),
                  pl.BlockSpec((1, D_INNER, SSM_STATE), lambda i: (i, 0, 0))] + _const_specs(consts),
        out_specs=[pl.BlockSpec((1, t_new, D_MODEL), lambda i: (i, 0, 0)),
                   pl.BlockSpec((1, D_INNER, SSM_STATE), lambda i: (i, 0, 0)),
                   pl.BlockSpec((1, t, CONV_DIM), lambda i: (i, 0, 0))],
        out_shape=[jax.ShapeDtypeStruct((b, t_new, D_MODEL), F32),
                   jax.ShapeDtypeStruct((b, D_INNER, SSM_STATE), F32),
                   jax.ShapeDtypeStruct((b, t, CONV_DIM), F32)],
        scratch_shapes=[pltpu.VMEM((2 * SUBLANES, CONV_DIM), F32),
                        pltpu.VMEM((SSM_STATE, D_INNER), F32),
                        pltpu.VMEM((t, D_INNER), F32)],
        compiler_params=_params(1),
        name="ssm_sample",
    )(proj8, xres, cst8, ssm_state, *consts)
    return y, st, cv[:, t_new - (CONV_WIDTH - 1):t_new]


def _ffn_weights(w_in, w_out):
    def chunks(w):
        return w.reshape(D_MODEL, N_FF_CHUNKS, FF_CHUNK).transpose(1, 0, 2).astype(BF16)
    return (chunks(w_in[:, :D_FF]), chunks(w_in[:, D_FF:]),
            w_out.reshape(N_FF_CHUNKS, FF_CHUNK, D_MODEL).astype(BF16))


def _row_tile(rows, want=512):
    return want if rows % want == 0 else 128


def kernel(x_prompt, x_sample, cache_k, cache_v, page_table, state_ssm, state_conv, ffn1_norm, ffn1_w_in, ffn1_w_out, mix_norm, ffn2_norm, ffn2_w_in, ffn2_w_out, attn_w_qkv, attn_w_o, rel_bias, ssm_w_in, ssm_conv_w, ssm_conv_b, ssm_dt_bias, ssm_A_log, ssm_D, ssm_norm, ssm_w_out, final_norm):
    bp, s, _ = x_prompt.shape
    bs, t_new, _ = x_sample.shape
    assert bp == 1 and attn_w_qkv.shape[0] == 1 and ssm_w_in.shape[0] == 1
    xp = x_prompt.reshape(s, D_MODEL)
    xs = x_sample.reshape(bs * t_new, D_MODEL)
    tp, ts = _row_tile(s), _row_tile(bs * t_new)
    tf = _row_tile(s, 1024)

    w1 = _ffn_weights(ffn1_w_in[0], ffn1_w_out[0])
    xp = _ffn(xp, ffn1_norm[0], w1, tf)
    xs = _ffn(xs, ffn1_norm[0], w1, ts)

    wqkv = attn_w_qkv[0].astype(BF16)
    wq, wk, wv = (wqkv[:, i * D_MODEL:(i + 1) * D_MODEL] for i in range(3))
    wo = attn_w_o[0].astype(BF16)
    ck = jnp.transpose(cache_k[0], (0, 2, 3, 1))
    cv = jnp.transpose(cache_v[0], (0, 2, 3, 1))
    k_p, v_p, kb, qt, vt, ksum = _qkv_prompt(xp, mix_norm[0], wq.T, wk, wk.T, wv.T, tp)
    o_p, bsum = _moba_prompt(qt, kb, vt, ksum.reshape(s // MOBA_BLOCK, D_MODEL), rel_bias, ck, page_table)
    if bsum is None:
        bsum = _page_sums(ck, page_table)

    qkv_s = _proj(xs, mix_norm[0], wqkv, ts).reshape(bs, t_new, 3 * D_MODEL)
    q_s, k_s, v_s = (qkv_s[..., i * D_MODEL:(i + 1) * D_MODEL] for i in range(3))
    sel = _sample_gate(q_s, bsum)
    o_s = _sample_attn(q_s, k_s, v_s, ck, cv, sel, page_table, rel_bias)

    w2 = _ffn_weights(ffn2_w_in[0], ffn2_w_out[0])
    xp = _ffn(xp, ffn2_norm[0], w2, tf, mix=(o_p, wo))
    xs = _ffn(xs, ffn2_norm[0], w2, ts, mix=(o_s.reshape(bs * t_new, D_MODEL), wo))

    w1 = _ffn_weights(ffn1_w_in[1], ffn1_w_out[1])
    xp = _ffn(xp, ffn1_norm[1], w1, tf)
    xs = _ffn(xs, ffn1_norm[1], w1, ts)

    w_in = jnp.pad(ssm_w_in[0], ((0, 0), (0, SSM_PROJ_PAD - SSM_IN_DIM))).astype(BF16)
    ssm_w = (ssm_conv_w[0], ssm_conv_b[0], ssm_dt_bias[0], ssm_A_log[0], ssm_D[0], ssm_norm[0],
             ssm_w_out[0].astype(BF16))
    xp, st_p, cv_p = _ssm_prompt(_proj(xp, mix_norm[1], w_in, tp), xp, ssm_w)
    proj_s = _proj(xs, mix_norm[1], w_in, ts).reshape(bs, t_new, SSM_PROJ_PAD)
    xs3, st_s, cv_s = _ssm_sample(proj_s, xs.reshape(bs, t_new, D_MODEL), state_conv[0],
                                  state_ssm[0].reshape(bs, D_INNER, SSM_STATE), ssm_w)
    xs = xs3.reshape(bs * t_new, D_MODEL)

    w2 = _ffn_weights(ffn2_w_in[1], ffn2_w_out[1])
    y_p = _ffn(xp, ffn2_norm[1], w2, tf, final_g=final_norm)
    y_s = _ffn(xs, ffn2_norm[1], w2, ts, final_g=final_norm)

    head_shape = (N_HEADS, HEAD_DIM)
    state_shape = (SSM_HEADS, SSM_HEAD_DIM, SSM_STATE)
    return (y_p.reshape(1, s, D_MODEL), y_s.reshape(bs, t_new, D_MODEL),
            jnp.transpose(k_p, (0, 3, 1, 2))[None, None], jnp.transpose(v_p, (0, 3, 1, 2))[None, None],
            k_s.reshape(1, bs, t_new, *head_shape), v_s.reshape(1, bs, t_new, *head_shape),
            st_p.reshape(1, 1, *state_shape), cv_p.reshape(1, 1, CONV_WIDTH - 1, CONV_DIM),
            st_s.reshape(1, bs, *state_shape), cv_s.reshape(1, bs, CONV_WIDTH - 1, CONV_DIM))
```
